```python
import math
import jax, jax.numpy as jnp
from jax import lax
import numpy as np

D_MODEL = 1024
BATCH = 8
SEQ = 2048
DEPTH = 4
DEC_BATCH = 128
DEC_SEQ = 8
PAST_LEN = 16384
PAGE_SIZE = 128

GLA_HEADS = 4
GLA_DK = (D_MODEL // 2) // GLA_HEADS
GLA_DV = D_MODEL // GLA_HEADS
GLA_GATE_RANK = 16
GLA_TAU = 16.0
HGRN_EXPAND = 128
HGRN_HEADS = D_MODEL // HGRN_EXPAND
HGRN_DV = D_MODEL // HGRN_HEADS
LB_FLOOR = 1e-20
D_FF = ((8 * D_MODEL // 3 + 255) // 256) * 256
CHUNK = 64
EPS = 1e-6

SPLIT_SIZES = (
    GLA_HEADS * GLA_DK,
    GLA_HEADS * GLA_DK,
    GLA_HEADS * GLA_DV,
    GLA_HEADS * GLA_DV,
    GLA_GATE_RANK,
    HGRN_HEADS * HGRN_EXPAND,
    HGRN_HEADS * HGRN_EXPAND,
    HGRN_HEADS * HGRN_DV,
    HGRN_HEADS * HGRN_DV,
    D_MODEL,
    D_MODEL,
)
D_IN = sum(SPLIT_SIZES)

kernel_name = "gla_hgrn2_parallel_decoder_step"


def rms_norm(x, w):
    xf = x.astype(jnp.float32)
    y = xf * lax.rsqrt(jnp.mean(xf * xf, axis=-1, keepdims=True) + EPS)
    return (y * w.astype(jnp.float32)).astype(x.dtype)


def chunked_gated_linear(q, k, v, log_a, s0):
    B, L, H, dk = q.shape
    dv = v.shape[-1]
    c = math.gcd(L, CHUNK)
    n = L // c

    def to_chunks(t):
        return t.astype(jnp.float32).reshape(B, n, c, H, t.shape[-1]).transpose(1, 0, 3, 2, 4)

    causal = jnp.tril(jnp.ones((c, c), dtype=bool))[None, None, :, :, None]

    def step(S, inp):
        qc, kc, vc, gc = inp
        b = jnp.cumsum(gc, axis=2)
        o_inter = jnp.einsum('bhtk,bhkv->bhtv', qc * jnp.exp(b), S)
        diff = b[:, :, :, None, :] - b[:, :, None, :, :]
        decay = jnp.where(causal, jnp.exp(jnp.where(causal, diff, 0.0)), 0.0)
        scores = jnp.einsum('bhtk,bhsk,bhtsk->bhts', qc, kc, decay)
        o = o_inter + jnp.einsum('bhts,bhsv->bhtv', scores, vc)
        b_last = b[:, :, -1:, :]
        S_new = jnp.exp(b_last[:, :, 0, :])[..., None] * S + jnp.einsum(
            'bhsk,bhsv->bhkv', kc * jnp.exp(b_last - b), vc)
        return S_new, o

    S_fin, o = lax.scan(step, s0.astype(jnp.float32),
                        (to_chunks(q), to_chunks(k), to_chunks(v), to_chunks(log_a)))
    o = o.transpose(1, 0, 3, 2, 4).reshape(B, L, H, dv)
    return o, S_fin


def head_norm_gate(o, w, gate):
    y = o * lax.rsqrt(jnp.mean(o * o, axis=-1, keepdims=True) + EPS) * w.astype(jnp.float32)
    B, L, H, d = o.shape
    g = jax.nn.silu(gate.astype(jnp.float32)).reshape(B, L, H, d)
    return (y * g).reshape(B, L, H * d)


def token_mix(h, w_in_l, w_lr2_l, b_lr_l, gla_onorm_l, log_lb_l, log1m_lb_l,
              hgrn_onorm_l, w_out_l, s_gla, s_hg):
    B, L, _ = h.shape
    split_points = [int(p) for p in np.cumsum(SPLIT_SIZES)[:-1]]
    proj = h @ w_in_l
    (gq, gk, gv, gg, glr, hq, hf, hi, hg, ga, gb) = jnp.split(proj, split_points, axis=-1)

    q = gq.reshape(B, L, GLA_HEADS, GLA_DK) * (GLA_DK ** -0.5)
    k = gk.reshape(B, L, GLA_HEADS, GLA_DK)
    v = gv.reshape(B, L, GLA_HEADS, GLA_DV)
    log_a = jax.nn.log_sigmoid((glr @ w_lr2_l + b_lr_l).astype(jnp.float32)) / GLA_TAU
    log_a = log_a.reshape(B, L, GLA_HEADS, GLA_DK)
    o_gla, s_gla_new = chunked_gated_linear(q, k, v, log_a, s_gla)
    o_gla = head_norm_gate(o_gla, gla_onorm_l, gg)

    z = hf.astype(jnp.float32)
    log_f = jnp.logaddexp(log_lb_l, log1m_lb_l + jax.nn.log_sigmoid(z))
    key = jnp.exp(log1m_lb_l + jax.nn.log_sigmoid(-z))
    q2 = hq.reshape(B, L, HGRN_HEADS, HGRN_EXPAND)
    k2 = key.reshape(B, L, HGRN_HEADS, HGRN_EXPAND)
    lf = log_f.reshape(B, L, HGRN_HEADS, HGRN_EXPAND)
    v2 = hi.reshape(B, L, HGRN_HEADS, HGRN_DV)
    o_hg, s_hg_new = chunked_gated_linear(q2, k2, v2, lf, s_hg)
    o_hg = head_norm_gate(o_hg, hgrn_onorm_l, hg)

    merged = (jax.nn.sigmoid(ga.astype(jnp.float32)) * o_gla
              + jax.nn.sigmoid(gb.astype(jnp.float32)) * o_hg).astype(h.dtype)
    return merged @ w_out_l, s_gla_new, s_hg_new


def swiglu(h, w_ffn_in_l, w_ffn_out_l):
    a, u = jnp.split(h @ w_ffn_in_l, 2, axis=-1)
    return (jax.nn.silu(a) * u) @ w_ffn_out_l


def setup_inputs(seed: int = 0) -> dict:
    key = jax.random.key(seed)
    ks = jax.random.split(key, 16)
    f32 = jnp.float32
    nrm = lambda k, s, sc: jax.random.normal(k, s, f32) * sc
    gain = lambda k, s: 1.0 + 0.05 * jax.random.normal(k, s, f32)
    return {
        "x_prompt": nrm(ks[0], (BATCH, SEQ, D_MODEL), 1.0),
        "x_sample": nrm(ks[1], (DEC_BATCH, DEC_SEQ, D_MODEL), 1.0),
        "state_gla": nrm(ks[2], (DEPTH, DEC_BATCH, GLA_HEADS, GLA_DK, GLA_DV), 0.5),
        "state_hgrn": nrm(ks[3], (DEPTH, DEC_BATCH, HGRN_HEADS, HGRN_EXPAND, HGRN_DV), 0.5),
        "norm_mix_pre": gain(ks[4], (DEPTH, D_MODEL)),
        "norm_mix_post": gain(ks[5], (DEPTH, D_MODEL)),
        "norm_ffn_pre": gain(ks[6], (DEPTH, D_MODEL)),
        "norm_ffn_post": gain(ks[7], (DEPTH, D_MODEL)),
        "w_in": nrm(ks[8], (DEPTH, D_MODEL, D_IN), D_MODEL ** -0.5),
        "gla_w_lr2": nrm(ks[9], (DEPTH, GLA_GATE_RANK, GLA_HEADS * GLA_DK), GLA_GATE_RANK ** -0.5),
        "gla_b_lr": nrm(ks[10], (DEPTH, GLA_HEADS * GLA_DK), 0.1),
        "gla_onorm": gain(ks[11], (DEPTH, GLA_DV)),
        "hgrn_lb": nrm(ks[12], (DEPTH, HGRN_HEADS * HGRN_EXPAND), 0.5),
        "hgrn_onorm": gain(ks[13], (DEPTH, HGRN_DV)),
        "w_out": nrm(ks[14], (DEPTH, D_MODEL, D_MODEL), D_MODEL ** -0.5),
        "w_ffn_in": nrm(jax.random.fold_in(ks[15], 0), (DEPTH, D_MODEL, 2 * D_FF), D_MODEL ** -0.5),
        "w_ffn_out": nrm(jax.random.fold_in(ks[15], 1), (DEPTH, D_FF, D_MODEL), D_FF ** -0.5),
    }


def reference(x_prompt, x_sample, state_gla, state_hgrn, norm_mix_pre, norm_mix_post,
              norm_ffn_pre, norm_ffn_post, w_in, gla_w_lr2, gla_b_lr, gla_onorm,
              hgrn_lb, hgrn_onorm, w_out, w_ffn_in, w_ffn_out):
    sm = jax.nn.softmax(hgrn_lb.astype(jnp.float32), axis=0)
    lb = jnp.cumsum(sm, axis=0) - sm[0:1]
    lb = jnp.clip(lb, 0.0, 1.0 - 1e-6)
    log_lb = jnp.log(jnp.maximum(lb, LB_FLOOR))
    log1m_lb = jnp.log1p(-lb)

    B = x_prompt.shape[0]
    zero_gla = jnp.zeros((B, GLA_HEADS, GLA_DK, GLA_DV), jnp.float32)
    zero_hg = jnp.zeros((B, HGRN_HEADS, HGRN_EXPAND, HGRN_DV), jnp.float32)

    xp, xs = x_prompt, x_sample
    gla_p, gla_s, hg_p, hg_s = [], [], [], []
    for l in range(DEPTH):
        def layer(x, s_gla, s_hg):
            m, s_gla_new, s_hg_new = token_mix(
                rms_norm(x, norm_mix_pre[l]), w_in[l], gla_w_lr2[l], gla_b_lr[l],
                gla_onorm[l], log_lb[l], log1m_lb[l], hgrn_onorm[l], w_out[l], s_gla, s_hg)
            h = x + rms_norm(m, norm_mix_post[l])
            f = swiglu(rms_norm(h, norm_ffn_pre[l]), w_ffn_in[l], w_ffn_out[l])
            return h + rms_norm(f, norm_ffn_post[l]), s_gla_new, s_hg_new

        xp, sgp, shp = layer(xp, zero_gla, zero_hg)
        xs, sgs, shs = layer(xs, state_gla[l], state_hgrn[l])
        gla_p.append(sgp); gla_s.append(sgs); hg_p.append(shp); hg_s.append(shs)

    return (xp, xs, jnp.stack(gla_p), jnp.stack(gla_s), jnp.stack(hg_p), jnp.stack(hg_s))
```

```python
import functools
import math

import jax
import jax.numpy as jnp
from jax import lax
from jax.experimental import pallas as pl
from jax.experimental.pallas import tpu as pltpu

F32 = jnp.float32
BF16 = jnp.bfloat16

D_MODEL = 1024
DEPTH = 4
GLA_HEADS = 4
GLA_DK = 128
GLA_DV = 256
GLA_GATE_RANK = 16
GLA_TAU = 16.0
HGRN_HEADS = 8
HGRN_EXPAND = 128
HGRN_DV = 128
LB_FLOOR = 1e-20
D_FF = 2816
EPS = 1e-6

SUBLANES = 8
LANES = 128
SEQ_PER_ROWGROUP = SUBLANES

COL_GQ = 0
COL_GK = 512
COL_GV = 1024
COL_GG = 2048
COL_HQ = 3072
COL_HF = 4096
COL_HI = 5120
COL_HG = 6144
COL_GA = 7168
COL_GB = 8192
N_PROJ = 9216

PROMPT_CHUNK = 64
VMEM_LIMIT = 52 * 1024 * 1024


def _cparams(sem):
    return pltpu.CompilerParams(dimension_semantics=sem, vmem_limit_bytes=VMEM_LIMIT)


def _rms(x, w):
    return x * lax.rsqrt(jnp.mean(x * x, axis=-1, keepdims=True) + EPS) * w


def _norm_kernel(x_ref, w_ref, o_ref):
    o_ref[...] = _rms(x_ref[...], w_ref[...]).astype(BF16)


def _norm_call(x, w, tm):
    T, D = x.shape
    return pl.pallas_call(
        _norm_kernel,
        grid=(T // tm,),
        in_specs=[pl.BlockSpec((tm, D), lambda i: (i, 0)),
                  pl.BlockSpec((1, D), lambda i: (0, 0))],
        out_specs=pl.BlockSpec((tm, D), lambda i: (i, 0)),
        out_shape=jax.ShapeDtypeStruct((T, D), BF16),
        compiler_params=_cparams(("parallel",)),
        name="prenorm",
    )(x, w.reshape(1, D))


def _matmul_kernel(x_ref, w_ref, o_ref):
    o_ref[...] = jnp.dot(x_ref[...], w_ref[...], preferred_element_type=F32)


def _inproj_call(xn, w, tm, tn):
    T, D = xn.shape
    N = w.shape[1]
    return pl.pallas_call(
        _matmul_kernel,
        grid=(N // tn, T // tm),
        in_specs=[pl.BlockSpec((tm, D), lambda j, i: (i, 0)),
                  pl.BlockSpec((D, tn), lambda j, i: (0, j))],
        out_specs=pl.BlockSpec((tm, tn), lambda j, i: (i, j)),
        out_shape=jax.ShapeDtypeStruct((T, N), F32),
        compiler_params=_cparams(("parallel", "parallel")),
        name="inproj",
    )(xn, w)


def _gla_gate_kernel(x_ref, wlr_ref, wlr2_ref, blr_ref, o_ref):
    glr = jnp.dot(x_ref[...], wlr_ref[...], preferred_element_type=F32)
    y = jnp.dot(glr.astype(BF16), wlr2_ref[...], preferred_element_type=F32) + blr_ref[...]
    ls = -(jnp.maximum(-y, 0.0) + jnp.log1p(jnp.exp(-jnp.abs(y))))
    o_ref[...] = ls * (1.0 / GLA_TAU)


def _gla_gate_call(xn, wlr, wlr2, blr, tm):
    T, D = xn.shape
    R = wlr.shape[1]
    N = wlr2.shape[1]
    return pl.pallas_call(
        _gla_gate_kernel,
        grid=(T // tm,),
        in_specs=[pl.BlockSpec((tm, D), lambda i: (i, 0)),
                  pl.BlockSpec((D, R), lambda i: (0, 0)),
                  pl.BlockSpec((R, N), lambda i: (0, 0)),
                  pl.BlockSpec((1, N), lambda i: (0, 0))],
        out_specs=pl.BlockSpec((tm, N), lambda i: (i, 0)),
        out_shape=jax.ShapeDtypeStruct((T, N), F32),
        compiler_params=_cparams(("parallel",)),
        name="gla_gate",
    )(xn, wlr, wlr2, blr.reshape(1, N))


def _level_halves(C):
    out = []
    m = C // 2
    while m >= 1:
        out.append(m)
        m //= 2
    return out


def _recurrence_kernel(*refs, C, dv, hgrn, carry_state, q_scale):
    nv = dv // LANES
    it = iter(refs)
    q_ref = next(it)
    kz_ref = next(it)
    g_ref = None if hgrn else next(it)
    v_refs = [next(it) for _ in range(nv)]
    gate_ref = next(it)
    onw_ref = next(it)
    lbf_ref = next(it) if hgrn else None
    oml_ref = next(it) if hgrn else None
    s0_ref = None if carry_state else next(it)
    if not carry_state:
        next(it)
    o_ref = next(it)
    sout_ref = next(it)
    b_scr = next(it)
    zs_ref = next(it)
    kk_scr = next(it) if hgrn else None
    oacc_ref = next(it)
    dcol_ref = next(it)
    S_ref = next(it) if carry_state else sout_ref

    levels = _level_halves(C)
    NL = len(levels)
    NS = SEQ_PER_ROWGROUP
    c_idx = pl.program_id(1)

    if carry_state:
        @pl.when(c_idx == 0)
        def _():
            S_ref[...] = jnp.zeros_like(S_ref)
    else:
        S_ref[...] = s0_ref[...]

    def rows(t):
        return pl.ds(t * NS, NS)

    acc = jnp.zeros((NS, LANES), F32)
    for t in range(C):
        if hgrn:
            z = kz_ref[rows(t), :]
            u = jnp.exp(-jnp.abs(z))
            r = 1.0 / (1.0 + u)
            ur = u * r
            pos = z >= 0.0
            sig_p = jnp.where(pos, r, ur)
            sig_n = jnp.where(pos, ur, r)
            oml = oml_ref[...]
            g_t = jnp.log(lbf_ref[...] + oml * sig_p)
            kk_scr[rows(t), :] = oml * sig_n
        else:
            g_t = g_ref[rows(t), :]
        acc = acc + g_t
        b_scr[t] = acc
    b_last = acc
    d_last = jnp.exp(b_last)

    d_t = jnp.transpose(jnp.concatenate([d_last] * (LANES // NS), axis=0))
    for s in range(NS):
        dcol_ref[s] = jnp.broadcast_to(d_t[:, s:s + 1], (LANES, LANES))

    k_src = kk_scr if hgrn else kz_ref
    for t in range(C):
        q_t = q_ref[rows(t), :]
        if q_scale != 1.0:
            q_t = q_t * q_scale
        k_t = k_src[rows(t), :]
        b_t = b_scr[t]
        zs_ref[NL, rows(t), :] = q_t * jnp.exp(b_t)
        zs_ref[NL + 1, rows(t), :] = k_t * jnp.exp(b_last - b_t)
        for j, m in enumerate(levels):
            r_idx = (t // (2 * m)) * (2 * m) + m - 1
            if (t % (2 * m)) >= m:
                zs_ref[j, rows(t), :] = q_t * jnp.exp(b_t - b_scr[r_idx])
            else:
                zs_ref[j, rows(t), :] = k_t * jnp.exp(b_scr[r_idx] - b_t)
        dsc = jnp.sum(q_t * k_t, axis=-1, keepdims=True)
        for i in range(nv):
            oacc_ref[i, rows(t), :] = dsc * v_refs[i][rows(t), :]

    ti = lax.broadcasted_iota(jnp.int32, (C, C), 0)
    si = lax.broadcasted_iota(jnp.int32, (C, C), 1)
    xr = jnp.bitwise_xor(ti, si)
    lower = ti > si
    masks = [lower & (xr >= m) & (xr < 2 * m) for m in levels]

    def seq_rows(s):
        return pl.ds(s, C, stride=NS)

    def per_seq(s, carry):
        sc = jnp.zeros((C, C), F32)
        for j in range(NL):
            z = zs_ref[j, seq_rows(s), :].astype(BF16)
            gram = lax.dot_general(z, z, (((1,), (1,)), ((), ())), preferred_element_type=F32)
            sc = jnp.where(masks[j], gram, sc)
        zq = zs_ref[NL, seq_rows(s), :].astype(BF16)
        zk = zs_ref[NL + 1, seq_rows(s), :].astype(BF16)
        if nv == 1:
            vb = v_refs[0][seq_rows(s), :]
        else:
            vb = jnp.concatenate([v_refs[i][seq_rows(s), :] for i in range(nv)], axis=1)
        vb = vb.astype(BF16)
        S = S_ref[s]
        o = jnp.dot(zq, S.astype(BF16), preferred_element_type=F32)
        o = o + jnp.dot(sc.astype(BF16), vb, preferred_element_type=F32)
        for i in range(nv):
            sl = slice(i * LANES, (i + 1) * LANES)
            oacc_ref[i, seq_rows(s), :] = oacc_ref[i, seq_rows(s), :] + o[:, sl]
        upd = lax.dot_general(zk, vb, (((0,), (0,)), ((), ())), preferred_element_type=F32)
        dc = dcol_ref[s]
        if nv > 1:
            dc = jnp.concatenate([dc] * nv, axis=1)
        S_ref[s] = dc * S + upd
        return carry

    lax.fori_loop(0, NS, per_seq, 0)

    if nv == 1:
        o_all = oacc_ref[0]
    else:
        o_all = jnp.concatenate([oacc_ref[i] for i in range(nv)], axis=1)
    y = o_all * lax.rsqrt(jnp.mean(o_all * o_all, axis=-1, keepdims=True) + EPS) * onw_ref[...]
    gt = gate_ref[...]
    o_ref[...] = y * (gt * (1.0 / (1.0 + jnp.exp(-gt))))

    if carry_state:
        @pl.when(c_idx == pl.num_programs(1) - 1)
        def _():
            sout_ref[...] = S_ref[...]


def _recurrence_call(proj, g_gla, s0, onw, lbf, oml, *, hgrn, C, row0, n_steps, carry_state, out_rows, prev_out=None):
    NS = SEQ_PER_ROWGROUP
    R = NS * C
    H = HGRN_HEADS if hgrn else GLA_HEADS
    dv = HGRN_DV if hgrn else GLA_DV
    dk = LANES
    nv = dv // LANES
    rb0 = row0 // R
    assert row0 % R == 0
    col_q = (COL_HQ if hgrn else COL_GQ) // LANES
    col_k = (COL_HF if hgrn else COL_GK) // LANES
    col_v = (COL_HI if hgrn else COL_GV) // LANES
    col_g = (COL_HG if hgrn else COL_GG) // dv

    def rblk(h, c):
        return rb0 + c

    in_specs = [pl.BlockSpec((R, LANES), lambda h, c: (rblk(h, c), col_q + h)),
                pl.BlockSpec((R, LANES), lambda h, c: (rblk(h, c), col_k + h))]
    args = [proj, proj]
    if not hgrn:
        in_specs.append(pl.BlockSpec((R, LANES), lambda h, c: (rblk(h, c), h)))
        args.append(g_gla)
    for i in range(nv):
        in_specs.append(pl.BlockSpec((R, LANES), lambda h, c, i=i: (rblk(h, c), col_v + h * nv + i)))
        args.append(proj)
    in_specs.append(pl.BlockSpec((R, dv), lambda h, c: (rblk(h, c), col_g + h)))
    args.append(proj)
    in_specs.append(pl.BlockSpec((1, dv), lambda h, c: (0, 0)))
    args.append(onw.reshape(1, dv))
    if hgrn:
        in_specs += [pl.BlockSpec((1, LANES), lambda h, c: (0, h)),
                     pl.BlockSpec((1, LANES), lambda h, c: (0, h))]
        args += [lbf.reshape(1, -1), oml.reshape(1, -1)]
    n_groups = 1 if carry_state else n_steps
    if carry_state:
        s_map = lambda h, c: (0, h, 0, 0)
    else:
        s_map = lambda h, c: (c, h, 0, 0)
        in_specs.append(pl.BlockSpec((NS, None, dk, dv), s_map))
        args.append(s0)
        in_specs.append(pl.BlockSpec(memory_space=pl.ANY))
        args.append(prev_out)
    aliases = {} if carry_state else {len(args) - 1: 0}

    out_specs = [pl.BlockSpec((R, dv), lambda h, c: (rblk(h, c), h)),
                 pl.BlockSpec((NS, None, dk, dv), s_map)]
    out_shape = [jax.ShapeDtypeStruct((out_rows, H * dv), F32),
                 jax.ShapeDtypeStruct((n_groups * NS, H, dk, dv), F32)]
    NL = len(_level_halves(C))
    scratch = [pltpu.VMEM((C, NS, LANES), F32),
               pltpu.VMEM((NL + 2, R, LANES), F32)]
    if hgrn:
        scratch.append(pltpu.VMEM((R, LANES), F32))
    scratch += [pltpu.VMEM((nv, R, LANES), F32),
                pltpu.VMEM((NS, LANES, LANES), F32)]
    if carry_state:
        scratch.append(pltpu.VMEM((NS, dk, dv), F32))

    kern = functools.partial(_recurrence_kernel, C=C, dv=dv, hgrn=hgrn, carry_state=carry_state,
                             q_scale=1.0 if hgrn else GLA_DK ** -0.5)
    return pl.pallas_call(
        kern,
        grid=(H, n_steps),
        in_specs=in_specs,
        out_specs=out_specs,
        out_shape=out_shape,
        scratch_shapes=scratch,
        input_output_aliases=aliases,
        compiler_params=_cparams(("parallel", "arbitrary")),
        name=("hgrn" if hgrn else "gla") + ("_prompt" if carry_state else "_sample"),
    )(*args)


def _outproj_kernel(og_ref, oh_ref, ga_ref, gb_ref, x_ref, wo_ref, npost_ref, npre_ref, h_ref, hn_ref):
    ga = ga_ref[...]
    gb = gb_ref[...]
    merged = og_ref[...] / (1.0 + jnp.exp(-ga)) + oh_ref[...] / (1.0 + jnp.exp(-gb))
    m = jnp.dot(merged.astype(BF16), wo_ref[...], preferred_element_type=F32)
    h = x_ref[...] + _rms(m, npost_ref[...])
    h_ref[...] = h
    hn_ref[...] = _rms(h, npre_ref[...]).astype(BF16)


def _outproj_call(og, oh, proj, x, wo, npost, npre, tm):
    T, D = x.shape
    row = lambda i: (i, 0)
    const = lambda i: (0, 0)
    return pl.pallas_call(
        _outproj_kernel,
        grid=(T // tm,),
        in_specs=[pl.BlockSpec((tm, D), row),
                  pl.BlockSpec((tm, D), row),
                  pl.BlockSpec((tm, D), lambda i: (i, COL_GA // D_MODEL)),
                  pl.BlockSpec((tm, D), lambda i: (i, COL_GB // D_MODEL)),
                  pl.BlockSpec((tm, D), row),
                  pl.BlockSpec((D, D), const),
                  pl.BlockSpec((1, D), const),
                  pl.BlockSpec((1, D), const)],
        out_specs=[pl.BlockSpec((tm, D), row), pl.BlockSpec((tm, D), row)],
        out_shape=[jax.ShapeDtypeStruct((T, D), F32), jax.ShapeDtypeStruct((T, D), BF16)],
        compiler_params=_cparams(("parallel",)),
        name="outproj",
    )(og, oh, proj, proj, x, wo, npost.reshape(1, D), npre.reshape(1, D))


def _ffn_kernel(hn_ref, h_ref, wa_ref, wu_ref, wo_ref, npost_ref, nnext_ref, y_ref, yn_ref, *, n_split):
    hn = hn_ref[...]
    F = wa_ref.shape[1]
    fc = F // n_split
    f = None
    for j in range(n_split):
        sl = slice(j * fc, (j + 1) * fc)
        a = jnp.dot(hn, wa_ref[:, sl], preferred_element_type=F32)
        u = jnp.dot(hn, wu_ref[:, sl], preferred_element_type=F32)
        g = (a * (1.0 / (1.0 + jnp.exp(-a))) * u).astype(BF16)
        part = jnp.dot(g, wo_ref[sl, :], preferred_element_type=F32)
        f = part if f is None else f + part
    y = h_ref[...] + _rms(f, npost_ref[...])
    y_ref[...] = y
    yn_ref[...] = _rms(y, nnext_ref[...]).astype(BF16)


def _ffn_call(hn, h, wa, wu, wo, npost, nnext, tm):
    T, D = h.shape
    F = wa.shape[1]
    row = lambda i: (i, 0)
    const = lambda i: (0, 0)
    once = pl.Buffered(1)
    return pl.pallas_call(
        functools.partial(_ffn_kernel, n_split=2),
        grid=(T // tm,),
        in_specs=[pl.BlockSpec((tm, D), row),
                  pl.BlockSpec((tm, D), row),
                  pl.BlockSpec((D, F), const, pipeline_mode=once),
                  pl.BlockSpec((D, F), const, pipeline_mode=once),
                  pl.BlockSpec((F, D), const, pipeline_mode=once),
                  pl.BlockSpec((1, D), const),
                  pl.BlockSpec((1, D), const)],
        out_specs=[pl.BlockSpec((tm, D), row), pl.BlockSpec((tm, D), row)],
        out_shape=[jax.ShapeDtypeStruct((T, D), F32), jax.ShapeDtypeStruct((T, D), BF16)],
        compiler_params=_cparams(("parallel",)),
        name="ffn",
    )(hn, h, wa, wu, wo, npost.reshape(1, D), nnext.reshape(1, D))


def _to_rows(x):
    B, L, D = x.shape
    NS = SEQ_PER_ROWGROUP
    return x.reshape(B // NS, NS, L, D).transpose(0, 2, 1, 3).reshape(B * L, D)


def _from_rows(r, B, L):
    NS = SEQ_PER_ROWGROUP
    D = r.shape[-1]
    return r.reshape(B // NS, L, NS, D).transpose(0, 2, 1, 3).reshape(B, L, D)


def kernel(x_prompt, x_sample, state_gla, state_hgrn, norm_mix_pre, norm_mix_post, norm_ffn_pre, norm_ffn_post, w_in, gla_w_lr2, gla_b_lr, gla_onorm, hgrn_lb, hgrn_onorm, w_out, w_ffn_in, w_ffn_out):
    B, L, D = x_prompt.shape
    BS, LS, _ = x_sample.shape
    NS = SEQ_PER_ROWGROUP
    assert B == NS and BS % NS == 0 and L % PROMPT_CHUNK == 0
    TP = B * L
    TS = BS * LS
    T = TP + TS
    TM = 1024
    assert T % TM == 0

    sm = jax.nn.softmax(hgrn_lb.astype(F32), axis=0)
    lb = jnp.clip(jnp.cumsum(sm, axis=0) - sm[0:1], 0.0, 1.0 - 1e-6)
    lbf = jnp.maximum(lb, LB_FLOOR)
    oml = 1.0 - lb

    c_lr0 = 2 * GLA_HEADS * GLA_DK + 2 * GLA_HEADS * GLA_DV
    c_lr1 = c_lr0 + GLA_GATE_RANK
    w_main = jnp.concatenate([w_in[:, :, :c_lr0], w_in[:, :, c_lr1:]], axis=2).astype(BF16)
    w_lr = jnp.pad(w_in[:, :, c_lr0:c_lr1], ((0, 0), (0, 0), (0, LANES - GLA_GATE_RANK))).astype(BF16)
    w_lr2 = jnp.pad(gla_w_lr2, ((0, 0), (0, LANES - GLA_GATE_RANK), (0, 0))).astype(BF16)
    w_o = w_out.astype(BF16)
    w_a = w_ffn_in[:, :, :D_FF].astype(BF16)
    w_u = w_ffn_in[:, :, D_FF:].astype(BF16)
    w_f = w_ffn_out.astype(BF16)

    x = jnp.concatenate([_to_rows(x_prompt), _to_rows(x_sample)], axis=0)
    xn = _norm_call(x, norm_mix_pre[0], TM)

    n_sample_groups = BS // NS
    gla_p, gla_s, hg_p, hg_s = [], [], [], []
    for l in range(DEPTH):
        proj = _inproj_call(xn, w_main[l], TM, 1024)
        g_gla = _gla_gate_call(xn, w_lr[l], w_lr2[l], gla_b_lr[l], TM)

        og, sg_p = _recurrence_call(proj, g_gla, None, gla_onorm[l], None, None, hgrn=False,
                                    C=PROMPT_CHUNK, row0=0, n_steps=L // PROMPT_CHUNK,
                                    carry_state=True, out_rows=T)
        og, sg_s = _recurrence_call(proj, g_gla, state_gla[l], gla_onorm[l], None, None, hgrn=False,
                                    C=LS, row0=TP, n_steps=n_sample_groups,
                                    carry_state=False, out_rows=T, prev_out=og)
        oh, sh_p = _recurrence_call(proj, None, None, hgrn_onorm[l], lbf[l], oml[l], hgrn=True,
                                    C=PROMPT_CHUNK, row0=0, n_steps=L // PROMPT_CHUNK,
                                    carry_state=True, out_rows=T)
        oh, sh_s = _recurrence_call(proj, None, state_hgrn[l], hgrn_onorm[l], lbf[l], oml[l], hgrn=True,
                                    C=LS, row0=TP, n_steps=n_sample_groups,
                                    carry_state=False, out_rows=T, prev_out=oh)

        h, hn = _outproj_call(og, oh, proj, x, w_o[l], norm_mix_post[l], norm_ffn_pre[l], TM // 2)
        nnext = norm_mix_pre[l + 1] if l + 1 < DEPTH else norm_mix_pre[0]
        x, xn = _ffn_call(hn, h, w_a[l], w_u[l], w_f[l], norm_ffn_post[l], nnext, TM // 2)

        gla_p.append(sg_p); gla_s.append(sg_s); hg_p.append(sh_p); hg_s.append(sh_s)

    y_p = _from_rows(x[:TP], B, L)
    y_s = _from_rows(x[TP:], BS, LS)
    return (y_p, y_s, jnp.stack(gla_p), jnp.stack(gla_s), jnp.stack(hg_p), jnp.stack(hg_s))
```

```python
import functools
import math

import jax
import jax.numpy as jnp
from jax import lax
from jax.experimental import pallas as pl
from jax.experimental.pallas import tpu as pltpu

F32 = jnp.float32
BF16 = jnp.bfloat16

D_MODEL = 1024
DEPTH = 4
GLA_HEADS = 4
GLA_DK = 128
GLA_DV = 256
GLA_GATE_RANK = 16
GLA_TAU = 16.0
HGRN_HEADS = 8
HGRN_EXPAND = 128
HGRN_DV = 128
LB_FLOOR = 1e-20
D_FF = 2816
EPS = 1e-6
LOG2E = 1.4426950408889634

SUBLANES = 8
LANES = 128
SEQ_PER_ROWGROUP = SUBLANES

COL_GQ = 0
COL_GK = 512
COL_GV = 1024
COL_GG = 2048
COL_HQ = 3072
COL_HF = 4096
COL_HI = 5120
COL_HG = 6144
COL_GA = 7168
COL_GB = 8192
N_PROJ = 9216

PROMPT_CHUNK = 64
VMEM_LIMIT = 52 * 1024 * 1024


def _cparams(sem):
    return pltpu.CompilerParams(dimension_semantics=sem, vmem_limit_bytes=VMEM_LIMIT)


def _rms(x, w):
    return x * lax.rsqrt(jnp.mean(x * x, axis=-1, keepdims=True) + EPS) * w


def _norm_kernel(x_ref, w_ref, o_ref):
    o_ref[...] = _rms(x_ref[...], w_ref[...]).astype(BF16)


def _norm_call(x, w, tm):
    T, D = x.shape
    return pl.pallas_call(
        _norm_kernel,
        grid=(T // tm,),
        in_specs=[pl.BlockSpec((tm, D), lambda i: (i, 0)),
                  pl.BlockSpec((1, D), lambda i: (0, 0))],
        out_specs=pl.BlockSpec((tm, D), lambda i: (i, 0)),
        out_shape=jax.ShapeDtypeStruct((T, D), BF16),
        compiler_params=_cparams(("parallel",)),
        name="prenorm",
    )(x, w.reshape(1, D))


def _matmul_kernel(x_ref, w_ref, o_ref):
    o_ref[...] = jnp.dot(x_ref[...], w_ref[...], preferred_element_type=F32)


def _inproj_call(xn, w, tm, tn):
    T, D = xn.shape
    N = w.shape[1]
    return pl.pallas_call(
        _matmul_kernel,
        grid=(N // tn, T // tm),
        in_specs=[pl.BlockSpec((tm, D), lambda j, i: (i, 0)),
                  pl.BlockSpec((D, tn), lambda j, i: (0, j))],
        out_specs=pl.BlockSpec((tm, tn), lambda j, i: (i, j)),
        out_shape=jax.ShapeDtypeStruct((T, N), F32),
        compiler_params=_cparams(("parallel", "parallel")),
        name="inproj",
    )(xn, w)


def _gla_gate_kernel(x_ref, wlr_ref, wlr2_ref, blr_ref, o_ref):
    glr = jnp.dot(x_ref[...], wlr_ref[...], preferred_element_type=F32)
    y = jnp.dot(glr.astype(BF16), wlr2_ref[...], preferred_element_type=F32) + blr_ref[...]
    ls = -(jnp.maximum(-y, 0.0) + jnp.log1p(jnp.exp(-jnp.abs(y))))
    o_ref[...] = ls * (1.0 / GLA_TAU)


def _gla_gate_call(xn, wlr, wlr2, blr, tm):
    T, D = xn.shape
    R = wlr.shape[1]
    N = wlr2.shape[1]
    return pl.pallas_call(
        _gla_gate_kernel,
        grid=(T // tm,),
        in_specs=[pl.BlockSpec((tm, D), lambda i: (i, 0)),
                  pl.BlockSpec((D, R), lambda i: (0, 0)),
                  pl.BlockSpec((R, N), lambda i: (0, 0)),
                  pl.BlockSpec((1, N), lambda i: (0, 0))],
        out_specs=pl.BlockSpec((tm, N), lambda i: (i, 0)),
        out_shape=jax.ShapeDtypeStruct((T, N), F32),
        compiler_params=_cparams(("parallel",)),
        name="gla_gate",
    )(xn, wlr, wlr2, blr.reshape(1, N))


def _level_halves(C):
    out = []
    m = C // 2
    while m >= 1:
        out.append(m)
        m //= 2
    return out


def _recurrence_kernel(*refs, C, dv, hgrn, carry_state, q_scale, n_aliased):
    nv = dv // LANES
    it = iter(refs)
    q_ref = next(it)
    kz_ref = next(it)
    g_ref = None if hgrn else next(it)
    v_refs = [next(it) for _ in range(nv)]
    gate_ref = next(it)
    onw_ref = next(it)
    lbf_ref = next(it) if hgrn else None
    oml_ref = next(it) if hgrn else None
    s0_ref = None if carry_state else next(it)
    for _ in range(n_aliased):
        next(it)
    o_ref = next(it)
    sout_ref = next(it)
    b_scr = next(it)
    zs_ref = next(it)
    kk_scr = next(it) if hgrn else None
    oacc_ref = next(it)
    dcol_ref = next(it)
    S_ref = next(it) if carry_state else sout_ref

    levels = _level_halves(C)
    NL = len(levels)
    NS = SEQ_PER_ROWGROUP
    c_idx = pl.program_id(1)

    if carry_state:
        @pl.when(c_idx == 0)
        def _():
            S_ref[...] = jnp.zeros_like(S_ref)
    else:
        S_ref[...] = s0_ref[...]

    def rows(t):
        return pl.ds(t * NS, NS)

    acc = jnp.zeros((NS, LANES), F32)
    for t in range(C):
        if hgrn:
            z = kz_ref[rows(t), :]
            u = jnp.exp(-jnp.abs(z))
            r = 1.0 / (1.0 + u)
            ur = u * r
            pos = z >= 0.0
            sig_p = jnp.where(pos, r, ur)
            sig_n = jnp.where(pos, ur, r)
            oml = oml_ref[...]
            g_t = jnp.log(lbf_ref[...] + oml * sig_p)
            kk_scr[rows(t), :] = oml * sig_n
        else:
            g_t = g_ref[rows(t), :]
        acc = acc + g_t
        b_scr[t] = acc * LOG2E
    b_last = acc * LOG2E
    d_last = jnp.exp2(b_last)

    d_t = jnp.transpose(jnp.concatenate([d_last] * (LANES // NS), axis=0))
    for s in range(NS):
        dcol_ref[s] = jnp.broadcast_to(d_t[:, s:s + 1], (LANES, LANES))

    k_src = kk_scr if hgrn else kz_ref
    for t in range(C):
        q_t = q_ref[rows(t), :]
        if q_scale != 1.0:
            q_t = q_t * q_scale
        k_t = k_src[rows(t), :]
        b_t = b_scr[t]
        zs_ref[NL, rows(t), :] = q_t * jnp.exp2(b_t)
        zs_ref[NL + 1, rows(t), :] = k_t * jnp.exp2(b_last - b_t)
        for j, m in enumerate(levels):
            r_idx = (t // (2 * m)) * (2 * m) + m - 1
            if (t % (2 * m)) >= m:
                zs_ref[j, rows(t), :] = q_t * jnp.exp2(b_t - b_scr[r_idx])
            else:
                zs_ref[j, rows(t), :] = k_t * jnp.exp2(b_scr[r_idx] - b_t)
        dsc = jnp.sum(q_t * k_t, axis=-1, keepdims=True)
        for i in range(nv):
            oacc_ref[i, rows(t), :] = dsc * v_refs[i][rows(t), :]

    ti = lax.broadcasted_iota(jnp.int32, (C, C), 0)
    si = lax.broadcasted_iota(jnp.int32, (C, C), 1)
    xr = jnp.bitwise_xor(ti, si)
    lower = ti > si
    masks = [lower & (xr >= m) & (xr < 2 * m) for m in levels]

    def seq_rows(s):
        return pl.ds(s, C, stride=NS)

    def load_v(s):
        if nv == 1:
            vb = v_refs[0][seq_rows(s), :]
        else:
            vb = jnp.concatenate([v_refs[i][seq_rows(s), :] for i in range(nv)], axis=1)
        return vb.astype(BF16)

    scores = []
    for s in range(NS):
        sc = jnp.zeros((C, C), F32)
        for j in range(NL):
            z = zs_ref[j, seq_rows(s), :].astype(BF16)
            gram = lax.dot_general(z, z, (((1,), (1,)), ((), ())), preferred_element_type=F32)
            sc = jnp.where(masks[j], gram, sc)
        scores.append(sc.astype(BF16))
    inter = []
    for s in range(NS):
        zq = zs_ref[NL, seq_rows(s), :].astype(BF16)
        zk = zs_ref[NL + 1, seq_rows(s), :].astype(BF16)
        S = S_ref[s]
        inter.append(jnp.dot(zq, S.astype(BF16), preferred_element_type=F32))
        upd = lax.dot_general(zk, load_v(s), (((0,), (0,)), ((), ())), preferred_element_type=F32)
        dc = dcol_ref[s]
        if nv > 1:
            dc = jnp.concatenate([dc] * nv, axis=1)
        S_ref[s] = dc * S + upd
    for s in range(NS):
        o = inter[s] + jnp.dot(scores[s], load_v(s), preferred_element_type=F32)
        for i in range(nv):
            sl = slice(i * LANES, (i + 1) * LANES)
            oacc_ref[i, seq_rows(s), :] = oacc_ref[i, seq_rows(s), :] + o[:, sl]

    if nv == 1:
        o_all = oacc_ref[0]
    else:
        o_all = jnp.concatenate([oacc_ref[i] for i in range(nv)], axis=1)
    y = o_all * lax.rsqrt(jnp.mean(o_all * o_all, axis=-1, keepdims=True) + EPS) * onw_ref[...]
    gt = gate_ref[...]
    o_ref[...] = y * (gt * (1.0 / (1.0 + jnp.exp(-gt))))

    if carry_state:
        @pl.when(c_idx == pl.num_programs(1) - 1)
        def _():
            sout_ref[...] = S_ref[...]


def _recurrence_call(proj, g_gla, s_in, onw, lbf, oml, *, layer, hgrn, C, row0, n_steps, carry_state, out_rows,
                     prev_out=None, prev_states=None):
    NS = SEQ_PER_ROWGROUP
    R = NS * C
    H = HGRN_HEADS if hgrn else GLA_HEADS
    dv = HGRN_DV if hgrn else GLA_DV
    dk = LANES
    nv = dv // LANES
    assert row0 % R == 0
    rb0 = row0 // R
    col_q = (COL_HQ if hgrn else COL_GQ) // LANES
    col_k = (COL_HF if hgrn else COL_GK) // LANES
    col_v = (COL_HI if hgrn else COL_GV) // LANES
    col_g = (COL_HG if hgrn else COL_GG) // dv

    in_specs = [pl.BlockSpec((R, LANES), lambda h, c: (rb0 + c, col_q + h)),
                pl.BlockSpec((R, LANES), lambda h, c: (rb0 + c, col_k + h))]
    args = [proj, proj]
    if not hgrn:
        in_specs.append(pl.BlockSpec((R, LANES), lambda h, c: (rb0 + c, h)))
        args.append(g_gla)
    for i in range(nv):
        in_specs.append(pl.BlockSpec((R, LANES), lambda h, c, i=i: (rb0 + c, col_v + h * nv + i)))
        args.append(proj)
    in_specs.append(pl.BlockSpec((R, dv), lambda h, c: (rb0 + c, col_g + h)))
    args.append(proj)
    in_specs.append(pl.BlockSpec((1, dv), lambda h, c: (0, 0)))
    args.append(onw.reshape(1, dv))
    if hgrn:
        in_specs += [pl.BlockSpec((1, LANES), lambda h, c: (0, h)),
                     pl.BlockSpec((1, LANES), lambda h, c: (0, h))]
        args += [lbf.reshape(1, -1), oml.reshape(1, -1)]
    n_groups = 1 if carry_state else n_steps
    if carry_state:
        s_map = lambda h, c: (layer, 0, h, 0, 0)
    else:
        s_map = lambda h, c: (layer, c, h, 0, 0)
        in_specs.append(pl.BlockSpec((None, NS, None, dk, dv), s_map))
        args.append(s_in)
    aliases = {}
    for k, prev in enumerate((prev_out, prev_states)):
        if prev is not None:
            in_specs.append(pl.BlockSpec(memory_space=pl.ANY))
            args.append(prev)
            aliases[len(args) - 1] = k

    out_specs = [pl.BlockSpec((R, dv), lambda h, c: (rb0 + c, h)),
                 pl.BlockSpec((None, NS, None, dk, dv), s_map)]
    out_shape = [jax.ShapeDtypeStruct((out_rows, H * dv), F32),
                 jax.ShapeDtypeStruct((DEPTH, n_groups * NS, H, dk, dv), F32)]
    NL = len(_level_halves(C))
    scratch = [pltpu.VMEM((C, NS, LANES), F32),
               pltpu.VMEM((NL + 2, R, LANES), F32)]
    if hgrn:
        scratch.append(pltpu.VMEM((R, LANES), F32))
    scratch += [pltpu.VMEM((nv, R, LANES), F32),
                pltpu.VMEM((NS, LANES, LANES), F32)]
    if carry_state:
        scratch.append(pltpu.VMEM((NS, dk, dv), F32))

    kern = functools.partial(_recurrence_kernel, C=C, dv=dv, hgrn=hgrn, carry_state=carry_state,
                             q_scale=1.0 if hgrn else GLA_DK ** -0.5, n_aliased=len(aliases))
    return pl.pallas_call(
        kern,
        grid=(H, n_steps),
        in_specs=in_specs,
        out_specs=out_specs,
        out_shape=out_shape,
        scratch_shapes=scratch,
        input_output_aliases=aliases,
        compiler_params=_cparams(("parallel", "arbitrary")),
        name=("hgrn" if hgrn else "gla") + ("_prompt" if carry_state else "_sample"),
    )(*args)


def _outproj_kernel(og_ref, oh_ref, ga_ref, gb_ref, x_ref, wo_ref, npost_ref, npre_ref, h_ref, hn_ref):
    ga = ga_ref[...]
    gb = gb_ref[...]
    merged = og_ref[...] / (1.0 + jnp.exp(-ga)) + oh_ref[...] / (1.0 + jnp.exp(-gb))
    m = jnp.dot(merged.astype(BF16), wo_ref[...], preferred_element_type=F32)
    h = x_ref[...] + _rms(m, npost_ref[...])
    h_ref[...] = h
    hn_ref[...] = _rms(h, npre_ref[...]).astype(BF16)


def _outproj_call(og, oh, proj, x, wo, npost, npre, tm):
    T, D = x.shape
    row = lambda i: (i, 0)
    const = lambda i: (0, 0)
    return pl.pallas_call(
        _outproj_kernel,
        grid=(T // tm,),
        in_specs=[pl.BlockSpec((tm, D), row),
                  pl.BlockSpec((tm, D), row),
                  pl.BlockSpec((tm, D), lambda i: (i, COL_GA // D_MODEL)),
                  pl.BlockSpec((tm, D), lambda i: (i, COL_GB // D_MODEL)),
                  pl.BlockSpec((tm, D), row),
                  pl.BlockSpec((D, D), const),
                  pl.BlockSpec((1, D), const),
                  pl.BlockSpec((1, D), const)],
        out_specs=[pl.BlockSpec((tm, D), row), pl.BlockSpec((tm, D), row)],
        out_shape=[jax.ShapeDtypeStruct((T, D), F32), jax.ShapeDtypeStruct((T, D), BF16)],
        compiler_params=_cparams(("parallel",)),
        name="outproj",
    )(og, oh, proj, proj, x, wo, npost.reshape(1, D), npre.reshape(1, D))


def _ffn_kernel(hn_ref, h_ref, wa_ref, wu_ref, wo_ref, npost_ref, nnext_ref, y_ref, yn_ref, *, n_split):
    hn = hn_ref[...]
    F = wa_ref.shape[1]
    fc = F // n_split
    f = None
    for j in range(n_split):
        sl = slice(j * fc, (j + 1) * fc)
        a = jnp.dot(hn, wa_ref[:, sl], preferred_element_type=F32)
        u = jnp.dot(hn, wu_ref[:, sl], preferred_element_type=F32)
        g = (a * (1.0 / (1.0 + jnp.exp(-a))) * u).astype(BF16)
        part = jnp.dot(g, wo_ref[sl, :], preferred_element_type=F32)
        f = part if f is None else f + part
    y = h_ref[...] + _rms(f, npost_ref[...])
    y_ref[...] = y
    yn_ref[...] = _rms(y, nnext_ref[...]).astype(BF16)


def _ffn_call(hn, h, wa, wu, wo, npost, nnext, tm):
    T, D = h.shape
    F = wa.shape[1]
    row = lambda i: (i, 0)
    const = lambda i: (0, 0)
    once = pl.Buffered(1)
    return pl.pallas_call(
        functools.partial(_ffn_kernel, n_split=2),
        grid=(T // tm,),
        in_specs=[pl.BlockSpec((tm, D), row),
                  pl.BlockSpec((tm, D), row),
                  pl.BlockSpec((D, F), const, pipeline_mode=once),
                  pl.BlockSpec((D, F), const, pipeline_mode=once),
                  pl.BlockSpec((F, D), const, pipeline_mode=once),
                  pl.BlockSpec((1, D), const),
                  pl.BlockSpec((1, D), const)],
        out_specs=[pl.BlockSpec((tm, D), row), pl.BlockSpec((tm, D), row)],
        out_shape=[jax.ShapeDtypeStruct((T, D), F32), jax.ShapeDtypeStruct((T, D), BF16)],
        compiler_params=_cparams(("parallel",)),
        name="ffn",
    )(hn, h, wa, wu, wo, npost.reshape(1, D), nnext.reshape(1, D))


def _to_rows(x):
    B, L, D = x.shape
    NS = SEQ_PER_ROWGROUP
    return x.reshape(B // NS, NS, L, D).transpose(0, 2, 1, 3).reshape(B * L, D)


def _from_rows(r, B, L):
    NS = SEQ_PER_ROWGROUP
    D = r.shape[-1]
    return r.reshape(B // NS, L, NS, D).transpose(0, 2, 1, 3).reshape(B, L, D)


def kernel(x_prompt, x_sample, state_gla, state_hgrn, norm_mix_pre, norm_mix_post, norm_ffn_pre, norm_ffn_post, w_in, gla_w_lr2, gla_b_lr, gla_onorm, hgrn_lb, hgrn_onorm, w_out, w_ffn_in, w_ffn_out):
    B, L, D = x_prompt.shape
    BS, LS, _ = x_sample.shape
    NS = SEQ_PER_ROWGROUP
    assert B == NS and BS % NS == 0 and L % PROMPT_CHUNK == 0
    TP = B * L
    TS = BS * LS
    T = TP + TS
    TM = 1024
    assert T % TM == 0

    sm = jax.nn.softmax(hgrn_lb.astype(F32), axis=0)
    lb = jnp.clip(jnp.cumsum(sm, axis=0) - sm[0:1], 0.0, 1.0 - 1e-6)
    lbf = jnp.maximum(lb, LB_FLOOR)
    oml = 1.0 - lb

    c_lr0 = 2 * GLA_HEADS * GLA_DK + 2 * GLA_HEADS * GLA_DV
    c_lr1 = c_lr0 + GLA_GATE_RANK
    w_main = jnp.concatenate([w_in[:, :, :c_lr0], w_in[:, :, c_lr1:]], axis=2).astype(BF16)
    w_lr = jnp.pad(w_in[:, :, c_lr0:c_lr1], ((0, 0), (0, 0), (0, LANES - GLA_GATE_RANK))).astype(BF16)
    w_lr2 = jnp.pad(gla_w_lr2, ((0, 0), (0, LANES - GLA_GATE_RANK), (0, 0))).astype(BF16)
    w_o = w_out.astype(BF16)
    w_a = w_ffn_in[:, :, :D_FF].astype(BF16)
    w_u = w_ffn_in[:, :, D_FF:].astype(BF16)
    w_f = w_ffn_out.astype(BF16)

    x = jnp.concatenate([_to_rows(x_prompt), _to_rows(x_sample)], axis=0)
    xn = _norm_call(x, norm_mix_pre[0], TM)

    n_sample_groups = BS // NS
    gla_p = gla_s = hg_p = hg_s = None
    for l in range(DEPTH):
        proj = _inproj_call(xn, w_main[l], TM, 1024)
        g_gla = _gla_gate_call(xn, w_lr[l], w_lr2[l], gla_b_lr[l], TM)

        og, gla_p = _recurrence_call(proj, g_gla, None, gla_onorm[l], None, None, layer=l, hgrn=False,
                                     C=PROMPT_CHUNK, row0=0, n_steps=L // PROMPT_CHUNK,
                                     carry_state=True, out_rows=T, prev_states=gla_p)
        og, gla_s = _recurrence_call(proj, g_gla, state_gla, gla_onorm[l], None, None, layer=l, hgrn=False,
                                     C=LS, row0=TP, n_steps=n_sample_groups,
                                     carry_state=False, out_rows=T, prev_out=og, prev_states=gla_s)
        oh, hg_p = _recurrence_call(proj, None, None, hgrn_onorm[l], lbf[l], oml[l], layer=l, hgrn=True,
                                    C=PROMPT_CHUNK, row0=0, n_steps=L // PROMPT_CHUNK,
                                    carry_state=True, out_rows=T, prev_states=hg_p)
        oh, hg_s = _recurrence_call(proj, None, state_hgrn, hgrn_onorm[l], lbf[l], oml[l], layer=l, hgrn=True,
                                    C=LS, row0=TP, n_steps=n_sample_groups,
                                    carry_state=False, out_rows=T, prev_out=oh, prev_states=hg_s)

        h, hn = _outproj_call(og, oh, proj, x, w_o[l], norm_mix_post[l], norm_ffn_pre[l], TM // 2)
        nnext = norm_mix_pre[l + 1] if l + 1 < DEPTH else norm_mix_pre[0]
        x, xn = _ffn_call(hn, h, w_a[l], w_u[l], w_f[l], norm_ffn_post[l], nnext, TM // 2)

    y_p = _from_rows(x[:TP], B, L)
    y_s = _from_rows(x[TP:], BS, LS)
    return (y_p, y_s, gla_p, gla_s, hg_p, hg_s)
```

```python
import functools
import math

import jax
import jax.numpy as jnp
from jax import lax
from jax.experimental import pallas as pl
from jax.experimental.pallas import tpu as pltpu

F32 = jnp.float32
BF16 = jnp.bfloat16

D_MODEL = 1024
DEPTH = 4
GLA_HEADS = 4
GLA_DK = 128
GLA_DV = 256
GLA_GATE_RANK = 16
GLA_TAU = 16.0
HGRN_HEADS = 8
HGRN_EXPAND = 128
HGRN_DV = 128
LB_FLOOR = 1e-20
D_FF = 2816
EPS = 1e-6
LOG2E = 1.4426950408889634

SUBLANES = 8
LANES = 128
MXU_COLS = 256
SEQ_PER_ROWGROUP = SUBLANES

PROMPT_CHUNK = 64
PROMPT_HEADS_PER_STEP = 1
SAMPLE_HEADS_PER_STEP = 4
VMEM_LIMIT = 52 * 1024 * 1024


def _cparams(sem):
    return pltpu.CompilerParams(dimension_semantics=sem, vmem_limit_bytes=VMEM_LIMIT)


def _rms(x, w):
    return x * lax.rsqrt(jnp.mean(x * x, axis=-1, keepdims=True) + EPS) * w


def _norm_kernel(x_ref, w_ref, o_ref):
    o_ref[...] = _rms(x_ref[...], w_ref[...]).astype(BF16)


def _norm_call(x, w, tm):
    T, D = x.shape
    return pl.pallas_call(
        _norm_kernel,
        grid=(T // tm,),
        in_specs=[pl.BlockSpec((tm, D), lambda i: (i, 0)),
                  pl.BlockSpec((1, D), lambda i: (0, 0))],
        out_specs=pl.BlockSpec((tm, D), lambda i: (i, 0)),
        out_shape=jax.ShapeDtypeStruct((T, D), BF16),
        compiler_params=_cparams(("parallel",)),
        name="prenorm",
    )(x, w.reshape(1, D))


def _gla_gate_kernel(x_ref, wlr_ref, wlr2_ref, blr_ref, o_ref):
    glr = jnp.dot(x_ref[...], wlr_ref[...], preferred_element_type=F32)
    y = jnp.dot(glr.astype(BF16), wlr2_ref[...], preferred_element_type=F32) + blr_ref[...]
    ls = -(jnp.maximum(-y, 0.0) + jnp.log1p(jnp.exp(-jnp.abs(y))))
    o_ref[...] = ls * (1.0 / GLA_TAU)


def _gla_gate_call(xn, wlr, wlr2, blr, tm):
    T, D = xn.shape
    R = wlr.shape[1]
    N = wlr2.shape[1]
    return pl.pallas_call(
        _gla_gate_kernel,
        grid=(T // tm,),
        in_specs=[pl.BlockSpec((tm, D), lambda i: (i, 0)),
                  pl.BlockSpec((D, R), lambda i: (0, 0)),
                  pl.BlockSpec((R, N), lambda i: (0, 0)),
                  pl.BlockSpec((1, N), lambda i: (0, 0))],
        out_specs=pl.BlockSpec((tm, N), lambda i: (i, 0)),
        out_shape=jax.ShapeDtypeStruct((T, N), F32),
        compiler_params=_cparams(("parallel",)),
        name="gla_gate",
    )(xn, wlr, wlr2, blr.reshape(1, N))


def _level_halves(C):
    out = []
    m = C // 2
    while m >= 1:
        out.append(m)
        m //= 2
    return out


def _recurrence_kernel(*refs, C, dv, hgrn, carry_state, q_scale, n_aliased, HB):
    nv = dv // LANES
    nblk = 2 + 2 * nv
    it = iter(refs)
    xcur_ref = next(it)
    xnext_ref = next(it)
    w_ref = next(it)
    g_ref = None if hgrn else next(it)
    onw_ref = next(it)
    lbf_ref = next(it) if hgrn else None
    oml_ref = next(it) if hgrn else None
    s0_ref = None if carry_state else next(it)
    for _ in range(n_aliased):
        next(it)
    o_ref = next(it)
    sout_ref = next(it)
    p_scr = next(it)
    pnext_scr = next(it)
    b_scr = next(it)
    zs_ref = next(it)
    kk_scr = next(it) if hgrn else None
    oacc_ref = next(it)
    dcol_ref = next(it)
    S_ref = next(it) if carry_state else sout_ref

    levels = _level_halves(C)
    NL = len(levels)
    NS = SEQ_PER_ROWGROUP
    c_idx = pl.program_id(1)

    if carry_state:
        @pl.when(c_idx == 0)
        def _():
            S_ref[...] = jnp.zeros_like(S_ref)
    else:
        S_ref[...] = s0_ref[...]

    def project(x_ref):
        p = jnp.dot(x_ref[...], w_ref[...], preferred_element_type=F32)
        for i in range(HB * nblk):
            pnext_scr[i] = p[:, i * LANES:(i + 1) * LANES]

    @pl.when(c_idx == 0)
    def _():
        project(xcur_ref)

    p_scr[...] = pnext_scr[...]
    project(xnext_ref)

    def rows(t):
        return pl.ds(t * NS, NS)

    def seq_rows(s):
        return pl.ds(s, C, stride=NS)

    def slab(hh, k):
        return hh * nblk + k

    def vector_phase(hh):
        lanes = slice(hh * LANES, (hh + 1) * LANES)
        acc = jnp.zeros((NS, LANES), F32)
        for t in range(C):
            if hgrn:
                z = p_scr[slab(hh, 1), rows(t), :]
                u = jnp.exp(-jnp.abs(z))
                r = 1.0 / (1.0 + u)
                ur = u * r
                pos = z >= 0.0
                sig_p = jnp.where(pos, r, ur)
                sig_n = jnp.where(pos, ur, r)
                oml = oml_ref[:, lanes]
                g_t = jnp.log(lbf_ref[:, lanes] + oml * sig_p)
                kk_scr[hh, rows(t), :] = oml * sig_n
            else:
                g_t = g_ref[rows(t), lanes]
            acc = acc + g_t
            b_scr[hh, t] = acc * LOG2E
        b_last = acc * LOG2E
        d_last = jnp.exp2(b_last)

        d_t = jnp.transpose(jnp.concatenate([d_last] * (LANES // NS), axis=0))
        for s in range(NS):
            dcol_ref[hh, s] = jnp.broadcast_to(d_t[:, s:s + 1], (LANES, LANES))

        for t in range(C):
            q_t = p_scr[slab(hh, 0), rows(t), :]
            if q_scale != 1.0:
                q_t = q_t * q_scale
            k_t = kk_scr[hh, rows(t), :] if hgrn else p_scr[slab(hh, 1), rows(t), :]
            b_t = b_scr[hh, t]
            zs_ref[hh, NL, rows(t), :] = q_t * jnp.exp2(b_t)
            zs_ref[hh, NL + 1, rows(t), :] = k_t * jnp.exp2(b_last - b_t)
            for j, m in enumerate(levels):
                r_idx = (t // (2 * m)) * (2 * m) + m - 1
                if (t % (2 * m)) >= m:
                    zs_ref[hh, j, rows(t), :] = q_t * jnp.exp2(b_t - b_scr[hh, r_idx])
                else:
                    zs_ref[hh, j, rows(t), :] = k_t * jnp.exp2(b_scr[hh, r_idx] - b_t)
            dsc = jnp.sum(q_t * k_t, axis=-1, keepdims=True)
            for i in range(nv):
                oacc_ref[hh, i, rows(t), :] = dsc * p_scr[slab(hh, 2 + i), rows(t), :]

    ti = lax.broadcasted_iota(jnp.int32, (C, C), 0)
    si = lax.broadcasted_iota(jnp.int32, (C, C), 1)
    xr = jnp.bitwise_xor(ti, si)
    lower = ti > si
    masks = [lower & (xr >= m) & (xr < 2 * m) for m in levels]

    def matmul_phase(hh):
        def load_v(s):
            vs = [p_scr[slab(hh, 2 + i), seq_rows(s), :] for i in range(nv)]
            vb = vs[0] if nv == 1 else jnp.concatenate(vs, axis=1)
            return vb.astype(BF16)

        scores = []
        for s in range(NS):
            sc = jnp.zeros((C, C), F32)
            for j in range(NL):
                z = zs_ref[hh, j, seq_rows(s), :].astype(BF16)
                gram = lax.dot_general(z, z, (((1,), (1,)), ((), ())), preferred_element_type=F32)
                sc = jnp.where(masks[j], gram, sc)
            scores.append(sc.astype(BF16))
        inter = []
        for s in range(NS):
            zq = zs_ref[hh, NL, seq_rows(s), :].astype(BF16)
            zk = zs_ref[hh, NL + 1, seq_rows(s), :].astype(BF16)
            S = S_ref[s, hh]
            inter.append(jnp.dot(zq, S.astype(BF16), preferred_element_type=F32))
            upd = lax.dot_general(zk, load_v(s), (((0,), (0,)), ((), ())), preferred_element_type=F32)
            dc = dcol_ref[hh, s]
            if nv > 1:
                dc = jnp.concatenate([dc] * nv, axis=1)
            S_ref[s, hh] = dc * S + upd
        for s in range(NS):
            o = inter[s] + jnp.dot(scores[s], load_v(s), preferred_element_type=F32)
            for i in range(nv):
                sl = slice(i * LANES, (i + 1) * LANES)
                oacc_ref[hh, i, seq_rows(s), :] = oacc_ref[hh, i, seq_rows(s), :] + o[:, sl]

    def gate_phase(hh):
        cols = slice(hh * dv, (hh + 1) * dv)
        if nv == 1:
            o_all = oacc_ref[hh, 0]
        else:
            o_all = jnp.concatenate([oacc_ref[hh, i] for i in range(nv)], axis=1)
        y = o_all * lax.rsqrt(jnp.mean(o_all * o_all, axis=-1, keepdims=True) + EPS) * onw_ref[...]
        gs = [p_scr[slab(hh, 2 + nv + i)] for i in range(nv)]
        gt = gs[0] if nv == 1 else jnp.concatenate(gs, axis=1)
        o_ref[:, cols] = y * (gt * (1.0 / (1.0 + jnp.exp(-gt))))

    for hh in range(HB):
        vector_phase(hh)
    for hh in range(HB):
        matmul_phase(hh)
    for hh in range(HB):
        gate_phase(hh)

    if carry_state:
        @pl.when(c_idx == pl.num_programs(1) - 1)
        def _():
            sout_ref[...] = S_ref[...]


def _recurrence_call(xn, w_heads, g_gla, s_in, onw, lbf, oml, *, layer, hgrn, C, row0, n_steps, carry_state,
                     out_rows, heads_per_step, prev_out=None, prev_states=None):
    NS = SEQ_PER_ROWGROUP
    R = NS * C
    H = HGRN_HEADS if hgrn else GLA_HEADS
    dv = HGRN_DV if hgrn else GLA_DV
    dk = LANES
    nv = dv // LANES
    assert row0 % R == 0
    rb0 = row0 // R
    D = xn.shape[1]
    W = (2 + 2 * nv) * LANES
    HB = heads_per_step
    assert H % HB == 0 and w_heads.shape == (D, H * W)
    last = n_steps - 1
    in_specs = [pl.BlockSpec((R, D), lambda h, c: (rb0 + c, 0)),
                pl.BlockSpec((R, D), lambda h, c: (rb0 + jnp.minimum(c + 1, last), 0)),
                pl.BlockSpec((D, HB * W), lambda h, c: (0, h))]
    args = [xn, xn, w_heads]
    if not hgrn:
        in_specs.append(pl.BlockSpec((R, HB * LANES), lambda h, c: (rb0 + c, h)))
        args.append(g_gla)
    in_specs.append(pl.BlockSpec((1, dv), lambda h, c: (0, 0)))
    args.append(onw.reshape(1, dv))
    if hgrn:
        in_specs += [pl.BlockSpec((1, HB * LANES), lambda h, c: (0, h)),
                     pl.BlockSpec((1, HB * LANES), lambda h, c: (0, h))]
        args += [lbf.reshape(1, -1), oml.reshape(1, -1)]
    n_groups = 1 if carry_state else n_steps
    if carry_state:
        s_map = lambda h, c: (layer, 0, h, 0, 0)
    else:
        s_map = lambda h, c: (layer, c, h, 0, 0)
        in_specs.append(pl.BlockSpec((None, NS, HB, dk, dv), s_map))
        args.append(s_in)
    aliases = {}
    for k, prev in enumerate((prev_out, prev_states)):
        if prev is not None:
            in_specs.append(pl.BlockSpec(memory_space=pl.ANY))
            args.append(prev)
            aliases[len(args) - 1] = k

    out_specs = [pl.BlockSpec((R, HB * dv), lambda h, c: (rb0 + c, h)),
                 pl.BlockSpec((None, NS, HB, dk, dv), s_map)]
    out_shape = [jax.ShapeDtypeStruct((out_rows, H * dv), F32),
                 jax.ShapeDtypeStruct((DEPTH, n_groups * NS, H, dk, dv), F32)]
    NL = len(_level_halves(C))
    scratch = [pltpu.VMEM((HB * W // LANES, R, LANES), F32),
               pltpu.VMEM((HB * W // LANES, R, LANES), F32),
               pltpu.VMEM((HB, C, NS, LANES), F32),
               pltpu.VMEM((HB, NL + 2, R, LANES), F32)]
    if hgrn:
        scratch.append(pltpu.VMEM((HB, R, LANES), F32))
    scratch += [pltpu.VMEM((HB, nv, R, LANES), F32),
                pltpu.VMEM((HB, NS, LANES, LANES), F32)]
    if carry_state:
        scratch.append(pltpu.VMEM((NS, HB, dk, dv), F32))

    kern = functools.partial(_recurrence_kernel, C=C, dv=dv, hgrn=hgrn, carry_state=carry_state,
                             q_scale=1.0 if hgrn else GLA_DK ** -0.5, n_aliased=len(aliases), HB=HB)
    return pl.pallas_call(
        kern,
        grid=(H // HB, n_steps),
        in_specs=in_specs,
        out_specs=out_specs,
        out_shape=out_shape,
        scratch_shapes=scratch,
        input_output_aliases=aliases,
        compiler_params=_cparams(("parallel", "arbitrary")),
        name=("hgrn" if hgrn else "gla") + ("_prompt" if carry_state else "_sample"),
    )(*args)


def _outproj_kernel(og_ref, oh_ref, xn_ref, x_ref, wab_ref, wo_ref, npost_ref, npre_ref, h_ref, hn_ref):
    D = x_ref.shape[1]
    gab = jnp.dot(xn_ref[...], wab_ref[...], preferred_element_type=F32)
    ga = gab[:, :D]
    gb = gab[:, D:]
    merged = og_ref[...] / (1.0 + jnp.exp(-ga)) + oh_ref[...] / (1.0 + jnp.exp(-gb))
    m = jnp.dot(merged.astype(BF16), wo_ref[...], preferred_element_type=F32)
    h = x_ref[...] + _rms(m, npost_ref[...])
    h_ref[...] = h
    hn_ref[...] = _rms(h, npre_ref[...]).astype(BF16)


def _outproj_call(og, oh, xn, x, wab, wo, npost, npre, tm):
    T, D = x.shape
    row = lambda i: (i, 0)
    const = lambda i: (0, 0)
    once = pl.Buffered(1)
    return pl.pallas_call(
        _outproj_kernel,
        grid=(T // tm,),
        in_specs=[pl.BlockSpec((tm, D), row),
                  pl.BlockSpec((tm, D), row),
                  pl.BlockSpec((tm, D), row),
                  pl.BlockSpec((tm, D), row),
                  pl.BlockSpec((D, 2 * D), const, pipeline_mode=once),
                  pl.BlockSpec((D, D), const, pipeline_mode=once),
                  pl.BlockSpec((1, D), const),
                  pl.BlockSpec((1, D), const)],
        out_specs=[pl.BlockSpec((tm, D), row), pl.BlockSpec((tm, D), row)],
        out_shape=[jax.ShapeDtypeStruct((T, D), F32), jax.ShapeDtypeStruct((T, D), BF16)],
        compiler_params=_cparams(("parallel",)),
        name="outproj",
    )(og, oh, xn, x, wab, wo, npost.reshape(1, D), npre.reshape(1, D))


def _ffn_kernel(hn_ref, h_ref, wa_ref, wu_ref, wo_ref, npost_ref, nnext_ref, y_ref, yn_ref, *, splits):
    hn = hn_ref[...]
    f = None
    for lo, hi in splits:
        sl = slice(lo, hi)
        a = jnp.dot(hn, wa_ref[:, sl], preferred_element_type=F32)
        u = jnp.dot(hn, wu_ref[:, sl], preferred_element_type=F32)
        g = (a * (1.0 / (1.0 + jnp.exp(-a))) * u).astype(BF16)
        part = jnp.dot(g, wo_ref[sl, :], preferred_element_type=F32)
        f = part if f is None else f + part
    y = h_ref[...] + _rms(f, npost_ref[...])
    y_ref[...] = y
    yn_ref[...] = _rms(y, nnext_ref[...]).astype(BF16)


def _ffn_call(hn, h, wa, wu, wo, npost, nnext, tm):
    T, D = h.shape
    F = wa.shape[1]
    row = lambda i: (i, 0)
    const = lambda i: (0, 0)
    once = pl.Buffered(1)
    half = (F // MXU_COLS + 1) // 2 * MXU_COLS
    assert F % MXU_COLS == 0
    return pl.pallas_call(
        functools.partial(_ffn_kernel, splits=((0, half), (half, F))),
        grid=(T // tm,),
        in_specs=[pl.BlockSpec((tm, D), row),
                  pl.BlockSpec((tm, D), row),
                  pl.BlockSpec((D, F), const, pipeline_mode=once),
                  pl.BlockSpec((D, F), const, pipeline_mode=once),
                  pl.BlockSpec((F, D), const, pipeline_mode=once),
                  pl.BlockSpec((1, D), const),
                  pl.BlockSpec((1, D), const)],
        out_specs=[pl.BlockSpec((tm, D), row), pl.BlockSpec((tm, D), row)],
        out_shape=[jax.ShapeDtypeStruct((T, D), F32), jax.ShapeDtypeStruct((T, D), BF16)],
        compiler_params=_cparams(("parallel",)),
        name="ffn",
    )(hn, h, wa, wu, wo, npost.reshape(1, D), nnext.reshape(1, D))


def _to_rows(x):
    B, L, D = x.shape
    NS = SEQ_PER_ROWGROUP
    return x.reshape(B // NS, NS, L, D).transpose(0, 2, 1, 3).reshape(B * L, D)


def _from_rows(r, B, L):
    NS = SEQ_PER_ROWGROUP
    D = r.shape[-1]
    return r.reshape(B // NS, L, NS, D).transpose(0, 2, 1, 3).reshape(B, L, D)


def kernel(x_prompt, x_sample, state_gla, state_hgrn, norm_mix_pre, norm_mix_post, norm_ffn_pre, norm_ffn_post, w_in, gla_w_lr2, gla_b_lr, gla_onorm, hgrn_lb, hgrn_onorm, w_out, w_ffn_in, w_ffn_out):
    B, L, D = x_prompt.shape
    BS, LS, _ = x_sample.shape
    NS = SEQ_PER_ROWGROUP
    assert B == NS and BS % NS == 0 and L % PROMPT_CHUNK == 0
    TP = B * L
    TS = BS * LS
    T = TP + TS
    TM = 1024
    assert T % TM == 0

    sm = jax.nn.softmax(hgrn_lb.astype(F32), axis=0)
    lb = jnp.clip(jnp.cumsum(sm, axis=0) - sm[0:1], 0.0, 1.0 - 1e-6)
    lbf = jnp.maximum(lb, LB_FLOOR)
    oml = 1.0 - lb

    def per_head(c0, widths, heads):
        parts, c = [], c0
        for width in widths:
            parts.append(w_in[:, :, c:c + heads * width].reshape(DEPTH, D, heads, width))
            c += heads * width
        return jnp.concatenate(parts, axis=-1).reshape(DEPTH, D, -1).astype(BF16), c

    w_gla, c_lr0 = per_head(0, (GLA_DK, GLA_DK, GLA_DV, GLA_DV), GLA_HEADS)
    c_lr1 = c_lr0 + GLA_GATE_RANK
    w_hgrn, c_ab = per_head(c_lr1, (HGRN_EXPAND, HGRN_EXPAND, HGRN_DV, HGRN_DV), HGRN_HEADS)
    w_ab = w_in[:, :, c_ab:].astype(BF16)
    assert w_ab.shape[2] == 2 * D
    w_lr = jnp.pad(w_in[:, :, c_lr0:c_lr1], ((0, 0), (0, 0), (0, LANES - GLA_GATE_RANK))).astype(BF16)
    w_lr2 = jnp.pad(gla_w_lr2, ((0, 0), (0, LANES - GLA_GATE_RANK), (0, 0))).astype(BF16)
    w_o = w_out.astype(BF16)
    w_a = w_ffn_in[:, :, :D_FF].astype(BF16)
    w_u = w_ffn_in[:, :, D_FF:].astype(BF16)
    w_f = w_ffn_out.astype(BF16)

    x = jnp.concatenate([_to_rows(x_prompt), _to_rows(x_sample)], axis=0)
    xn = _norm_call(x, norm_mix_pre[0], TM)

    n_sample_groups = BS // NS
    gla_p = gla_s = hg_p = hg_s = None
    for l in range(DEPTH):
        g_gla = _gla_gate_call(xn, w_lr[l], w_lr2[l], gla_b_lr[l], TM)

        og, gla_p = _recurrence_call(xn, w_gla[l], g_gla, None, gla_onorm[l], None, None, layer=l, hgrn=False,
                                     C=PROMPT_CHUNK, row0=0, n_steps=L // PROMPT_CHUNK,
                                     carry_state=True, out_rows=T, heads_per_step=PROMPT_HEADS_PER_STEP, prev_states=gla_p)
        og, gla_s = _recurrence_call(xn, w_gla[l], g_gla, state_gla, gla_onorm[l], None, None, layer=l, hgrn=False,
                                     C=LS, row0=TP, n_steps=n_sample_groups,
                                     carry_state=False, out_rows=T, heads_per_step=SAMPLE_HEADS_PER_STEP, prev_out=og, prev_states=gla_s)
        oh, hg_p = _recurrence_call(xn, w_hgrn[l], None, None, hgrn_onorm[l], lbf[l], oml[l], layer=l, hgrn=True,
                                    C=PROMPT_CHUNK, row0=0, n_steps=L // PROMPT_CHUNK,
                                    carry_state=True, out_rows=T, heads_per_step=PROMPT_HEADS_PER_STEP, prev_states=hg_p)
        oh, hg_s = _recurrence_call(xn, w_hgrn[l], None, state_hgrn, hgrn_onorm[l], lbf[l], oml[l], layer=l, hgrn=True,
                                    C=LS, row0=TP, n_steps=n_sample_groups,
                                    carry_state=False, out_rows=T, heads_per_step=SAMPLE_HEADS_PER_STEP, prev_out=oh, prev_states=hg_s)

        h, hn = _outproj_call(og, oh, xn, x, w_ab[l], w_o[l], norm_mix_post[l], norm_ffn_pre[l], TM // 2)
        nnext = norm_mix_pre[l + 1] if l + 1 < DEPTH else norm_mix_pre[0]
        x, xn = _ffn_call(hn, h, w_a[l], w_u[l], w_f[l], norm_ffn_post[l], nnext, TM // 2)

    y_p = _from_rows(x[:TP], B, L)
    y_s = _from_rows(x[TP:], BS, LS)
    return (y_p, y_s, gla_p, gla_s, hg_p, hg_s)
```

```python
import functools
import math

import jax
import jax.numpy as jnp
from jax import lax
from jax.experimental import pallas as pl
from jax.experimental.pallas import tpu as pltpu

F32 = jnp.float32
BF16 = jnp.bfloat16

D_MODEL = 1024
DEPTH = 4
GLA_HEADS = 4
GLA_DK = 128
GLA_DV = 256
GLA_GATE_RANK = 16
GLA_TAU = 16.0
HGRN_HEADS = 8
HGRN_EXPAND = 128
HGRN_DV = 128
LB_FLOOR = 1e-20
D_FF = 2816
EPS = 1e-6
LOG2E = 1.4426950408889634

SUBLANES = 8
LANES = 128
MXU_COLS = 256
SEQ_PER_ROWGROUP = SUBLANES

PROMPT_CHUNK = 64
PROMPT_HEADS_PER_STEP = 1
SAMPLE_HEADS_PER_STEP = 4
FINE_BLOCK = SUBLANES
POST_ROWS = 256
VMEM_LIMIT = 52 * 1024 * 1024


def _cparams(sem):
    return pltpu.CompilerParams(dimension_semantics=sem, vmem_limit_bytes=VMEM_LIMIT)


def _rms(x, w):
    return x * lax.rsqrt(jnp.mean(x * x, axis=-1, keepdims=True) + EPS) * w


def _gla_gate(xn, wlr, wlr2, blr):
    glr = jnp.dot(xn, wlr, preferred_element_type=F32)
    y = jnp.dot(glr.astype(BF16), wlr2, preferred_element_type=F32) + blr
    ls = -(jnp.maximum(-y, 0.0) + jnp.log1p(jnp.exp(-jnp.abs(y))))
    return ls * (1.0 / GLA_TAU)


def _prenorm_kernel(x_ref, w_ref, wlr_ref, wlr2_ref, blr_ref, xn_ref, g_ref):
    xn = _rms(x_ref[...], w_ref[...]).astype(BF16)
    xn_ref[...] = xn
    g_ref[...] = _gla_gate(xn, wlr_ref[...], wlr2_ref[...], blr_ref[...])


def _prenorm_call(x, w, w_lr, w_lr2, b_lr, tm):
    T, D = x.shape
    R, N = w_lr2.shape[1:]
    row = lambda i: (i, 0)
    const = lambda i: (0, 0)
    l0 = lambda i: (0, 0, 0)
    return pl.pallas_call(
        _prenorm_kernel,
        grid=(T // tm,),
        in_specs=[pl.BlockSpec((tm, D), row),
                  pl.BlockSpec((1, D), const),
                  pl.BlockSpec((None, D, R), l0),
                  pl.BlockSpec((None, R, N), l0),
                  pl.BlockSpec((1, N), const)],
        out_specs=[pl.BlockSpec((tm, D), row), pl.BlockSpec((tm, N), row)],
        out_shape=[jax.ShapeDtypeStruct((T, D), BF16), jax.ShapeDtypeStruct((T, N), F32)],
        compiler_params=_cparams(("parallel",)),
        name="prenorm",
    )(x, w.reshape(1, D), w_lr, w_lr2, b_lr.reshape(1, N))


def _gram_levels(C):
    out = []
    m = C // 2
    while m >= FINE_BLOCK:
        out.append(m)
        m //= 2
    return out


def _recurrence_kernel(*refs, C, dv, hgrn, carry_state, q_scale, n_aliased, HB):
    nv = dv // LANES
    nblk = 2 + 2 * nv
    it = iter(refs)
    xcur_ref = next(it)
    xnext_ref = next(it)
    w_ref = next(it)
    g_ref = None if hgrn else next(it)
    onw_ref = next(it)
    lbf_ref = next(it) if hgrn else None
    oml_ref = next(it) if hgrn else None
    s0_ref = None if carry_state else next(it)
    for _ in range(n_aliased):
        next(it)
    o_ref = next(it)
    sout_ref = next(it)
    p_scr = next(it)
    pnext_scr = next(it)
    a_scr = next(it)
    b_scr = next(it)
    zs_ref = next(it)
    kk_scr = next(it) if hgrn else None
    oacc_ref = next(it)
    dcol_ref = next(it)
    S_ref = next(it) if carry_state else sout_ref

    FINE = min(C, FINE_BLOCK)
    levels = _gram_levels(C)
    NL = len(levels)
    NS = SEQ_PER_ROWGROUP
    c_idx = pl.program_id(1)

    if carry_state:
        @pl.when(c_idx == 0)
        def _():
            S_ref[...] = jnp.zeros_like(S_ref)
    else:
        S_ref[...] = s0_ref[...]

    def project(x_ref):
        p = jnp.dot(x_ref[...], w_ref[...], preferred_element_type=F32)
        for i in range(HB * nblk):
            pnext_scr[i] = p[:, i * LANES:(i + 1) * LANES]

    @pl.when(c_idx == 0)
    def _():
        project(xcur_ref)

    p_scr[...] = pnext_scr[...]

    def rows(t):
        return pl.ds(t * NS, NS)

    def seq_rows(s):
        return pl.ds(s, C, stride=NS)

    def slab(hh, k):
        return hh * nblk + k

    def vector_phase(hh):
        lanes = slice(hh * LANES, (hh + 1) * LANES)
        acc = jnp.zeros((NS, LANES), F32)
        for t in range(C):
            if hgrn:
                z = p_scr[slab(hh, 1), rows(t), :]
                u = jnp.exp(-jnp.abs(z))
                r = 1.0 / (1.0 + u)
                ur = u * r
                pos = z >= 0.0
                sig_p = jnp.where(pos, r, ur)
                sig_n = jnp.where(pos, ur, r)
                oml = oml_ref[:, lanes]
                g_t = jnp.log(lbf_ref[:, lanes] + oml * sig_p)
                kk_scr[hh, rows(t), :] = oml * sig_n
            else:
                g_t = g_ref[rows(t), lanes]
            acc = acc + g_t
            a_scr[hh, t] = jnp.exp2(g_t * LOG2E)
            b_scr[hh, t] = acc * LOG2E
        b_last = acc * LOG2E
        d_last = jnp.exp2(b_last)

        d_t = jnp.transpose(jnp.concatenate([d_last] * (LANES // NS), axis=0))
        for s in range(NS):
            dcol_ref[hh, s] = jnp.broadcast_to(d_t[:, s:s + 1], (LANES, LANES))

        for t in range(C):
            q_t = p_scr[slab(hh, 0), rows(t), :]
            if q_scale != 1.0:
                q_t = q_t * q_scale
            k_t = kk_scr[hh, rows(t), :] if hgrn else p_scr[slab(hh, 1), rows(t), :]
            b_t = b_scr[hh, t]
            zs_ref[hh, NL, rows(t), :] = q_t * jnp.exp2(b_t)
            zs_ref[hh, NL + 1, rows(t), :] = k_t * jnp.exp2(b_last - b_t)
            for j, m in enumerate(levels):
                r_idx = (t // (2 * m)) * (2 * m) + m - 1
                if (t % (2 * m)) >= m:
                    zs_ref[hh, j, rows(t), :] = q_t * jnp.exp2(b_t - b_scr[hh, r_idx])
                else:
                    zs_ref[hh, j, rows(t), :] = k_t * jnp.exp2(b_scr[hh, r_idx] - b_t)

        for t0 in range(0, C, FINE):
            q_blk, k_blk, v_blk, decay = {}, {}, {}, {}
            for t in range(t0, t0 + FINE):
                q_t = p_scr[slab(hh, 0), rows(t), :]
                if q_scale != 1.0:
                    q_t = q_t * q_scale
                q_blk[t] = q_t
                k_blk[t] = kk_scr[hh, rows(t), :] if hgrn else p_scr[slab(hh, 1), rows(t), :]
                v_blk[t] = [p_scr[slab(hh, 2 + i), rows(t), :] for i in range(nv)]
                a_t = a_scr[hh, t]
                for s in range(t0, t):
                    decay[s] = a_t if s == t - 1 else decay[s] * a_t
                w = jnp.sum(q_t * k_blk[t], axis=-1, keepdims=True)
                o_t = [w * v_blk[t][i] for i in range(nv)]
                for s in range(t0, t):
                    w = jnp.sum(q_t * (k_blk[s] * decay[s]), axis=-1, keepdims=True)
                    o_t = [o_t[i] + w * v_blk[s][i] for i in range(nv)]
                for i in range(nv):
                    oacc_ref[hh, i, rows(t), :] = o_t[i]

    ti = lax.broadcasted_iota(jnp.int32, (C, C), 0)
    si = lax.broadcasted_iota(jnp.int32, (C, C), 1)
    xr = jnp.bitwise_xor(ti, si)
    lower = ti > si
    masks = [lower & (xr >= m) & (xr < 2 * m) for m in levels]

    def matmul_phase(hh):
        def load_v(s):
            vs = [p_scr[slab(hh, 2 + i), seq_rows(s), :] for i in range(nv)]
            vb = vs[0] if nv == 1 else jnp.concatenate(vs, axis=1)
            return vb.astype(BF16)

        scores = []
        for s in range(NS if NL else 0):
            sc = jnp.zeros((C, C), F32)
            for j in range(NL):
                z = zs_ref[hh, j, seq_rows(s), :].astype(BF16)
                gram = lax.dot_general(z, z, (((1,), (1,)), ((), ())), preferred_element_type=F32)
                sc = jnp.where(masks[j], gram, sc)
            scores.append(sc.astype(BF16))
        inter = []
        for s in range(NS):
            zq = zs_ref[hh, NL, seq_rows(s), :].astype(BF16)
            zk = zs_ref[hh, NL + 1, seq_rows(s), :].astype(BF16)
            S = S_ref[s, hh]
            inter.append(jnp.dot(zq, S.astype(BF16), preferred_element_type=F32))
            upd = lax.dot_general(zk, load_v(s), (((0,), (0,)), ((), ())), preferred_element_type=F32)
            dc = dcol_ref[hh, s]
            if nv > 1:
                dc = jnp.concatenate([dc] * nv, axis=1)
            S_ref[s, hh] = dc * S + upd
        for s in range(NS):
            o = inter[s]
            if NL:
                o = o + jnp.dot(scores[s], load_v(s), preferred_element_type=F32)
            for i in range(nv):
                sl = slice(i * LANES, (i + 1) * LANES)
                oacc_ref[hh, i, seq_rows(s), :] = oacc_ref[hh, i, seq_rows(s), :] + o[:, sl]

    def gate_phase(hh):
        cols = slice(hh * dv, (hh + 1) * dv)
        if nv == 1:
            o_all = oacc_ref[hh, 0]
        else:
            o_all = jnp.concatenate([oacc_ref[hh, i] for i in range(nv)], axis=1)
        y = o_all * lax.rsqrt(jnp.mean(o_all * o_all, axis=-1, keepdims=True) + EPS) * onw_ref[...]
        gs = [p_scr[slab(hh, 2 + nv + i)] for i in range(nv)]
        gt = gs[0] if nv == 1 else jnp.concatenate(gs, axis=1)
        o_ref[:, cols] = y * (gt * (1.0 / (1.0 + jnp.exp(-gt))))

    for hh in range(HB):
        vector_phase(hh)
    project(xnext_ref)
    for hh in range(HB):
        matmul_phase(hh)
    for hh in range(HB):
        gate_phase(hh)

    if carry_state:
        @pl.when(c_idx == pl.num_programs(1) - 1)
        def _():
            sout_ref[...] = S_ref[...]


def _recurrence_call(xn, w_heads, g_gla, s_in, onw, lbf, oml, *, layer, hgrn, C, row0, n_steps, carry_state,
                     out_rows, heads_per_step, prev_out=None, prev_states=None):
    NS = SEQ_PER_ROWGROUP
    R = NS * C
    H = HGRN_HEADS if hgrn else GLA_HEADS
    dv = HGRN_DV if hgrn else GLA_DV
    dk = LANES
    nv = dv // LANES
    assert row0 % R == 0
    rb0 = row0 // R
    D = xn.shape[1]
    W = (2 + 2 * nv) * LANES
    HB = heads_per_step
    assert H % HB == 0 and w_heads.shape == (DEPTH, D, H * W)
    last = n_steps - 1
    in_specs = [pl.BlockSpec((R, D), lambda h, c: (rb0 + c, 0)),
                pl.BlockSpec((R, D), lambda h, c: (rb0 + jnp.minimum(c + 1, last), 0)),
                pl.BlockSpec((None, D, HB * W), lambda h, c: (layer, 0, h))]
    args = [xn, xn, w_heads]
    if not hgrn:
        in_specs.append(pl.BlockSpec((R, HB * LANES), lambda h, c: (rb0 + c, h)))
        args.append(g_gla)
    in_specs.append(pl.BlockSpec((1, dv), lambda h, c: (0, 0)))
    args.append(onw.reshape(1, dv))
    if hgrn:
        in_specs += [pl.BlockSpec((1, HB * LANES), lambda h, c: (0, h)),
                     pl.BlockSpec((1, HB * LANES), lambda h, c: (0, h))]
        args += [lbf.reshape(1, -1), oml.reshape(1, -1)]
    n_groups = 1 if carry_state else n_steps
    if carry_state:
        s_map = lambda h, c: (layer, 0, h, 0, 0)
    else:
        s_map = lambda h, c: (layer, c, h, 0, 0)
        in_specs.append(pl.BlockSpec((None, NS, HB, dk, dv), s_map))
        args.append(s_in)
    aliases = {}
    for k, prev in enumerate((prev_out, prev_states)):
        if prev is not None:
            in_specs.append(pl.BlockSpec(memory_space=pl.ANY))
            args.append(prev)
            aliases[len(args) - 1] = k

    out_specs = [pl.BlockSpec((R, HB * dv), lambda h, c: (rb0 + c, h)),
                 pl.BlockSpec((None, NS, HB, dk, dv), s_map)]
    out_shape = [jax.ShapeDtypeStruct((out_rows, H * dv), F32),
                 jax.ShapeDtypeStruct((DEPTH, n_groups * NS, H, dk, dv), F32)]
    NL = len(_gram_levels(C))
    scratch = [pltpu.VMEM((HB * W // LANES, R, LANES), F32),
               pltpu.VMEM((HB * W // LANES, R, LANES), F32),
               pltpu.VMEM((HB, C, NS, LANES), F32),
               pltpu.VMEM((HB, C, NS, LANES), F32),
               pltpu.VMEM((HB, NL + 2, R, LANES), F32)]
    if hgrn:
        scratch.append(pltpu.VMEM((HB, R, LANES), F32))
    scratch += [pltpu.VMEM((HB, nv, R, LANES), F32),
                pltpu.VMEM((HB, NS, LANES, LANES), F32)]
    if carry_state:
        scratch.append(pltpu.VMEM((NS, HB, dk, dv), F32))

    kern = functools.partial(_recurrence_kernel, C=C, dv=dv, hgrn=hgrn, carry_state=carry_state,
                             q_scale=1.0 if hgrn else GLA_DK ** -0.5, n_aliased=len(aliases), HB=HB)
    return pl.pallas_call(
        kern,
        grid=(H // HB, n_steps),
        in_specs=in_specs,
        out_specs=out_specs,
        out_shape=out_shape,
        scratch_shapes=scratch,
        input_output_aliases=aliases,
        compiler_params=_cparams(("parallel", "arbitrary")),
        name=("hgrn" if hgrn else "gla") + ("_prompt" if carry_state else "_sample"),
    )(*args)


def _post_kernel(*refs, splits, emit_next):
    (og_ref, oh_ref, xn_ref, x_ref, wab_ref, wo_ref, wa_ref, wu_ref, wf_ref,
     npost_ref, npre_ref, nfpost_ref) = refs[:12]
    if emit_next:
        nnext_ref, wlr_ref, wlr2_ref, blr_ref, y_ref, yn_ref, g_ref = refs[12:]
    else:
        (y_ref,) = refs[12:]
    D = x_ref.shape[1]
    gab = jnp.dot(xn_ref[...], wab_ref[...], preferred_element_type=F32)
    merged = og_ref[...] / (1.0 + jnp.exp(-gab[:, :D])) + oh_ref[...] / (1.0 + jnp.exp(-gab[:, D:]))
    m = jnp.dot(merged.astype(BF16), wo_ref[...], preferred_element_type=F32)
    h = x_ref[...] + _rms(m, npost_ref[...])
    hn = _rms(h, npre_ref[...]).astype(BF16)
    f = None
    for lo, hi in splits:
        sl = slice(lo, hi)
        a = jnp.dot(hn, wa_ref[:, sl], preferred_element_type=F32)
        u = jnp.dot(hn, wu_ref[:, sl], preferred_element_type=F32)
        g = (a * (1.0 / (1.0 + jnp.exp(-a))) * u).astype(BF16)
        part = jnp.dot(g, wf_ref[sl, :], preferred_element_type=F32)
        f = part if f is None else f + part
    y = h + _rms(f, nfpost_ref[...])
    y_ref[...] = y
    if emit_next:
        yn = _rms(y, nnext_ref[...]).astype(BF16)
        yn_ref[...] = yn
        g_ref[...] = _gla_gate(yn, wlr_ref[...], wlr2_ref[...], blr_ref[...])


def _post_call(og, oh, xn, x, w_ab, w_o, w_a, w_u, w_f, npost, npre, nfpost, next_layer, *, layer, tm):
    T, D = x.shape
    F = w_a.shape[2]
    row = lambda i: (i, 0)
    const = lambda i: (0, 0)
    lconst = lambda i: (layer, 0, 0)
    once = pl.Buffered(1)
    half = (F // MXU_COLS + 1) // 2 * MXU_COLS
    assert F % MXU_COLS == 0
    in_specs = [pl.BlockSpec((tm, D), row),
                pl.BlockSpec((tm, D), row),
                pl.BlockSpec((tm, D), row),
                pl.BlockSpec((tm, D), row),
                pl.BlockSpec((None, D, 2 * D), lconst, pipeline_mode=once),
                pl.BlockSpec((None, D, D), lconst, pipeline_mode=once),
                pl.BlockSpec((None, D, F), lconst, pipeline_mode=once),
                pl.BlockSpec((None, D, F), lconst, pipeline_mode=once),
                pl.BlockSpec((None, F, D), lconst, pipeline_mode=once),
                pl.BlockSpec((1, D), const),
                pl.BlockSpec((1, D), const),
                pl.BlockSpec((1, D), const)]
    args = [og, oh, xn, x, w_ab, w_o, w_a, w_u, w_f,
            npost.reshape(1, D), npre.reshape(1, D), nfpost.reshape(1, D)]
    out_specs = [pl.BlockSpec((tm, D), row)]
    out_shape = [jax.ShapeDtypeStruct((T, D), F32)]
    if next_layer is not None:
        nnext, w_lr, w_lr2, b_lr = next_layer
        R, N = w_lr2.shape[1:]
        lnext = lambda i: (layer + 1, 0, 0)
        in_specs += [pl.BlockSpec((1, D), const),
                     pl.BlockSpec((None, D, R), lnext, pipeline_mode=once),
                     pl.BlockSpec((None, R, N), lnext, pipeline_mode=once),
                     pl.BlockSpec((1, N), const)]
        args += [nnext.reshape(1, D), w_lr, w_lr2, b_lr.reshape(1, N)]
        out_specs += [pl.BlockSpec((tm, D), row), pl.BlockSpec((tm, N), row)]
        out_shape += [jax.ShapeDtypeStruct((T, D), BF16), jax.ShapeDtypeStruct((T, N), F32)]
    return pl.pallas_call(
        functools.partial(_post_kernel, splits=((0, half), (half, F)), emit_next=next_layer is not None),
        grid=(T // tm,),
        in_specs=in_specs,
        out_specs=out_specs,
        out_shape=out_shape,
        compiler_params=_cparams(("parallel",)),
        name="post",
    )(*args)


def _to_rows(x):
    B, L, D = x.shape
    NS = SEQ_PER_ROWGROUP
    return x.reshape(B // NS, NS, L, D).transpose(0, 2, 1, 3).reshape(B * L, D)


def _from_rows(r, B, L):
    NS = SEQ_PER_ROWGROUP
    D = r.shape[-1]
    return r.reshape(B // NS, L, NS, D).transpose(0, 2, 1, 3).reshape(B, L, D)


def kernel(x_prompt, x_sample, state_gla, state_hgrn, norm_mix_pre, norm_mix_post, norm_ffn_pre, norm_ffn_post, w_in, gla_w_lr2, gla_b_lr, gla_onorm, hgrn_lb, hgrn_onorm, w_out, w_ffn_in, w_ffn_out):
    B, L, D = x_prompt.shape
    BS, LS, _ = x_sample.shape
    NS = SEQ_PER_ROWGROUP
    assert B == NS and BS % NS == 0 and L % PROMPT_CHUNK == 0
    TP = B * L
    TS = BS * LS
    T = TP + TS
    TM = 1024
    assert T % TM == 0

    sm = jax.nn.softmax(hgrn_lb.astype(F32), axis=0)
    lb = jnp.clip(jnp.cumsum(sm, axis=0) - sm[0:1], 0.0, 1.0 - 1e-6)
    lbf = jnp.maximum(lb, LB_FLOOR)
    oml = 1.0 - lb

    def per_head(c0, widths, heads):
        parts, c = [], c0
        for width in widths:
            parts.append(w_in[:, :, c:c + heads * width].reshape(DEPTH, D, heads, width))
            c += heads * width
        return jnp.concatenate(parts, axis=-1).reshape(DEPTH, D, -1).astype(BF16), c

    w_gla, c_lr0 = per_head(0, (GLA_DK, GLA_DK, GLA_DV, GLA_DV), GLA_HEADS)
    c_lr1 = c_lr0 + GLA_GATE_RANK
    w_hgrn, c_ab = per_head(c_lr1, (HGRN_EXPAND, HGRN_EXPAND, HGRN_DV, HGRN_DV), HGRN_HEADS)
    w_ab = w_in[:, :, c_ab:].astype(BF16)
    assert w_ab.shape[2] == 2 * D
    w_lr = jnp.pad(w_in[:, :, c_lr0:c_lr1], ((0, 0), (0, 0), (0, LANES - GLA_GATE_RANK))).astype(BF16)
    w_lr2 = jnp.pad(gla_w_lr2, ((0, 0), (0, LANES - GLA_GATE_RANK), (0, 0))).astype(BF16)
    w_o = w_out.astype(BF16)
    w_a = w_ffn_in[:, :, :D_FF].astype(BF16)
    w_u = w_ffn_in[:, :, D_FF:].astype(BF16)
    w_f = w_ffn_out.astype(BF16)

    x = jnp.concatenate([_to_rows(x_prompt), _to_rows(x_sample)], axis=0)
    xn, g_gla = _prenorm_call(x, norm_mix_pre[0], w_lr, w_lr2, gla_b_lr[0], TM)

    n_sample_groups = BS // NS
    gla_p = gla_s = hg_p = hg_s = None
    for l in range(DEPTH):
        og, gla_p = _recurrence_call(xn, w_gla, g_gla, None, gla_onorm[l], None, None, layer=l, hgrn=False,
                                     C=PROMPT_CHUNK, row0=0, n_steps=L // PROMPT_CHUNK,
                                     carry_state=True, out_rows=T, heads_per_step=PROMPT_HEADS_PER_STEP, prev_states=gla_p)
        og, gla_s = _recurrence_call(xn, w_gla, g_gla, state_gla, gla_onorm[l], None, None, layer=l, hgrn=False,
                                     C=LS, row0=TP, n_steps=n_sample_groups,
                                     carry_state=False, out_rows=T, heads_per_step=SAMPLE_HEADS_PER_STEP, prev_out=og, prev_states=gla_s)
        oh, hg_p = _recurrence_call(xn, w_hgrn, None, None, hgrn_onorm[l], lbf[l], oml[l], layer=l, hgrn=True,
                                    C=PROMPT_CHUNK, row0=0, n_steps=L // PROMPT_CHUNK,
                                    carry_state=True, out_rows=T, heads_per_step=PROMPT_HEADS_PER_STEP, prev_states=hg_p)
        oh, hg_s = _recurrence_call(xn, w_hgrn, None, state_hgrn, hgrn_onorm[l], lbf[l], oml[l], layer=l, hgrn=True,
                                    C=LS, row0=TP, n_steps=n_sample_groups,
                                    carry_state=False, out_rows=T, heads_per_step=SAMPLE_HEADS_PER_STEP, prev_out=oh, prev_states=hg_s)

        if l + 1 < DEPTH:
            nxt = (norm_mix_pre[l + 1], w_lr, w_lr2, gla_b_lr[l + 1])
            x, xn, g_gla = _post_call(og, oh, xn, x, w_ab, w_o, w_a, w_u, w_f, norm_mix_post[l],
                                      norm_ffn_pre[l], norm_ffn_post[l], nxt, layer=l, tm=POST_ROWS)
        else:
            (x,) = _post_call(og, oh, xn, x, w_ab, w_o, w_a, w_u, w_f, norm_mix_post[l],
                              norm_ffn_pre[l], norm_ffn_post[l], None, layer=l, tm=POST_ROWS)

    y_p = _from_rows(x[:TP], B, L)
    y_s = _from_rows(x[TP:], BS, LS)
    return (y_p, y_s, gla_p, gla_s, hg_p, hg_s)
```

```python
import functools
import math

import jax
import jax.numpy as jnp
from jax import lax
from jax.experimental import pallas as pl
from jax.experimental.pallas import tpu as pltpu

F32 = jnp.float32
BF16 = jnp.bfloat16

D_MODEL = 1024
DEPTH = 4
GLA_HEADS = 4
GLA_DK = 128
GLA_DV = 256
GLA_GATE_RANK = 16
GLA_TAU = 16.0
HGRN_HEADS = 8
HGRN_EXPAND = 128
HGRN_DV = 128
LB_FLOOR = 1e-20
D_FF = 2816
EPS = 1e-6
LOG2E = 1.4426950408889634

SUBLANES = 8
LANES = 128
MXU_COLS = 256
SEQ_PER_ROWGROUP = SUBLANES

PROMPT_CHUNK = 64
PROMPT_HEADS_PER_STEP = 2
SAMPLE_HEADS_PER_STEP = 4
FINE_BLOCK = SUBLANES
POST_ROWS = 512
VMEM_LIMIT = 52 * 1024 * 1024


def _cparams(sem):
    return pltpu.CompilerParams(dimension_semantics=sem, vmem_limit_bytes=VMEM_LIMIT)


def _rms(x, w):
    return x * lax.rsqrt(jnp.mean(x * x, axis=-1, keepdims=True) + EPS) * w


def _sigmoid(x):
    return 0.5 + 0.5 * jnp.tanh(0.5 * x)


def _gla_gate(xn, wlr, wlr2, blr):
    glr = jnp.dot(xn, wlr, preferred_element_type=F32)
    y = jnp.dot(glr.astype(BF16), wlr2, preferred_element_type=F32) + blr
    ls = -(jnp.maximum(-y, 0.0) + jnp.log1p(jnp.exp(-jnp.abs(y))))
    return ls * (1.0 / GLA_TAU)


def _prenorm_kernel(x_ref, w_ref, wlr_ref, wlr2_ref, blr_ref, xn_ref, g_ref):
    xn = _rms(x_ref[...], w_ref[...]).astype(BF16)
    xn_ref[...] = xn
    g_ref[...] = _gla_gate(xn, wlr_ref[...], wlr2_ref[...], blr_ref[...])


def _prenorm_call(x, w, w_lr, w_lr2, b_lr, tm):
    T, D = x.shape
    R, N = w_lr2.shape[1:]
    row = lambda i: (i, 0)
    const = lambda i: (0, 0)
    l0 = lambda i: (0, 0, 0)
    return pl.pallas_call(
        _prenorm_kernel,
        grid=(T // tm,),
        in_specs=[pl.BlockSpec((tm, D), row),
                  pl.BlockSpec((1, D), const),
                  pl.BlockSpec((None, D, R), l0),
                  pl.BlockSpec((None, R, N), l0),
                  pl.BlockSpec((1, N), const)],
        out_specs=[pl.BlockSpec((tm, D), row), pl.BlockSpec((tm, N), row)],
        out_shape=[jax.ShapeDtypeStruct((T, D), BF16), jax.ShapeDtypeStruct((T, N), F32)],
        compiler_params=_cparams(("parallel",)),
        name="prenorm",
    )(x, w.reshape(1, D), w_lr, w_lr2, b_lr.reshape(1, N))


def _gram_levels(C):
    out = []
    m = C // 2
    while m >= FINE_BLOCK:
        out.append(m)
        m //= 2
    return out


def _recurrence_kernel(*refs, C, dv, hgrn, carry_state, q_scale, n_aliased, HB):
    nv = dv // LANES
    nblk = 2 + 2 * nv
    it = iter(refs)
    xcur_ref = next(it)
    xnext_ref = next(it)
    w_refs = [next(it) for _ in range(4)]
    g_ref = None if hgrn else next(it)
    onw_ref = next(it)
    lbf_ref = next(it) if hgrn else None
    oml_ref = next(it) if hgrn else None
    s0_ref = None if carry_state else next(it)
    for _ in range(n_aliased):
        next(it)
    o_ref = next(it)
    sout_ref = next(it)
    p_scr = next(it)
    pnext_scr = next(it)
    a_scr = next(it)
    b_scr = next(it)
    zs_ref = next(it)
    kk_scr = next(it) if hgrn else None
    oacc_ref = next(it)
    dcol_ref = next(it)
    S_ref = next(it) if carry_state else sout_ref

    FINE = min(C, FINE_BLOCK)
    levels = _gram_levels(C)
    NL = len(levels)
    NS = SEQ_PER_ROWGROUP
    c_idx = pl.program_id(1)

    if carry_state:
        @pl.when(c_idx == 0)
        def _():
            S_ref[...] = jnp.zeros_like(S_ref)
    else:
        S_ref[...] = s0_ref[...]

    def project(x_ref):
        x = x_ref[...]
        base = 0
        for w_ref in w_refs:
            p = jnp.dot(x, w_ref[...], preferred_element_type=F32)
            n = w_ref.shape[1] // LANES
            for i in range(n):
                pnext_scr[base + i] = p[:, i * LANES:(i + 1) * LANES]
            base += n

    @pl.when(c_idx == 0)
    def _():
        project(xcur_ref)

    p_scr[...] = pnext_scr[...]

    def rows(t):
        return pl.ds(t * NS, NS)

    def seq_rows(s):
        return pl.ds(s, C, stride=NS)

    def slab(hh, k):
        if k < 2:
            return k * HB + hh
        if k < 2 + nv:
            return 2 * HB + hh * nv + (k - 2)
        return (2 + nv) * HB + hh * nv + (k - 2 - nv)

    def vector_phase(hh):
        lanes = slice(hh * LANES, (hh + 1) * LANES)
        acc = jnp.zeros((NS, LANES), F32)
        for t in range(C):
            if hgrn:
                z = p_scr[slab(hh, 1), rows(t), :]
                u = jnp.exp(-jnp.abs(z))
                r = 1.0 / (1.0 + u)
                ur = u * r
                pos = z >= 0.0
                sig_p = jnp.where(pos, r, ur)
                sig_n = jnp.where(pos, ur, r)
                oml = oml_ref[:, lanes]
                g_t = jnp.log(lbf_ref[:, lanes] + oml * sig_p)
                kk_scr[hh, rows(t), :] = oml * sig_n
            else:
                g_t = g_ref[rows(t), lanes]
            acc = acc + g_t
            a_scr[hh, t] = jnp.exp2(g_t * LOG2E)
            b_scr[hh, t] = acc * LOG2E
        b_last = acc * LOG2E
        d_last = jnp.exp2(b_last)

        d_t = jnp.transpose(jnp.concatenate([d_last] * (LANES // NS), axis=0))
        for s in range(NS):
            dcol_ref[hh, s] = jnp.broadcast_to(d_t[:, s:s + 1], (LANES, LANES))

        for t in range(C):
            q_t = p_scr[slab(hh, 0), rows(t), :]
            if q_scale != 1.0:
                q_t = q_t * q_scale
            k_t = kk_scr[hh, rows(t), :] if hgrn else p_scr[slab(hh, 1), rows(t), :]
            b_t = b_scr[hh, t]
            zs_ref[hh, NL, rows(t), :] = q_t * jnp.exp2(b_t)
            zs_ref[hh, NL + 1, rows(t), :] = k_t * jnp.exp2(b_last - b_t)
            for j, m in enumerate(levels):
                r_idx = (t // (2 * m)) * (2 * m) + m - 1
                if (t % (2 * m)) >= m:
                    zs_ref[hh, j, rows(t), :] = q_t * jnp.exp2(b_t - b_scr[hh, r_idx])
                else:
                    zs_ref[hh, j, rows(t), :] = k_t * jnp.exp2(b_scr[hh, r_idx] - b_t)

        for t0 in range(0, C, FINE):
            q_blk, k_blk, v_blk, decay = {}, {}, {}, {}
            for t in range(t0, t0 + FINE):
                q_t = p_scr[slab(hh, 0), rows(t), :]
                if q_scale != 1.0:
                    q_t = q_t * q_scale
                q_blk[t] = q_t
                k_blk[t] = kk_scr[hh, rows(t), :] if hgrn else p_scr[slab(hh, 1), rows(t), :]
                v_blk[t] = [p_scr[slab(hh, 2 + i), rows(t), :] for i in range(nv)]
                a_t = a_scr[hh, t]
                for s in range(t0, t):
                    decay[s] = a_t if s == t - 1 else decay[s] * a_t
                w = jnp.sum(q_t * k_blk[t], axis=-1, keepdims=True)
                o_t = [w * v_blk[t][i] for i in range(nv)]
                for s in range(t0, t):
                    w = jnp.sum(q_t * (k_blk[s] * decay[s]), axis=-1, keepdims=True)
                    o_t = [o_t[i] + w * v_blk[s][i] for i in range(nv)]
                for i in range(nv):
                    oacc_ref[hh, i, rows(t), :] = o_t[i]

    ti = lax.broadcasted_iota(jnp.int32, (C, C), 0)
    si = lax.broadcasted_iota(jnp.int32, (C, C), 1)
    xr = jnp.bitwise_xor(ti, si)
    lower = ti > si
    masks = [lower & (xr >= m) & (xr < 2 * m) for m in levels]

    def matmul_phase(hh):
        def load_v(s):
            vs = [p_scr[slab(hh, 2 + i), seq_rows(s), :] for i in range(nv)]
            vb = vs[0] if nv == 1 else jnp.concatenate(vs, axis=1)
            return vb.astype(BF16)

        scores = []
        for s in range(NS if NL else 0):
            sc = jnp.zeros((C, C), F32)
            for j in range(NL):
                z = zs_ref[hh, j, seq_rows(s), :].astype(BF16)
                gram = lax.dot_general(z, z, (((1,), (1,)), ((), ())), preferred_element_type=F32)
                sc = jnp.where(masks[j], gram, sc)
            scores.append(sc.astype(BF16))
        inter = []
        for s in range(NS):
            zq = zs_ref[hh, NL, seq_rows(s), :].astype(BF16)
            zk = zs_ref[hh, NL + 1, seq_rows(s), :].astype(BF16)
            S = S_ref[s, hh]
            inter.append(jnp.dot(zq, S.astype(BF16), preferred_element_type=F32))
            upd = lax.dot_general(zk, load_v(s), (((0,), (0,)), ((), ())), preferred_element_type=F32)
            dc = dcol_ref[hh, s]
            if nv > 1:
                dc = jnp.concatenate([dc] * nv, axis=1)
            S_ref[s, hh] = dc * S + upd
        for s in range(NS):
            o = inter[s]
            if NL:
                o = o + jnp.dot(scores[s], load_v(s), preferred_element_type=F32)
            for i in range(nv):
                sl = slice(i * LANES, (i + 1) * LANES)
                oacc_ref[hh, i, seq_rows(s), :] = oacc_ref[hh, i, seq_rows(s), :] + o[:, sl]

    def gate_phase(hh):
        cols = slice(hh * dv, (hh + 1) * dv)
        if nv == 1:
            o_all = oacc_ref[hh, 0]
        else:
            o_all = jnp.concatenate([oacc_ref[hh, i] for i in range(nv)], axis=1)
        y = o_all * lax.rsqrt(jnp.mean(o_all * o_all, axis=-1, keepdims=True) + EPS) * onw_ref[...]
        gs = [p_scr[slab(hh, 2 + nv + i)] for i in range(nv)]
        gt = gs[0] if nv == 1 else jnp.concatenate(gs, axis=1)
        o_ref[:, cols] = y * (gt * _sigmoid(gt))

    for hh in range(HB):
        vector_phase(hh)
    project(xnext_ref)
    for hh in range(HB):
        matmul_phase(hh)
    for hh in range(HB):
        gate_phase(hh)

    if carry_state:
        @pl.when(c_idx == pl.num_programs(1) - 1)
        def _():
            sout_ref[...] = S_ref[...]


def _recurrence_call(xn, w_c, w_col0, g_gla, s_in, onw, lbf, oml, *, layer, hgrn, C, row0, n_steps, carry_state,
                     out_rows, heads_per_step, prev_out=None, prev_states=None):
    NS = SEQ_PER_ROWGROUP
    R = NS * C
    H = HGRN_HEADS if hgrn else GLA_HEADS
    dv = HGRN_DV if hgrn else GLA_DV
    dk = LANES
    nv = dv // LANES
    assert row0 % R == 0
    rb0 = row0 // R
    D = xn.shape[1]
    W = (2 + 2 * nv) * LANES
    HB = heads_per_step
    assert H % HB == 0
    last = n_steps - 1
    in_specs = [pl.BlockSpec((R, D), lambda h, c: (rb0 + c, 0)),
                pl.BlockSpec((R, D), lambda h, c: (rb0 + jnp.minimum(c + 1, last), 0))]
    args = [xn, xn]
    col = w_col0
    for width in (LANES, LANES, dv, dv):
        blk = HB * width
        assert col % blk == 0
        in_specs.append(pl.BlockSpec((None, D, blk), lambda h, c, j=col // blk: (layer, 0, j + h)))
        args.append(w_c)
        col += H * width
    if not hgrn:
        in_specs.append(pl.BlockSpec((R, HB * LANES), lambda h, c: (rb0 + c, h)))
        args.append(g_gla)
    in_specs.append(pl.BlockSpec((1, dv), lambda h, c: (0, 0)))
    args.append(onw.reshape(1, dv))
    if hgrn:
        in_specs += [pl.BlockSpec((1, HB * LANES), lambda h, c: (0, h)),
                     pl.BlockSpec((1, HB * LANES), lambda h, c: (0, h))]
        args += [lbf.reshape(1, -1), oml.reshape(1, -1)]
    n_groups = 1 if carry_state else n_steps
    if carry_state:
        s_map = lambda h, c: (layer, 0, h, 0, 0)
    else:
        s_map = lambda h, c: (layer, c, h, 0, 0)
        in_specs.append(pl.BlockSpec((None, NS, HB, dk, dv), s_map))
        args.append(s_in)
    aliases = {}
    for k, prev in enumerate((prev_out, prev_states)):
        if prev is not None:
            in_specs.append(pl.BlockSpec(memory_space=pl.ANY))
            args.append(prev)
            aliases[len(args) - 1] = k

    out_specs = [pl.BlockSpec((R, HB * dv), lambda h, c: (rb0 + c, h)),
                 pl.BlockSpec((None, NS, HB, dk, dv), s_map)]
    out_shape = [jax.ShapeDtypeStruct((out_rows, H * dv), F32),
                 jax.ShapeDtypeStruct((DEPTH, n_groups * NS, H, dk, dv), F32)]
    NL = len(_gram_levels(C))
    scratch = [pltpu.VMEM((HB * W // LANES, R, LANES), F32),
               pltpu.VMEM((HB * W // LANES, R, LANES), F32),
               pltpu.VMEM((HB, C, NS, LANES), F32),
               pltpu.VMEM((HB, C, NS, LANES), F32),
               pltpu.VMEM((HB, NL + 2, R, LANES), F32)]
    if hgrn:
        scratch.append(pltpu.VMEM((HB, R, LANES), F32))
    scratch += [pltpu.VMEM((HB, nv, R, LANES), F32),
                pltpu.VMEM((HB, NS, LANES, LANES), F32)]
    if carry_state:
        scratch.append(pltpu.VMEM((NS, HB, dk, dv), F32))

    kern = functools.partial(_recurrence_kernel, C=C, dv=dv, hgrn=hgrn, carry_state=carry_state,
                             q_scale=1.0 if hgrn else GLA_DK ** -0.5, n_aliased=len(aliases), HB=HB)
    return pl.pallas_call(
        kern,
        grid=(H // HB, n_steps),
        in_specs=in_specs,
        out_specs=out_specs,
        out_shape=out_shape,
        scratch_shapes=scratch,
        input_output_aliases=aliases,
        compiler_params=_cparams(("parallel", "arbitrary")),
        name=("hgrn" if hgrn else "gla") + ("_prompt" if carry_state else "_sample"),
    )(*args)


def _outproj_kernel(og_ref, oh_ref, xn_ref, x_ref, wga_ref, wgb_ref, wo_ref, npost_ref, npre_ref, h_ref, hn_ref):
    xn = xn_ref[...]
    ga = jnp.dot(xn, wga_ref[...], preferred_element_type=F32)
    gb = jnp.dot(xn, wgb_ref[...], preferred_element_type=F32)
    merged = og_ref[...] * _sigmoid(ga) + oh_ref[...] * _sigmoid(gb)
    m = jnp.dot(merged.astype(BF16), wo_ref[...], preferred_element_type=F32)
    h = x_ref[...] + _rms(m, npost_ref[...])
    h_ref[...] = h
    hn_ref[...] = _rms(h, npre_ref[...]).astype(BF16)


def _outproj_call(og, oh, xn, x, w_c, w_o, npost, npre, *, layer, tm):
    T, D = x.shape
    row = lambda i: (i, 0)
    const = lambda i: (0, 0)
    once = pl.Buffered(1)
    n_blk = w_c.shape[2] // D
    return pl.pallas_call(
        _outproj_kernel,
        grid=(T // tm,),
        in_specs=[pl.BlockSpec((tm, D), row),
                  pl.BlockSpec((tm, D), row),
                  pl.BlockSpec((tm, D), row),
                  pl.BlockSpec((tm, D), row),
                  pl.BlockSpec((None, D, D), lambda i: (layer, 0, n_blk - 2), pipeline_mode=once),
                  pl.BlockSpec((None, D, D), lambda i: (layer, 0, n_blk - 1), pipeline_mode=once),
                  pl.BlockSpec((None, D, D), lambda i: (layer, 0, 0), pipeline_mode=once),
                  pl.BlockSpec((1, D), const),
                  pl.BlockSpec((1, D), const)],
        out_specs=[pl.BlockSpec((tm, D), row), pl.BlockSpec((tm, D), row)],
        out_shape=[jax.ShapeDtypeStruct((T, D), F32), jax.ShapeDtypeStruct((T, D), BF16)],
        compiler_params=_cparams(("parallel",)),
        name="outproj",
    )(og, oh, xn, x, w_c, w_c, w_o, npost.reshape(1, D), npre.reshape(1, D))


def _ffn_kernel(*refs, splits, emit_next):
    hn_ref, h_ref, wa_ref, wu_ref, wf_ref, nfpost_ref = refs[:6]
    if emit_next:
        nnext_ref, wlr_ref, wlr2_ref, blr_ref, y_ref, yn_ref, g_ref = refs[6:]
    else:
        (y_ref,) = refs[6:]
    hn = hn_ref[...]
    f = None
    for lo, hi in splits:
        sl = slice(lo, hi)
        a = jnp.dot(hn, wa_ref[:, sl], preferred_element_type=F32)
        u = jnp.dot(hn, wu_ref[:, sl], preferred_element_type=F32)
        g = (a * _sigmoid(a) * u).astype(BF16)
        part = jnp.dot(g, wf_ref[sl, :], preferred_element_type=F32)
        f = part if f is None else f + part
    y = h_ref[...] + _rms(f, nfpost_ref[...])
    y_ref[...] = y
    if emit_next:
        yn = _rms(y, nnext_ref[...]).astype(BF16)
        yn_ref[...] = yn
        g_ref[...] = _gla_gate(yn, wlr_ref[...], wlr2_ref[...], blr_ref[...])


def _ffn_call(hn, h, w_ffn, w_f, nfpost, next_layer, *, layer, tm):
    T, D = h.shape
    F = w_f.shape[1]
    row = lambda i: (i, 0)
    const = lambda i: (0, 0)
    once = pl.Buffered(1)
    half = (F // MXU_COLS + 1) // 2 * MXU_COLS
    assert F % MXU_COLS == 0 and F % LANES == 0
    in_specs = [pl.BlockSpec((tm, D), row),
                pl.BlockSpec((tm, D), row),
                pl.BlockSpec((None, D, F), lambda i: (layer, 0, 0), pipeline_mode=once),
                pl.BlockSpec((None, D, F), lambda i: (layer, 0, 1), pipeline_mode=once),
                pl.BlockSpec((None, F, D), lambda i: (layer, 0, 0), pipeline_mode=once),
                pl.BlockSpec((1, D), const)]
    args = [hn, h, w_ffn, w_ffn, w_f, nfpost.reshape(1, D)]
    out_specs = [pl.BlockSpec((tm, D), row)]
    out_shape = [jax.ShapeDtypeStruct((T, D), F32)]
    if next_layer is not None:
        nnext, w_lr, w_lr2, b_lr = next_layer
        R, N = w_lr2.shape[1:]
        lnext = lambda i: (layer + 1, 0, 0)
        in_specs += [pl.BlockSpec((1, D), const),
                     pl.BlockSpec((None, D, R), lnext, pipeline_mode=once),
                     pl.BlockSpec((None, R, N), lnext, pipeline_mode=once),
                     pl.BlockSpec((1, N), const)]
        args += [nnext.reshape(1, D), w_lr, w_lr2, b_lr.reshape(1, N)]
        out_specs += [pl.BlockSpec((tm, D), row), pl.BlockSpec((tm, N), row)]
        out_shape += [jax.ShapeDtypeStruct((T, D), BF16), jax.ShapeDtypeStruct((T, N), F32)]
    return pl.pallas_call(
        functools.partial(_ffn_kernel, splits=((0, half), (half, F)), emit_next=next_layer is not None),
        grid=(T // tm,),
        in_specs=in_specs,
        out_specs=out_specs,
        out_shape=out_shape,
        compiler_params=_cparams(("parallel",)),
        name="ffn",
    )(*args)


def _to_rows(x):
    B, L, D = x.shape
    NS = SEQ_PER_ROWGROUP
    return x.reshape(B // NS, NS, L, D).transpose(0, 2, 1, 3).reshape(B * L, D)


def _from_rows(r, B, L):
    NS = SEQ_PER_ROWGROUP
    D = r.shape[-1]
    return r.reshape(B // NS, L, NS, D).transpose(0, 2, 1, 3).reshape(B, L, D)


def kernel(x_prompt, x_sample, state_gla, state_hgrn, norm_mix_pre, norm_mix_post, norm_ffn_pre, norm_ffn_post, w_in, gla_w_lr2, gla_b_lr, gla_onorm, hgrn_lb, hgrn_onorm, w_out, w_ffn_in, w_ffn_out):
    B, L, D = x_prompt.shape
    BS, LS, _ = x_sample.shape
    NS = SEQ_PER_ROWGROUP
    assert B == NS and BS % NS == 0 and L % PROMPT_CHUNK == 0
    TP = B * L
    TS = BS * LS
    T = TP + TS
    TM = 1024
    assert T % TM == 0

    sm = jax.nn.softmax(hgrn_lb.astype(F32), axis=0)
    lb = jnp.clip(jnp.cumsum(sm, axis=0) - sm[0:1], 0.0, 1.0 - 1e-6)
    lbf = jnp.maximum(lb, LB_FLOOR)
    oml = 1.0 - lb

    c_lr0 = GLA_HEADS * (2 * GLA_DK + 2 * GLA_DV)
    c_lr1 = c_lr0 + GLA_GATE_RANK
    w_c = jnp.concatenate([w_in[:, :, :c_lr0], w_in[:, :, c_lr1:]], axis=2).astype(BF16)
    col_hgrn = c_lr0
    assert w_c.shape[2] == col_hgrn + HGRN_HEADS * (2 * HGRN_EXPAND + 2 * HGRN_DV) + 2 * D
    w_lr = jnp.pad(w_in[:, :, c_lr0:c_lr1], ((0, 0), (0, 0), (0, LANES - GLA_GATE_RANK))).astype(BF16)
    w_lr2 = jnp.pad(gla_w_lr2, ((0, 0), (0, LANES - GLA_GATE_RANK), (0, 0))).astype(BF16)
    w_o = w_out.astype(BF16)
    w_ffn = w_ffn_in.astype(BF16)
    w_f = w_ffn_out.astype(BF16)

    x = jnp.concatenate([_to_rows(x_prompt), _to_rows(x_sample)], axis=0)
    xn, g_gla = _prenorm_call(x, norm_mix_pre[0], w_lr, w_lr2, gla_b_lr[0], TM)

    n_sample_groups = BS // NS
    gla_p = gla_s = hg_p = hg_s = None
    for l in range(DEPTH):
        og, gla_p = _recurrence_call(xn, w_c, 0, g_gla, None, gla_onorm[l], None, None, layer=l, hgrn=False,
                                     C=PROMPT_CHUNK, row0=0, n_steps=L // PROMPT_CHUNK,
                                     carry_state=True, out_rows=T, heads_per_step=PROMPT_HEADS_PER_STEP, prev_states=gla_p)
        og, gla_s = _recurrence_call(xn, w_c, 0, g_gla, state_gla, gla_onorm[l], None, None, layer=l, hgrn=False,
                                     C=LS, row0=TP, n_steps=n_sample_groups,
                                     carry_state=False, out_rows=T, heads_per_step=SAMPLE_HEADS_PER_STEP, prev_out=og, prev_states=gla_s)
        oh, hg_p = _recurrence_call(xn, w_c, col_hgrn, None, None, hgrn_onorm[l], lbf[l], oml[l], layer=l, hgrn=True,
                                    C=PROMPT_CHUNK, row0=0, n_steps=L // PROMPT_CHUNK,
                                    carry_state=True, out_rows=T, heads_per_step=PROMPT_HEADS_PER_STEP, prev_states=hg_p)
        oh, hg_s = _recurrence_call(xn, w_c, col_hgrn, None, state_hgrn, hgrn_onorm[l], lbf[l], oml[l], layer=l, hgrn=True,
                                    C=LS, row0=TP, n_steps=n_sample_groups,
                                    carry_state=False, out_rows=T, heads_per_step=SAMPLE_HEADS_PER_STEP, prev_out=oh, prev_states=hg_s)

        h, hn = _outproj_call(og, oh, xn, x, w_c, w_o, norm_mix_post[l], norm_ffn_pre[l], layer=l, tm=POST_ROWS)
        if l + 1 < DEPTH:
            nxt = (norm_mix_pre[l + 1], w_lr, w_lr2, gla_b_lr[l + 1])
            x, xn, g_gla = _ffn_call(hn, h, w_ffn, w_f, norm_ffn_post[l], nxt, layer=l, tm=POST_ROWS)
        else:
            (x,) = _ffn_call(hn, h, w_ffn, w_f, norm_ffn_post[l], None, layer=l, tm=POST_ROWS)

    y_p = _from_rows(x[:TP], B, L)
    y_s = _from_rows(x[TP:], BS, LS)
    return (y_p, y_s, gla_p, gla_s, hg_p, hg_s)
```

```python
import functools
import math

import jax
import jax.numpy as jnp
from jax import lax
from jax.experimental import pallas as pl
from jax.experimental.pallas import tpu as pltpu

F32 = jnp.float32
BF16 = jnp.bfloat16

D_MODEL = 1024
DEPTH = 4
GLA_HEADS = 4
GLA_DK = 128
GLA_DV = 256
GLA_GATE_RANK = 16
GLA_TAU = 16.0
HGRN_HEADS = 8
HGRN_EXPAND = 128
HGRN_DV = 128
LB_FLOOR = 1e-20
D_FF = 2816
EPS = 1e-6
LOG2E = 1.4426950408889634

SUBLANES = 8
LANES = 128
MXU_COLS = 256
SEQ_PER_ROWGROUP = SUBLANES

PROMPT_CHUNK = 64
PROMPT_HEADS_PER_STEP = 2
SAMPLE_HEADS_PER_STEP = 4
FINE_BLOCK = SUBLANES
POST_ROWS = 512
VMEM_LIMIT = 52 * 1024 * 1024


def _cparams(sem):
    return pltpu.CompilerParams(dimension_semantics=sem, vmem_limit_bytes=VMEM_LIMIT)


def _rms(x, w):
    return x * lax.rsqrt(jnp.mean(x * x, axis=-1, keepdims=True) + EPS) * w


def _sigmoid(x):
    return 0.5 + 0.5 * jnp.tanh(0.5 * x)


def _gla_gate(xn, wlr, wlr2, blr):
    glr = jnp.dot(xn, wlr, preferred_element_type=F32)
    y = jnp.dot(glr.astype(BF16), wlr2, preferred_element_type=F32) + blr
    ls = -(jnp.maximum(-y, 0.0) + jnp.log1p(jnp.exp(-jnp.abs(y))))
    return ls * (1.0 / GLA_TAU)


def _tile_rows(seq, n_time):
    NS = SEQ_PER_ROWGROUP
    return pl.ds((seq // NS) * n_time * NS + seq % NS, n_time, stride=NS)


def _prenorm_kernel(*refs, n_aliased):
    x_ref, w_ref, wlr_ref, wlr2_ref, blr_ref = refs[:5]
    xr_ref, xn_ref, g_ref, rows_scr = refs[5 + n_aliased:]
    n_seq, n_time, D = x_ref.shape
    for seq in range(n_seq):
        for j in range(D // LANES):
            rows_scr[j, _tile_rows(seq, n_time), :] = x_ref[seq, :, j * LANES:(j + 1) * LANES]
    x = jnp.concatenate([rows_scr[j] for j in range(D // LANES)], axis=1)
    xr_ref[...] = x
    xn = _rms(x, w_ref[...]).astype(BF16)
    xn_ref[...] = xn
    g_ref[...] = _gla_gate(xn, wlr_ref[...], wlr2_ref[...], blr_ref[...])


def _prenorm_call(x3, w, w_lr, w_lr2, b_lr, *, n_seq, n_time, row0, total_rows, prev=None):
    B, L, D = x3.shape
    R, N = w_lr2.shape[1:]
    tm = n_seq * n_time
    assert B % n_seq == 0 and L % n_time == 0 and (B == n_seq or L == n_time) and row0 % tm == 0
    nb, nt = B // n_seq, L // n_time
    rb0 = row0 // tm
    row = lambda i: (rb0 + i, 0)
    const = lambda i: (0, 0)
    l0 = lambda i: (0, 0, 0)
    in_specs = [pl.BlockSpec((n_seq, n_time, D), lambda i: (i // nt, i % nt, 0)),
                pl.BlockSpec((1, D), const),
                pl.BlockSpec((None, D, R), l0),
                pl.BlockSpec((None, R, N), l0),
                pl.BlockSpec((1, N), const)]
    args = [x3, w.reshape(1, D), w_lr, w_lr2, b_lr.reshape(1, N)]
    aliases = {}
    if prev is not None:
        for k, p in enumerate(prev):
            in_specs.append(pl.BlockSpec(memory_space=pl.ANY))
            args.append(p)
            aliases[len(args) - 1] = k
    return pl.pallas_call(
        functools.partial(_prenorm_kernel, n_aliased=len(aliases)),
        grid=(nb * nt,),
        in_specs=in_specs,
        out_specs=[pl.BlockSpec((tm, D), row), pl.BlockSpec((tm, D), row), pl.BlockSpec((tm, N), row)],
        out_shape=[jax.ShapeDtypeStruct((total_rows, D), F32), jax.ShapeDtypeStruct((total_rows, D), BF16),
                   jax.ShapeDtypeStruct((total_rows, N), F32)],
        scratch_shapes=[pltpu.VMEM((D // LANES, tm, LANES), F32)],
        input_output_aliases=aliases,
        compiler_params=_cparams(("parallel",)),
        name="prenorm",
    )(*args)


def _gram_levels(C):
    out = []
    m = C // 2
    while m >= FINE_BLOCK:
        out.append(m)
        m //= 2
    return out


def _recurrence_kernel(*refs, C, dv, hgrn, carry_state, q_scale, n_aliased, HB):
    nv = dv // LANES
    nblk = 2 + 2 * nv
    it = iter(refs)
    xcur_ref = next(it)
    xnext_ref = next(it)
    w_refs = [next(it) for _ in range(4)]
    g_ref = None if hgrn else next(it)
    onw_ref = next(it)
    lbf_ref = next(it) if hgrn else None
    oml_ref = next(it) if hgrn else None
    s0_ref = None if carry_state else next(it)
    for _ in range(n_aliased):
        next(it)
    o_ref = next(it)
    sout_ref = next(it)
    p_scr = next(it)
    pnext_scr = next(it)
    a_scr = next(it)
    b_scr = next(it)
    zs_ref = next(it)
    kk_scr = next(it) if hgrn else None
    oacc_ref = next(it)
    dcol_ref = next(it)
    S_ref = next(it) if carry_state else sout_ref

    FINE = min(C, FINE_BLOCK)
    levels = _gram_levels(C)
    NL = len(levels)
    NS = SEQ_PER_ROWGROUP
    c_idx = pl.program_id(1)

    if carry_state:
        @pl.when(c_idx == 0)
        def _():
            S_ref[...] = jnp.zeros_like(S_ref)
    Sin_ref = S_ref if carry_state else s0_ref

    def project(x_ref):
        x = x_ref[...]
        base = 0
        for w_ref in w_refs:
            p = jnp.dot(x, w_ref[...], preferred_element_type=F32)
            n = w_ref.shape[1] // LANES
            for i in range(n):
                pnext_scr[base + i] = p[:, i * LANES:(i + 1) * LANES]
            base += n

    @pl.when(c_idx == 0)
    def _():
        project(xcur_ref)

    p_scr[...] = pnext_scr[...]

    def rows(t):
        return pl.ds(t * NS, NS)

    def seq_rows(s):
        return pl.ds(s, C, stride=NS)

    def slab(hh, k):
        if k < 2:
            return k * HB + hh
        if k < 2 + nv:
            return 2 * HB + hh * nv + (k - 2)
        return (2 + nv) * HB + hh * nv + (k - 2 - nv)

    def vector_phase(hh):
        lanes = slice(hh * LANES, (hh + 1) * LANES)
        acc = jnp.zeros((NS, LANES), F32)
        for t in range(C):
            if hgrn:
                z = p_scr[slab(hh, 1), rows(t), :]
                u = jnp.exp(-jnp.abs(z))
                r = 1.0 / (1.0 + u)
                ur = u * r
                pos = z >= 0.0
                sig_p = jnp.where(pos, r, ur)
                sig_n = jnp.where(pos, ur, r)
                oml = oml_ref[:, lanes]
                g_t = jnp.log(lbf_ref[:, lanes] + oml * sig_p)
                kk_scr[hh, rows(t), :] = oml * sig_n
            else:
                g_t = g_ref[rows(t), lanes]
            acc = acc + g_t
            a_scr[hh, t] = jnp.exp2(g_t * LOG2E)
            b_scr[hh, t] = acc * LOG2E
        b_last = acc * LOG2E
        d_last = jnp.exp2(b_last)

        d_t = jnp.transpose(jnp.concatenate([d_last] * (LANES // NS), axis=0))
        for s in range(NS):
            dcol_ref[hh, s] = jnp.broadcast_to(d_t[:, s:s + 1], (LANES, LANES))

        for t in range(C):
            q_t = p_scr[slab(hh, 0), rows(t), :]
            if q_scale != 1.0:
                q_t = q_t * q_scale
            k_t = kk_scr[hh, rows(t), :] if hgrn else p_scr[slab(hh, 1), rows(t), :]
            b_t = b_scr[hh, t]
            zs_ref[hh, NL, rows(t), :] = q_t * jnp.exp2(b_t)
            zs_ref[hh, NL + 1, rows(t), :] = k_t * jnp.exp2(b_last - b_t)
            for j, m in enumerate(levels):
                r_idx = (t // (2 * m)) * (2 * m) + m - 1
                if (t % (2 * m)) >= m:
                    zs_ref[hh, j, rows(t), :] = q_t * jnp.exp2(b_t - b_scr[hh, r_idx])
                else:
                    zs_ref[hh, j, rows(t), :] = k_t * jnp.exp2(b_scr[hh, r_idx] - b_t)

        for t0 in range(0, C, FINE):
            q_blk, k_blk, v_blk, decay = {}, {}, {}, {}
            for t in range(t0, t0 + FINE):
                q_t = p_scr[slab(hh, 0), rows(t), :]
                if q_scale != 1.0:
                    q_t = q_t * q_scale
                q_blk[t] = q_t
                k_blk[t] = kk_scr[hh, rows(t), :] if hgrn else p_scr[slab(hh, 1), rows(t), :]
                v_blk[t] = [p_scr[slab(hh, 2 + i), rows(t), :] for i in range(nv)]
                a_t = a_scr[hh, t]
                for s in range(t0, t):
                    decay[s] = a_t if s == t - 1 else decay[s] * a_t
                w = jnp.sum(q_t * k_blk[t], axis=-1, keepdims=True)
                o_t = [w * v_blk[t][i] for i in range(nv)]
                for s in range(t0, t):
                    w = jnp.sum(q_t * (k_blk[s] * decay[s]), axis=-1, keepdims=True)
                    o_t = [o_t[i] + w * v_blk[s][i] for i in range(nv)]
                for i in range(nv):
                    oacc_ref[hh, i, rows(t), :] = o_t[i]

    ti = lax.broadcasted_iota(jnp.int32, (C, C), 0)
    si = lax.broadcasted_iota(jnp.int32, (C, C), 1)
    xr = jnp.bitwise_xor(ti, si)
    lower = ti > si
    masks = [lower & (xr >= m) & (xr < 2 * m) for m in levels]

    def matmul_phase(hh):
        def load_v(s):
            vs = [p_scr[slab(hh, 2 + i), seq_rows(s), :] for i in range(nv)]
            vb = vs[0] if nv == 1 else jnp.concatenate(vs, axis=1)
            return vb.astype(BF16)

        scores = []
        for s in range(NS if NL else 0):
            sc = jnp.zeros((C, C), F32)
            for j in range(NL):
                z = zs_ref[hh, j, seq_rows(s), :].astype(BF16)
                gram = lax.dot_general(z, z, (((1,), (1,)), ((), ())), preferred_element_type=F32)
                sc = jnp.where(masks[j], gram, sc)
            scores.append(sc.astype(BF16))
        inter = []
        for s in range(NS):
            zq = zs_ref[hh, NL, seq_rows(s), :].astype(BF16)
            zk = zs_ref[hh, NL + 1, seq_rows(s), :].astype(BF16)
            S = Sin_ref[s, hh]
            inter.append(jnp.dot(zq, S.astype(BF16), preferred_element_type=F32))
            upd = lax.dot_general(zk, load_v(s), (((0,), (0,)), ((), ())), preferred_element_type=F32)
            dc = dcol_ref[hh, s]
            if nv > 1:
                dc = jnp.concatenate([dc] * nv, axis=1)
            S_ref[s, hh] = dc * S + upd
        for s in range(NS):
            o = inter[s]
            if NL:
                o = o + jnp.dot(scores[s], load_v(s), preferred_element_type=F32)
            for i in range(nv):
                sl = slice(i * LANES, (i + 1) * LANES)
                oacc_ref[hh, i, seq_rows(s), :] = oacc_ref[hh, i, seq_rows(s), :] + o[:, sl]

    def gate_phase(hh):
        cols = slice(hh * dv, (hh + 1) * dv)
        if nv == 1:
            o_all = oacc_ref[hh, 0]
        else:
            o_all = jnp.concatenate([oacc_ref[hh, i] for i in range(nv)], axis=1)
        y = o_all * lax.rsqrt(jnp.mean(o_all * o_all, axis=-1, keepdims=True) + EPS) * onw_ref[...]
        gs = [p_scr[slab(hh, 2 + nv + i)] for i in range(nv)]
        gt = gs[0] if nv == 1 else jnp.concatenate(gs, axis=1)
        o_ref[:, cols] = y * (gt * _sigmoid(gt))

    for hh in range(HB):
        vector_phase(hh)
    project(xnext_ref)
    for hh in range(HB):
        matmul_phase(hh)
    for hh in range(HB):
        gate_phase(hh)

    if carry_state:
        @pl.when(c_idx == pl.num_programs(1) - 1)
        def _():
            sout_ref[...] = S_ref[...]


def _recurrence_call(xn, w_c, w_col0, g_gla, s_in, onw, lbf, oml, *, layer, hgrn, C, row0, n_steps, carry_state,
                     out_rows, heads_per_step, prev_out=None, prev_states=None):
    NS = SEQ_PER_ROWGROUP
    R = NS * C
    H = HGRN_HEADS if hgrn else GLA_HEADS
    dv = HGRN_DV if hgrn else GLA_DV
    dk = LANES
    nv = dv // LANES
    assert row0 % R == 0
    rb0 = row0 // R
    D = xn.shape[1]
    W = (2 + 2 * nv) * LANES
    HB = heads_per_step
    assert H % HB == 0
    last = n_steps - 1
    in_specs = [pl.BlockSpec((R, D), lambda h, c: (rb0 + c, 0)),
                pl.BlockSpec((R, D), lambda h, c: (rb0 + jnp.minimum(c + 1, last), 0))]
    args = [xn, xn]
    col = w_col0
    for width in (LANES, LANES, dv, dv):
        blk = HB * width
        assert col % blk == 0
        in_specs.append(pl.BlockSpec((None, D, blk), lambda h, c, j=col // blk: (layer, 0, j + h)))
        args.append(w_c)
        col += H * width
    if not hgrn:
        in_specs.append(pl.BlockSpec((R, HB * LANES), lambda h, c: (rb0 + c, h)))
        args.append(g_gla)
    in_specs.append(pl.BlockSpec((1, dv), lambda h, c: (0, 0)))
    args.append(onw.reshape(1, dv))
    if hgrn:
        in_specs += [pl.BlockSpec((1, HB * LANES), lambda h, c: (0, h)),
                     pl.BlockSpec((1, HB * LANES), lambda h, c: (0, h))]
        args += [lbf.reshape(1, -1), oml.reshape(1, -1)]
    n_groups = 1 if carry_state else n_steps
    if carry_state:
        s_map = lambda h, c: (layer, 0, h, 0, 0)
    else:
        s_map = lambda h, c: (layer, c, h, 0, 0)
        in_specs.append(pl.BlockSpec((None, NS, HB, dk, dv), s_map))
        args.append(s_in)
    aliases = {}
    for k, prev in enumerate((prev_out, prev_states)):
        if prev is not None:
            in_specs.append(pl.BlockSpec(memory_space=pl.ANY))
            args.append(prev)
            aliases[len(args) - 1] = k

    out_specs = [pl.BlockSpec((R, HB * dv), lambda h, c: (rb0 + c, h)),
                 pl.BlockSpec((None, NS, HB, dk, dv), s_map)]
    out_shape = [jax.ShapeDtypeStruct((out_rows, H * dv), F32),
                 jax.ShapeDtypeStruct((DEPTH, n_groups * NS, H, dk, dv), F32)]
    NL = len(_gram_levels(C))
    scratch = [pltpu.VMEM((HB * W // LANES, R, LANES), F32),
               pltpu.VMEM((HB * W // LANES, R, LANES), F32),
               pltpu.VMEM((HB, C, NS, LANES), F32),
               pltpu.VMEM((HB, C, NS, LANES), F32),
               pltpu.VMEM((HB, NL + 2, R, LANES), F32)]
    if hgrn:
        scratch.append(pltpu.VMEM((HB, R, LANES), F32))
    scratch += [pltpu.VMEM((HB, nv, R, LANES), F32),
                pltpu.VMEM((HB, NS, LANES, LANES), F32)]
    if carry_state:
        scratch.append(pltpu.VMEM((NS, HB, dk, dv), F32))

    kern = functools.partial(_recurrence_kernel, C=C, dv=dv, hgrn=hgrn, carry_state=carry_state,
                             q_scale=1.0 if hgrn else GLA_DK ** -0.5, n_aliased=len(aliases), HB=HB)
    return pl.pallas_call(
        kern,
        grid=(H // HB, n_steps),
        in_specs=in_specs,
        out_specs=out_specs,
        out_shape=out_shape,
        scratch_shapes=scratch,
        input_output_aliases=aliases,
        compiler_params=_cparams(("parallel", "arbitrary")),
        name=("hgrn" if hgrn else "gla") + ("_prompt" if carry_state else "_sample"),
    )(*args)


def _outproj_kernel(og_ref, oh_ref, xn_ref, x_ref, wga_ref, wgb_ref, wo_ref, npost_ref, npre_ref, h_ref, hn_ref):
    xn = xn_ref[...]
    ga = jnp.dot(xn, wga_ref[...], preferred_element_type=F32)
    gb = jnp.dot(xn, wgb_ref[...], preferred_element_type=F32)
    merged = og_ref[...] * _sigmoid(ga) + oh_ref[...] * _sigmoid(gb)
    m = jnp.dot(merged.astype(BF16), wo_ref[...], preferred_element_type=F32)
    h = x_ref[...] + _rms(m, npost_ref[...])
    h_ref[...] = h
    hn_ref[...] = _rms(h, npre_ref[...]).astype(BF16)


def _outproj_call(og, oh, xn, x, w_c, w_o, npost, npre, *, layer, tm):
    T, D = x.shape
    row = lambda i: (i, 0)
    const = lambda i: (0, 0)
    once = pl.Buffered(1)
    n_blk = w_c.shape[2] // D
    return pl.pallas_call(
        _outproj_kernel,
        grid=(T // tm,),
        in_specs=[pl.BlockSpec((tm, D), row),
                  pl.BlockSpec((tm, D), row),
                  pl.BlockSpec((tm, D), row),
                  pl.BlockSpec((tm, D), row),
                  pl.BlockSpec((None, D, D), lambda i: (layer, 0, n_blk - 2), pipeline_mode=once),
                  pl.BlockSpec((None, D, D), lambda i: (layer, 0, n_blk - 1), pipeline_mode=once),
                  pl.BlockSpec((None, D, D), lambda i: (layer, 0, 0), pipeline_mode=once),
                  pl.BlockSpec((1, D), const),
                  pl.BlockSpec((1, D), const)],
        out_specs=[pl.BlockSpec((tm, D), row), pl.BlockSpec((tm, D), row)],
        out_shape=[jax.ShapeDtypeStruct((T, D), F32), jax.ShapeDtypeStruct((T, D), BF16)],
        compiler_params=_cparams(("parallel",)),
        name="outproj",
    )(og, oh, xn, x, w_c, w_c, w_o, npost.reshape(1, D), npre.reshape(1, D))


def _ffn_kernel(*refs, splits, emit_next):
    hn_ref, h_ref, wa_ref, wu_ref, wf_ref, nfpost_ref = refs[:6]
    if emit_next:
        nnext_ref, wlr_ref, wlr2_ref, blr_ref, y_ref, yn_ref, g_ref = refs[6:]
    else:
        y_ref, rows_scr = refs[6:]
    hn = hn_ref[...]
    f = None
    for lo, hi in splits:
        sl = slice(lo, hi)
        a = jnp.dot(hn, wa_ref[:, sl], preferred_element_type=F32)
        u = jnp.dot(hn, wu_ref[:, sl], preferred_element_type=F32)
        g = (a * _sigmoid(a) * u).astype(BF16)
        part = jnp.dot(g, wf_ref[sl, :], preferred_element_type=F32)
        f = part if f is None else f + part
    y = h_ref[...] + _rms(f, nfpost_ref[...])
    if emit_next:
        y_ref[...] = y
        yn = _rms(y, nnext_ref[...]).astype(BF16)
        yn_ref[...] = yn
        g_ref[...] = _gla_gate(yn, wlr_ref[...], wlr2_ref[...], blr_ref[...])
    else:
        n_seq, n_time, D = y_ref.shape
        for j in range(D // LANES):
            rows_scr[j] = y[:, j * LANES:(j + 1) * LANES]
        for seq in range(n_seq):
            for j in range(D // LANES):
                y_ref[seq, :, j * LANES:(j + 1) * LANES] = rows_scr[j, _tile_rows(seq, n_time), :]


def _ffn_call(hn, h, w_ffn, w_f, nfpost, next_layer, *, layer, tm, final=None):
    T, D = h.shape
    F = w_f.shape[1]
    rb0, n_tiles = 0, T // tm
    if final is not None:
        row0, B, L, n_seq, n_time = final
        assert n_seq * n_time == tm and row0 % tm == 0 and (B == n_seq or L == n_time)
        rb0, n_tiles, nt = row0 // tm, B * L // tm, L // n_time
    row = lambda i: (rb0 + i, 0)
    const = lambda i: (0, 0)
    once = pl.Buffered(1)
    half = (F // MXU_COLS + 1) // 2 * MXU_COLS
    assert F % MXU_COLS == 0 and F % LANES == 0
    in_specs = [pl.BlockSpec((tm, D), row),
                pl.BlockSpec((tm, D), row),
                pl.BlockSpec((None, D, F), lambda i: (layer, 0, 0), pipeline_mode=once),
                pl.BlockSpec((None, D, F), lambda i: (layer, 0, 1), pipeline_mode=once),
                pl.BlockSpec((None, F, D), lambda i: (layer, 0, 0), pipeline_mode=once),
                pl.BlockSpec((1, D), const)]
    args = [hn, h, w_ffn, w_ffn, w_f, nfpost.reshape(1, D)]
    scratch = []
    if next_layer is None:
        out_specs = [pl.BlockSpec((n_seq, n_time, D), lambda i: (i // nt, i % nt, 0))]
        out_shape = [jax.ShapeDtypeStruct((B, L, D), F32)]
        scratch = [pltpu.VMEM((D // LANES, tm, LANES), F32)]
    else:
        out_specs = [pl.BlockSpec((tm, D), row)]
        out_shape = [jax.ShapeDtypeStruct((T, D), F32)]
        nnext, w_lr, w_lr2, b_lr = next_layer
        R, N = w_lr2.shape[1:]
        lnext = lambda i: (layer + 1, 0, 0)
        in_specs += [pl.BlockSpec((1, D), const),
                     pl.BlockSpec((None, D, R), lnext, pipeline_mode=once),
                     pl.BlockSpec((None, R, N), lnext, pipeline_mode=once),
                     pl.BlockSpec((1, N), const)]
        args += [nnext.reshape(1, D), w_lr, w_lr2, b_lr.reshape(1, N)]
        out_specs += [pl.BlockSpec((tm, D), row), pl.BlockSpec((tm, N), row)]
        out_shape += [jax.ShapeDtypeStruct((T, D), BF16), jax.ShapeDtypeStruct((T, N), F32)]
    return pl.pallas_call(
        functools.partial(_ffn_kernel, splits=((0, half), (half, F)), emit_next=next_layer is not None),
        grid=(n_tiles,),
        in_specs=in_specs,
        out_specs=out_specs,
        out_shape=out_shape,
        scratch_shapes=scratch,
        compiler_params=_cparams(("parallel",)),
        name="ffn",
    )(*args)


def kernel(x_prompt, x_sample, state_gla, state_hgrn, norm_mix_pre, norm_mix_post, norm_ffn_pre, norm_ffn_post, w_in, gla_w_lr2, gla_b_lr, gla_onorm, hgrn_lb, hgrn_onorm, w_out, w_ffn_in, w_ffn_out):
    B, L, D = x_prompt.shape
    BS, LS, _ = x_sample.shape
    NS = SEQ_PER_ROWGROUP
    assert B == NS and BS % NS == 0 and L % PROMPT_CHUNK == 0
    TP = B * L
    TS = BS * LS
    T = TP + TS
    assert TP % POST_ROWS == 0 and TS % POST_ROWS == 0

    sm = jax.nn.softmax(hgrn_lb.astype(F32), axis=0)
    lb = jnp.clip(jnp.cumsum(sm, axis=0) - sm[0:1], 0.0, 1.0 - 1e-6)
    lbf = jnp.maximum(lb, LB_FLOOR)
    oml = 1.0 - lb

    c_lr0 = GLA_HEADS * (2 * GLA_DK + 2 * GLA_DV)
    c_lr1 = c_lr0 + GLA_GATE_RANK
    w_g = w_in[:, :, :c_lr0].astype(BF16)
    w_h = w_in[:, :, c_lr1:].astype(BF16)
    assert w_h.shape[2] == HGRN_HEADS * (2 * HGRN_EXPAND + 2 * HGRN_DV) + 2 * D
    w_lr = jnp.pad(w_in[:, :, c_lr0:c_lr1], ((0, 0), (0, 0), (0, LANES - GLA_GATE_RANK))).astype(BF16)
    w_lr2 = jnp.pad(gla_w_lr2, ((0, 0), (0, LANES - GLA_GATE_RANK), (0, 0))).astype(BF16)
    w_o = w_out.astype(BF16)
    w_ffn = w_ffn_in.astype(BF16)
    w_f = w_ffn_out.astype(BF16)

    P_TIME = POST_ROWS // NS
    S_SEQ = POST_ROWS // LS
    pre = _prenorm_call(x_prompt, norm_mix_pre[0], w_lr, w_lr2, gla_b_lr[0],
                        n_seq=NS, n_time=P_TIME, row0=0, total_rows=T)
    x, xn, g_gla = _prenorm_call(x_sample, norm_mix_pre[0], w_lr, w_lr2, gla_b_lr[0],
                                 n_seq=S_SEQ, n_time=LS, row0=TP, total_rows=T, prev=pre)

    n_sample_groups = BS // NS
    gla_p = gla_s = hg_p = hg_s = None
    for l in range(DEPTH):
        og, gla_p = _recurrence_call(xn, w_g, 0, g_gla, None, gla_onorm[l], None, None, layer=l, hgrn=False,
                                     C=PROMPT_CHUNK, row0=0, n_steps=L // PROMPT_CHUNK,
                                     carry_state=True, out_rows=T, heads_per_step=PROMPT_HEADS_PER_STEP, prev_states=gla_p)
        og, gla_s = _recurrence_call(xn, w_g, 0, g_gla, state_gla, gla_onorm[l], None, None, layer=l, hgrn=False,
                                     C=LS, row0=TP, n_steps=n_sample_groups,
                                     carry_state=False, out_rows=T, heads_per_step=SAMPLE_HEADS_PER_STEP, prev_out=og, prev_states=gla_s)
        oh, hg_p = _recurrence_call(xn, w_h, 0, None, None, hgrn_onorm[l], lbf[l], oml[l], layer=l, hgrn=True,
                                    C=PROMPT_CHUNK, row0=0, n_steps=L // PROMPT_CHUNK,
                                    carry_state=True, out_rows=T, heads_per_step=PROMPT_HEADS_PER_STEP, prev_states=hg_p)
        oh, hg_s = _recurrence_call(xn, w_h, 0, None, state_hgrn, hgrn_onorm[l], lbf[l], oml[l], layer=l, hgrn=True,
                                    C=LS, row0=TP, n_steps=n_sample_groups,
                                    carry_state=False, out_rows=T, heads_per_step=SAMPLE_HEADS_PER_STEP, prev_out=oh, prev_states=hg_s)

        h, hn = _outproj_call(og, oh, xn, x, w_h, w_o, norm_mix_post[l], norm_ffn_pre[l], layer=l, tm=POST_ROWS)
        if l + 1 < DEPTH:
            nxt = (norm_mix_pre[l + 1], w_lr, w_lr2, gla_b_lr[l + 1])
            x, xn, g_gla = _ffn_call(hn, h, w_ffn, w_f, norm_ffn_post[l], nxt, layer=l, tm=POST_ROWS)
        else:
            (y_p,) = _ffn_call(hn, h, w_ffn, w_f, norm_ffn_post[l], None, layer=l, tm=POST_ROWS,
                               final=(0, B, L, NS, P_TIME))
            (y_s,) = _ffn_call(hn, h, w_ffn, w_f, norm_ffn_post[l], None, layer=l, tm=POST_ROWS,
                               final=(TP, BS, LS, S_SEQ, LS))

    return (y_p, y_s, gla_p, gla_s, hg_p, hg_s)
```

```python
import functools
from types import SimpleNamespace

import jax
import jax.numpy as jnp
from jax import lax
from jax.experimental import pallas as pl
from jax.experimental.pallas import tpu as pltpu

F32 = jnp.float32
BF16 = jnp.bfloat16

D_MODEL = 1024
DEPTH = 4
GLA_HEADS = 4
GLA_DK = 128
GLA_DV = 256
GLA_GATE_RANK = 16
GLA_TAU = 16.0
HGRN_HEADS = 8
HGRN_EXPAND = 128
HGRN_DV = 128
LB_FLOOR = 1e-20
D_FF = 2816
EPS = 1e-6
LOG2E = 1.4426950408889634

SUBLANES = 8
LANES = 128
MXU_COLS = 256
SEQ_PER_ROWGROUP = SUBLANES

PROMPT_CHUNK = 64
HEADS_PER_STEP = 2
FINE_BLOCK = SUBLANES
POST_ROWS = 512
VMEM_LIMIT = 52 * 1024 * 1024


def _cparams(sem):
    return pltpu.CompilerParams(dimension_semantics=sem, vmem_limit_bytes=VMEM_LIMIT)


def _rms(x, w):
    return x * lax.rsqrt(jnp.mean(x * x, axis=-1, keepdims=True) + EPS) * w


def _sigmoid(x):
    return 0.5 + 0.5 * jnp.tanh(0.5 * x)


def _gla_gate(xn, wlr, wlr2, blr):
    glr = jnp.dot(xn, wlr, preferred_element_type=F32)
    y = jnp.dot(glr.astype(BF16), wlr2, preferred_element_type=F32) + blr
    ls = -(jnp.maximum(-y, 0.0) + jnp.log1p(jnp.exp(-jnp.abs(y))))
    return ls * (1.0 / GLA_TAU)


def _tile_rows(seq, n_time):
    NS = SEQ_PER_ROWGROUP
    return pl.ds((seq // NS) * n_time * NS + seq % NS, n_time, stride=NS)


def _prenorm_kernel(*refs, n_aliased):
    x_ref, w_ref, wlr_ref, wlr2_ref, blr_ref = refs[:5]
    xr_ref, xn_ref, g_ref, rows_scr = refs[5 + n_aliased:]
    n_seq, n_time, D = x_ref.shape
    for seq in range(n_seq):
        for j in range(D // LANES):
            rows_scr[j, _tile_rows(seq, n_time), :] = x_ref[seq, :, j * LANES:(j + 1) * LANES]
    x = jnp.concatenate([rows_scr[j] for j in range(D // LANES)], axis=1)
    xr_ref[...] = x
    xn = _rms(x, w_ref[...]).astype(BF16)
    xn_ref[...] = xn
    g_ref[...] = _gla_gate(xn, wlr_ref[...], wlr2_ref[...], blr_ref[...])


def _prenorm_call(x3, w, w_lr, w_lr2, b_lr, *, n_seq, n_time, row0, total_rows, prev=None):
    B, L, D = x3.shape
    R, N = w_lr2.shape[1:]
    tm = n_seq * n_time
    assert B % n_seq == 0 and L % n_time == 0 and (B == n_seq or L == n_time) and row0 % tm == 0
    nb, nt = B // n_seq, L // n_time
    rb0 = row0 // tm
    row = lambda i: (rb0 + i, 0)
    const = lambda i: (0, 0)
    l0 = lambda i: (0, 0, 0)
    in_specs = [pl.BlockSpec((n_seq, n_time, D), lambda i: (i // nt, i % nt, 0)),
                pl.BlockSpec((1, D), const),
                pl.BlockSpec((None, D, R), l0),
                pl.BlockSpec((None, R, N), l0),
                pl.BlockSpec((1, N), const)]
    args = [x3, w.reshape(1, D), w_lr, w_lr2, b_lr.reshape(1, N)]
    aliases = {}
    if prev is not None:
        for k, p in enumerate(prev):
            in_specs.append(pl.BlockSpec(memory_space=pl.ANY))
            args.append(p)
            aliases[len(args) - 1] = k
    return pl.pallas_call(
        functools.partial(_prenorm_kernel, n_aliased=len(aliases)),
        grid=(nb * nt,),
        in_specs=in_specs,
        out_specs=[pl.BlockSpec((tm, D), row), pl.BlockSpec((tm, D), row), pl.BlockSpec((tm, N), row)],
        out_shape=[jax.ShapeDtypeStruct((total_rows, D), F32), jax.ShapeDtypeStruct((total_rows, D), BF16),
                   jax.ShapeDtypeStruct((total_rows, N), F32)],
        scratch_shapes=[pltpu.VMEM((D // LANES, tm, LANES), F32)],
        input_output_aliases=aliases,
        compiler_params=_cparams(("parallel",)),
        name="prenorm",
    )(*args)


def _gram_levels(C):
    out = []
    m = C // 2
    while m >= FINE_BLOCK:
        out.append(m)
        m //= 2
    return out


def _recurrence_unit_phases(u):
    C, nv = u.C, u.nv
    NS = SEQ_PER_ROWGROUP
    FINE = min(C, FINE_BLOCK)
    levels = _gram_levels(C)
    NL = len(levels)

    def rows(t):
        return pl.ds(t * NS, NS)

    def seq_rows(s):
        return pl.ds(s, C, stride=NS)

    def vector_phase(hh):
        acc = jnp.zeros((NS, LANES), F32)
        for t in range(C):
            if u.hgrn:
                z = u.p[u.slab(hh, 1), rows(t), :]
                e = jnp.exp(-jnp.abs(z))
                r = 1.0 / (1.0 + e)
                er = e * r
                pos = z >= 0.0
                sig_p = jnp.where(pos, r, er)
                sig_n = jnp.where(pos, er, r)
                oml = u.oml(hh)
                g_t = jnp.log(u.lbf(hh) + oml * sig_p)
                u.kk[hh, rows(t), :] = oml * sig_n
            else:
                g_t = u.g(hh, rows(t))
            acc = acc + g_t
            u.a[hh, t] = jnp.exp2(g_t * LOG2E)
            u.b[hh, t] = acc * LOG2E
        b_last = acc * LOG2E
        d_last = jnp.exp2(b_last)

        d_t = jnp.transpose(jnp.concatenate([d_last] * (LANES // NS), axis=0))
        for s in range(NS):
            u.dcol[hh, s] = jnp.broadcast_to(d_t[:, s:s + 1], (LANES, LANES))

        def q_rows(t):
            q_t = u.p[u.slab(hh, 0), rows(t), :]
            return q_t if u.q_scale == 1.0 else q_t * u.q_scale

        def k_rows(t):
            return u.kk[hh, rows(t), :] if u.hgrn else u.p[u.slab(hh, 1), rows(t), :]

        for t in range(C):
            q_t, k_t, b_t = q_rows(t), k_rows(t), u.b[hh, t]
            u.zs[hh, NL, rows(t), :] = q_t * jnp.exp2(b_t)
            u.zs[hh, NL + 1, rows(t), :] = k_t * jnp.exp2(b_last - b_t)
            for j, m in enumerate(levels):
                r_idx = (t // (2 * m)) * (2 * m) + m - 1
                if (t % (2 * m)) >= m:
                    u.zs[hh, j, rows(t), :] = q_t * jnp.exp2(b_t - u.b[hh, r_idx])
                else:
                    u.zs[hh, j, rows(t), :] = k_t * jnp.exp2(u.b[hh, r_idx] - b_t)

        for t0 in range(0, C, FINE):
            k_blk, v_blk, decay = {}, {}, {}
            for t in range(t0, t0 + FINE):
                q_t = q_rows(t)
                k_blk[t] = k_rows(t)
                v_blk[t] = [u.p[u.slab(hh, 2 + i), rows(t), :] for i in range(nv)]
                a_t = u.a[hh, t]
                for s in range(t0, t):
                    decay[s] = a_t if s == t - 1 else decay[s] * a_t
                w = jnp.sum(q_t * k_blk[t], axis=-1, keepdims=True)
                o_t = [w * v_blk[t][i] for i in range(nv)]
                for s in range(t0, t):
                    w = jnp.sum(q_t * (k_blk[s] * decay[s]), axis=-1, keepdims=True)
                    o_t = [o_t[i] + w * v_blk[s][i] for i in range(nv)]
                for i in range(nv):
                    u.oacc[hh, i, rows(t), :] = o_t[i]

    ti = lax.broadcasted_iota(jnp.int32, (C, C), 0)
    si = lax.broadcasted_iota(jnp.int32, (C, C), 1)
    xr = jnp.bitwise_xor(ti, si)
    lower = ti > si
    masks = [lower & (xr >= m) & (xr < 2 * m) for m in levels]

    def matmul_phase(hh):
        def load_v(s):
            vs = [u.p[u.slab(hh, 2 + i), seq_rows(s), :] for i in range(nv)]
            vb = vs[0] if nv == 1 else jnp.concatenate(vs, axis=1)
            return vb.astype(BF16)

        scores = []
        for s in range(NS if NL else 0):
            sc = jnp.zeros((C, C), F32)
            for j in range(NL):
                z = u.zs[hh, j, seq_rows(s), :].astype(BF16)
                gram = lax.dot_general(z, z, (((1,), (1,)), ((), ())), preferred_element_type=F32)
                sc = jnp.where(masks[j], gram, sc)
            scores.append(sc.astype(BF16))
        inter = []
        for s in range(NS):
            zq = u.zs[hh, NL, seq_rows(s), :].astype(BF16)
            zk = u.zs[hh, NL + 1, seq_rows(s), :].astype(BF16)
            S = u.S_in(s, hh)
            inter.append(jnp.dot(zq, S.astype(BF16), preferred_element_type=F32))
            upd = lax.dot_general(zk, load_v(s), (((0,), (0,)), ((), ())), preferred_element_type=F32)
            dc = u.dcol[hh, s]
            if nv > 1:
                dc = jnp.concatenate([dc] * nv, axis=1)
            u.S_out(s, hh, dc * S + upd)
        for s in range(NS):
            o = inter[s]
            if NL:
                o = o + jnp.dot(scores[s], load_v(s), preferred_element_type=F32)
            for i in range(nv):
                sl = slice(i * LANES, (i + 1) * LANES)
                u.oacc[hh, i, seq_rows(s), :] = u.oacc[hh, i, seq_rows(s), :] + o[:, sl]

    def gate_phase(hh):
        if nv == 1:
            o_all = u.oacc[hh, 0]
        else:
            o_all = jnp.concatenate([u.oacc[hh, i] for i in range(nv)], axis=1)
        y = o_all * lax.rsqrt(jnp.mean(o_all * o_all, axis=-1, keepdims=True) + EPS) * u.onw()
        gs = [u.p[u.slab(hh, 2 + nv + i)] for i in range(nv)]
        gt = gs[0] if nv == 1 else jnp.concatenate(gs, axis=1)
        u.out(hh, y * (gt * _sigmoid(gt)))

    return vector_phase, matmul_phase, gate_phase


def _recurrence_kernel(*refs, C, CS, dv, hgrn, q_scale, n_aliased, HB):
    nv = dv // LANES
    NS = SEQ_PER_ROWGROUP
    R = NS * C
    it = iter(refs)
    xcur_ref, xnext_ref, xs_cur_ref, xs_next_ref = next(it), next(it), next(it), next(it)
    w_refs = [next(it) for _ in range(4)]
    g_ref, gs_ref = (None, None) if hgrn else (next(it), next(it))
    onw_ref = next(it)
    lbf_ref, oml_ref, lbfs_ref, omls_ref = (next(it), next(it), next(it), next(it)) if hgrn else (None,) * 4
    s0_ref = next(it)
    for _ in range(n_aliased):
        next(it)
    o_ref, sout_ref, os_ref, ssout_ref = next(it), next(it), next(it), next(it)
    p_scr, pnext_scr, a_scr, b_scr, zs_scr = next(it), next(it), next(it), next(it), next(it)
    kk_scr = next(it) if hgrn else None
    oacc_scr, dcol_scr, S_scr = next(it), next(it), next(it)
    ps_scr, psnext_scr, as_scr, bs_scr, zss_scr = next(it), next(it), next(it), next(it), next(it)
    kks_scr = next(it) if hgrn else None
    oaccs_scr, dcols_scr = next(it), next(it)

    c_idx = pl.program_id(1)
    hs = lax.rem(c_idx, HB)

    @pl.when(c_idx == 0)
    def _():
        S_scr[...] = jnp.zeros_like(S_scr)

    def project(xp_ref, xs_ref):
        x = jnp.concatenate([xp_ref[...], xs_ref[...]], axis=0)
        base = 0
        for w_ref in w_refs:
            p = jnp.dot(x, w_ref[...], preferred_element_type=F32)
            n = w_ref.shape[1] // LANES
            for i in range(n):
                pnext_scr[base + i] = p[:R, i * LANES:(i + 1) * LANES]
                psnext_scr[base + i] = p[R:, i * LANES:(i + 1) * LANES]
            base += n

    @pl.when(c_idx == 0)
    def _():
        project(xcur_ref, xs_cur_ref)

    p_scr[...] = pnext_scr[...]
    ps_scr[...] = psnext_scr[...]

    def slab_of(head_slot, k):
        if k < 2:
            return k * HB + head_slot
        if k < 2 + nv:
            return 2 * HB + head_slot * nv + (k - 2)
        return (2 + nv) * HB + head_slot * nv + (k - 2 - nv)

    def prompt_out(hh, val):
        o_ref[:, hh * dv:(hh + 1) * dv] = val

    def prompt_S_out(s, hh, val):
        S_scr[s, hh] = val

    def sample_out(hh, val):
        os_ref[...] = val

    def sample_S_out(s, hh, val):
        ssout_ref[s] = val

    lane = lambda hh: slice(hh * LANES, (hh + 1) * LANES)
    prompt = SimpleNamespace(
        C=C, nv=nv, hgrn=hgrn, q_scale=q_scale, heads=range(HB), p=p_scr, slab=slab_of,
        g=lambda hh, r: g_ref[r, lane(hh)], lbf=lambda hh: lbf_ref[:, lane(hh)],
        oml=lambda hh: oml_ref[:, lane(hh)], onw=lambda: onw_ref[...],
        a=a_scr, b=b_scr, zs=zs_scr, kk=kk_scr, oacc=oacc_scr, dcol=dcol_scr,
        S_in=lambda s, hh: S_scr[s, hh], S_out=prompt_S_out, out=prompt_out)
    sample = SimpleNamespace(
        C=CS, nv=nv, hgrn=hgrn, q_scale=q_scale, heads=range(1), p=ps_scr,
        slab=lambda hh, k: slab_of(hs, k),
        g=lambda hh, r: gs_ref[r, :], lbf=lambda hh: lbfs_ref[...], oml=lambda hh: omls_ref[...],
        onw=lambda: onw_ref[...],
        a=as_scr, b=bs_scr, zs=zss_scr, kk=kks_scr, oacc=oaccs_scr, dcol=dcols_scr,
        S_in=lambda s, hh: s0_ref[s], S_out=sample_S_out, out=sample_out)

    units = [(un, _recurrence_unit_phases(un)) for un in (prompt, sample)]
    for un, (vector_phase, _, _) in units:
        for hh in un.heads:
            vector_phase(hh)
    project(xnext_ref, xs_next_ref)
    for un, (_, matmul_phase, _) in units:
        for hh in un.heads:
            matmul_phase(hh)
    for un, (_, _, gate_phase) in units:
        for hh in un.heads:
            gate_phase(hh)

    @pl.when(c_idx == pl.num_programs(1) - 1)
    def _():
        sout_ref[...] = S_scr[...]


def _recurrence_call(xn, w_c, w_col0, g_gla, s_in, onw, lbf, oml, *, layer, hgrn, C, CS, n_chunks, sample_row0,
                     n_groups, heads_per_step, prev_states=None, prev_sample_states=None):
    NS = SEQ_PER_ROWGROUP
    R, RS = NS * C, NS * CS
    H = HGRN_HEADS if hgrn else GLA_HEADS
    dv = HGRN_DV if hgrn else GLA_DV
    dk = LANES
    nv = dv // LANES
    T, D = xn.shape
    W = (2 + 2 * nv) * LANES
    HB = heads_per_step
    assert H % HB == 0 and n_chunks == n_groups * HB and sample_row0 % RS == 0
    rbs = sample_row0 // RS
    last = n_chunks - 1
    s_grp = lambda c: c // HB
    s_head = lambda h, c: h * HB + c % HB

    in_specs = [pl.BlockSpec((R, D), lambda h, c: (c, 0)),
                pl.BlockSpec((R, D), lambda h, c: (jnp.minimum(c + 1, last), 0)),
                pl.BlockSpec((RS, D), lambda h, c: (rbs, 0)),
                pl.BlockSpec((RS, D), lambda h, c: (rbs + s_grp(jnp.minimum(c + 1, last)), 0))]
    args = [xn, xn, xn, xn]
    col = w_col0
    for width in (LANES, LANES, dv, dv):
        blk = HB * width
        assert col % blk == 0
        in_specs.append(pl.BlockSpec((None, D, blk), lambda h, c, j=col // blk: (layer, 0, j + h)))
        args.append(w_c)
        col += H * width
    if not hgrn:
        in_specs += [pl.BlockSpec((R, HB * LANES), lambda h, c: (c, h)),
                     pl.BlockSpec((RS, LANES), lambda h, c: (rbs + s_grp(c), s_head(h, c)))]
        args += [g_gla, g_gla]
    in_specs.append(pl.BlockSpec((1, dv), lambda h, c: (0, 0)))
    args.append(onw.reshape(1, dv))
    if hgrn:
        in_specs += [pl.BlockSpec((1, HB * LANES), lambda h, c: (0, h)),
                     pl.BlockSpec((1, HB * LANES), lambda h, c: (0, h)),
                     pl.BlockSpec((1, LANES), lambda h, c: (0, s_head(h, c))),
                     pl.BlockSpec((1, LANES), lambda h, c: (0, s_head(h, c)))]
        args += [lbf.reshape(1, -1), oml.reshape(1, -1)] * 2
    ss_map = lambda h, c: (layer, s_grp(c), s_head(h, c), 0, 0)
    in_specs.append(pl.BlockSpec((None, NS, None, dk, dv), ss_map))
    args.append(s_in)
    aliases = {}
    for k, prev in ((1, prev_states), (3, prev_sample_states)):
        if prev is not None:
            in_specs.append(pl.BlockSpec(memory_space=pl.ANY))
            args.append(prev)
            aliases[len(args) - 1] = k

    out_specs = [pl.BlockSpec((R, HB * dv), lambda h, c: (c, h)),
                 pl.BlockSpec((None, NS, HB, dk, dv), lambda h, c: (layer, 0, h, 0, 0)),
                 pl.BlockSpec((RS, dv), lambda h, c: (s_grp(c), s_head(h, c))),
                 pl.BlockSpec((None, NS, None, dk, dv), ss_map)]
    out_shape = [jax.ShapeDtypeStruct((T, H * dv), F32),
                 jax.ShapeDtypeStruct((DEPTH, NS, H, dk, dv), F32),
                 jax.ShapeDtypeStruct((n_groups * RS, H * dv), F32),
                 jax.ShapeDtypeStruct((DEPTH, n_groups * NS, H, dk, dv), F32)]

    def unit_scratch(heads, Cu, prompt_state):
        Ru = NS * Cu
        sc = [pltpu.VMEM((HB * W // LANES, Ru, LANES), F32),
              pltpu.VMEM((HB * W // LANES, Ru, LANES), F32),
              pltpu.VMEM((heads, Cu, NS, LANES), F32),
              pltpu.VMEM((heads, Cu, NS, LANES), F32),
              pltpu.VMEM((heads, len(_gram_levels(Cu)) + 2, Ru, LANES), F32)]
        if hgrn:
            sc.append(pltpu.VMEM((heads, Ru, LANES), F32))
        sc += [pltpu.VMEM((heads, nv, Ru, LANES), F32),
               pltpu.VMEM((heads, NS, LANES, LANES), F32)]
        if prompt_state:
            sc.append(pltpu.VMEM((NS, heads, dk, dv), F32))
        return sc

    kern = functools.partial(_recurrence_kernel, C=C, CS=CS, dv=dv, hgrn=hgrn,
                             q_scale=1.0 if hgrn else GLA_DK ** -0.5, n_aliased=len(aliases), HB=HB)
    return pl.pallas_call(
        kern,
        grid=(H // HB, n_chunks),
        in_specs=in_specs,
        out_specs=out_specs,
        out_shape=out_shape,
        scratch_shapes=unit_scratch(HB, C, True) + unit_scratch(1, CS, False),
        input_output_aliases=aliases,
        compiler_params=_cparams(("parallel", "arbitrary")),
        name="hgrn" if hgrn else "gla",
    )(*args)


def _outproj_kernel(og_ref, oh_ref, xn_ref, x_ref, wga_ref, wgb_ref, wo_ref, npost_ref, npre_ref, h_ref, hn_ref):
    xn = xn_ref[...]
    ga = jnp.dot(xn, wga_ref[...], preferred_element_type=F32)
    gb = jnp.dot(xn, wgb_ref[...], preferred_element_type=F32)
    merged = og_ref[...] * _sigmoid(ga) + oh_ref[...] * _sigmoid(gb)
    m = jnp.dot(merged.astype(BF16), wo_ref[...], preferred_element_type=F32)
    h = x_ref[...] + _rms(m, npost_ref[...])
    h_ref[...] = h
    hn_ref[...] = _rms(h, npre_ref[...]).astype(BF16)


def _outproj_call(og, oh, xn, x, w_c, w_o, npost, npre, *, layer, tm):
    T, D = x.shape
    row = lambda i: (i, 0)
    const = lambda i: (0, 0)
    once = pl.Buffered(1)
    n_blk = w_c.shape[2] // D
    return pl.pallas_call(
        _outproj_kernel,
        grid=(T // tm,),
        in_specs=[pl.BlockSpec((tm, D), row),
                  pl.BlockSpec((tm, D), row),
                  pl.BlockSpec((tm, D), row),
                  pl.BlockSpec((tm, D), row),
                  pl.BlockSpec((None, D, D), lambda i: (layer, 0, n_blk - 2), pipeline_mode=once),
                  pl.BlockSpec((None, D, D), lambda i: (layer, 0, n_blk - 1), pipeline_mode=once),
                  pl.BlockSpec((None, D, D), lambda i: (layer, 0, 0), pipeline_mode=once),
                  pl.BlockSpec((1, D), const),
                  pl.BlockSpec((1, D), const)],
        out_specs=[pl.BlockSpec((tm, D), row), pl.BlockSpec((tm, D), row)],
        out_shape=[jax.ShapeDtypeStruct((T, D), F32), jax.ShapeDtypeStruct((T, D), BF16)],
        compiler_params=_cparams(("parallel",)),
        name="outproj",
    )(og, oh, xn, x, w_c, w_c, w_o, npost.reshape(1, D), npre.reshape(1, D))


def _ffn_kernel(*refs, splits, emit_next):
    hn_ref, h_ref, wa_ref, wu_ref, wf_ref, nfpost_ref = refs[:6]
    if emit_next:
        nnext_ref, wlr_ref, wlr2_ref, blr_ref, y_ref, yn_ref, g_ref = refs[6:]
    else:
        y_ref, rows_scr = refs[6:]
    hn = hn_ref[...]
    f = None
    for lo, hi in splits:
        sl = slice(lo, hi)
        a = jnp.dot(hn, wa_ref[:, sl], preferred_element_type=F32)
        u = jnp.dot(hn, wu_ref[:, sl], preferred_element_type=F32)
        g = (a * _sigmoid(a) * u).astype(BF16)
        part = jnp.dot(g, wf_ref[sl, :], preferred_element_type=F32)
        f = part if f is None else f + part
    y = h_ref[...] + _rms(f, nfpost_ref[...])
    if emit_next:
        y_ref[...] = y
        yn = _rms(y, nnext_ref[...]).astype(BF16)
        yn_ref[...] = yn
        g_ref[...] = _gla_gate(yn, wlr_ref[...], wlr2_ref[...], blr_ref[...])
    else:
        n_seq, n_time, D = y_ref.shape
        for j in range(D // LANES):
            rows_scr[j] = y[:, j * LANES:(j + 1) * LANES]
        for seq in range(n_seq):
            for j in range(D // LANES):
                y_ref[seq, :, j * LANES:(j + 1) * LANES] = rows_scr[j, _tile_rows(seq, n_time), :]


def _ffn_call(hn, h, w_ffn, w_f, nfpost, next_layer, *, layer, tm, final=None):
    T, D = h.shape
    F = w_f.shape[1]
    rb0, n_tiles = 0, T // tm
    if final is not None:
        row0, B, L, n_seq, n_time = final
        assert n_seq * n_time == tm and row0 % tm == 0 and (B == n_seq or L == n_time)
        rb0, n_tiles, nt = row0 // tm, B * L // tm, L // n_time
    row = lambda i: (rb0 + i, 0)
    const = lambda i: (0, 0)
    once = pl.Buffered(1)
    half = (F // MXU_COLS + 1) // 2 * MXU_COLS
    assert F % MXU_COLS == 0 and F % LANES == 0
    in_specs = [pl.BlockSpec((tm, D), row),
                pl.BlockSpec((tm, D), row),
                pl.BlockSpec((None, D, F), lambda i: (layer, 0, 0), pipeline_mode=once),
                pl.BlockSpec((None, D, F), lambda i: (layer, 0, 1), pipeline_mode=once),
                pl.BlockSpec((None, F, D), lambda i: (layer, 0, 0), pipeline_mode=once),
                pl.BlockSpec((1, D), const)]
    args = [hn, h, w_ffn, w_ffn, w_f, nfpost.reshape(1, D)]
    scratch = []
    if next_layer is None:
        out_specs = [pl.BlockSpec((n_seq, n_time, D), lambda i: (i // nt, i % nt, 0))]
        out_shape = [jax.ShapeDtypeStruct((B, L, D), F32)]
        scratch = [pltpu.VMEM((D // LANES, tm, LANES), F32)]
    else:
        out_specs = [pl.BlockSpec((tm, D), row)]
        out_shape = [jax.ShapeDtypeStruct((T, D), F32)]
        nnext, w_lr, w_lr2, b_lr = next_layer
        R, N = w_lr2.shape[1:]
        lnext = lambda i: (layer + 1, 0, 0)
        in_specs += [pl.BlockSpec((1, D), const),
                     pl.BlockSpec((None, D, R), lnext, pipeline_mode=once),
                     pl.BlockSpec((None, R, N), lnext, pipeline_mode=once),
                     pl.BlockSpec((1, N), const)]
        args += [nnext.reshape(1, D), w_lr, w_lr2, b_lr.reshape(1, N)]
        out_specs += [pl.BlockSpec((tm, D), row), pl.BlockSpec((tm, N), row)]
        out_shape += [jax.ShapeDtypeStruct((T, D), BF16), jax.ShapeDtypeStruct((T, N), F32)]
    return pl.pallas_call(
        functools.partial(_ffn_kernel, splits=((0, half), (half, F)), emit_next=next_layer is not None),
        grid=(n_tiles,),
        in_specs=in_specs,
        out_specs=out_specs,
        out_shape=out_shape,
        scratch_shapes=scratch,
        compiler_params=_cparams(("parallel",)),
        name="ffn",
    )(*args)


def _wconv_kernel(w_ref, g_ref, h_ref, lr_ref, *, c0, c1):
    w = w_ref[...]
    g_ref[...] = w[:, :c0].astype(BF16)
    h_ref[...] = w[:, c1:].astype(BF16)
    pad = jnp.zeros((w.shape[0], LANES - (c1 - c0)), w.dtype)
    lr_ref[...] = jnp.concatenate([w[:, c0:c1], pad], axis=1).astype(BF16)


def _wconv_call(w_in, c0, c1, rows):
    Dp, D, N = w_in.shape
    spec = lambda n: pl.BlockSpec((None, rows, n), lambda l, i: (l, i, 0))
    return pl.pallas_call(
        functools.partial(_wconv_kernel, c0=c0, c1=c1),
        grid=(Dp, D // rows),
        in_specs=[spec(N)],
        out_specs=[spec(c0), spec(N - c1), spec(LANES)],
        out_shape=[jax.ShapeDtypeStruct((Dp, D, c0), BF16), jax.ShapeDtypeStruct((Dp, D, N - c1), BF16),
                   jax.ShapeDtypeStruct((Dp, D, LANES), BF16)],
        compiler_params=_cparams(("parallel", "parallel")),
        name="wconv",
    )(w_in)


def kernel(x_prompt, x_sample, state_gla, state_hgrn, norm_mix_pre, norm_mix_post, norm_ffn_pre, norm_ffn_post, w_in, gla_w_lr2, gla_b_lr, gla_onorm, hgrn_lb, hgrn_onorm, w_out, w_ffn_in, w_ffn_out):
    B, L, D = x_prompt.shape
    BS, LS, _ = x_sample.shape
    NS = SEQ_PER_ROWGROUP
    assert B == NS and BS % NS == 0 and L % PROMPT_CHUNK == 0
    TP = B * L
    TS = BS * LS
    T = TP + TS
    assert TP % POST_ROWS == 0 and TS % POST_ROWS == 0

    sm = jax.nn.softmax(hgrn_lb.astype(F32), axis=0)
    lb = jnp.clip(jnp.cumsum(sm, axis=0) - sm[0:1], 0.0, 1.0 - 1e-6)
    lbf = jnp.maximum(lb, LB_FLOOR)
    oml = 1.0 - lb

    c_lr0 = GLA_HEADS * (2 * GLA_DK + 2 * GLA_DV)
    c_lr1 = c_lr0 + GLA_GATE_RANK
    w_g, w_h, w_lr = _wconv_call(w_in, c_lr0, c_lr1, rows=256)
    assert w_h.shape[2] == HGRN_HEADS * (2 * HGRN_EXPAND + 2 * HGRN_DV) + 2 * D
    w_lr2 = jnp.pad(gla_w_lr2, ((0, 0), (0, LANES - GLA_GATE_RANK), (0, 0))).astype(BF16)
    w_o = w_out.astype(BF16)
    w_ffn = w_ffn_in.astype(BF16)
    w_f = w_ffn_out.astype(BF16)

    P_TIME = POST_ROWS // NS
    S_SEQ = POST_ROWS // LS
    pre = _prenorm_call(x_prompt, norm_mix_pre[0], w_lr, w_lr2, gla_b_lr[0],
                        n_seq=NS, n_time=P_TIME, row0=0, total_rows=T)
    x, xn, g_gla = _prenorm_call(x_sample, norm_mix_pre[0], w_lr, w_lr2, gla_b_lr[0],
                                 n_seq=S_SEQ, n_time=LS, row0=TP, total_rows=T, prev=pre)

    n_sample_groups = BS // NS
    gla_p = gla_s = hg_p = hg_s = None
    for l in range(DEPTH):
        og, gla_p, og_s, gla_s = _recurrence_call(
            xn, w_g, 0, g_gla, state_gla, gla_onorm[l], None, None, layer=l, hgrn=False,
            C=PROMPT_CHUNK, CS=LS, n_chunks=L // PROMPT_CHUNK, sample_row0=TP, n_groups=n_sample_groups,
            heads_per_step=HEADS_PER_STEP, prev_states=gla_p, prev_sample_states=gla_s)
        oh, hg_p, oh_s, hg_s = _recurrence_call(
            xn, w_h, 0, None, state_hgrn, hgrn_onorm[l], lbf[l], oml[l], layer=l, hgrn=True,
            C=PROMPT_CHUNK, CS=LS, n_chunks=L // PROMPT_CHUNK, sample_row0=TP, n_groups=n_sample_groups,
            heads_per_step=HEADS_PER_STEP, prev_states=hg_p, prev_sample_states=hg_s)
        og = lax.dynamic_update_slice(og, og_s, (TP, 0))
        oh = lax.dynamic_update_slice(oh, oh_s, (TP, 0))

        h, hn = _outproj_call(og, oh, xn, x, w_h, w_o, norm_mix_post[l], norm_ffn_pre[l], layer=l, tm=POST_ROWS)
        if l + 1 < DEPTH:
            nxt = (norm_mix_pre[l + 1], w_lr, w_lr2, gla_b_lr[l + 1])
            x, xn, g_gla = _ffn_call(hn, h, w_ffn, w_f, norm_ffn_post[l], nxt, layer=l, tm=POST_ROWS)
        else:
            (y_p,) = _ffn_call(hn, h, w_ffn, w_f, norm_ffn_post[l], None, layer=l, tm=POST_ROWS,
                               final=(0, B, L, NS, P_TIME))
            (y_s,) = _ffn_call(hn, h, w_ffn, w_f, norm_ffn_post[l], None, layer=l, tm=POST_ROWS,
                               final=(TP, BS, LS, S_SEQ, LS))

    return (y_p, y_s, gla_p, gla_s, hg_p, hg_s)
```

```python
import functools
from types import SimpleNamespace

import jax
import jax.numpy as jnp
from jax import lax
from jax.experimental import pallas as pl
from jax.experimental.pallas import tpu as pltpu

F32 = jnp.float32
BF16 = jnp.bfloat16

D_MODEL = 1024
DEPTH = 4
GLA_HEADS = 4
GLA_DK = 128
GLA_DV = 256
GLA_GATE_RANK = 16
GLA_TAU = 16.0
HGRN_HEADS = 8
HGRN_EXPAND = 128
HGRN_DV = 128
LB_FLOOR = 1e-20
D_FF = 2816
EPS = 1e-6
LOG2E = 1.4426950408889634

SUBLANES = 8
LANES = 128
MXU_COLS = 256
SEQ_PER_ROWGROUP = SUBLANES

PROMPT_CHUNK = 64
HEADS_PER_STEP = 2
FINE_BLOCK = SUBLANES
POST_ROWS = 512
VMEM_LIMIT = 52 * 1024 * 1024


def _cparams(sem):
    return pltpu.CompilerParams(dimension_semantics=sem, vmem_limit_bytes=VMEM_LIMIT)


def _rms(x, w):
    return x * lax.rsqrt(jnp.mean(x * x, axis=-1, keepdims=True) + EPS) * w


def _sigmoid(x):
    return 0.5 + 0.5 * jnp.tanh(0.5 * x)


def _dot_t(x, wt):
    return lax.dot_general(x, wt, (((1,), (1,)), ((), ())), preferred_element_type=F32)


def _gla_gate(xn, wlr_t, wlr2, blr):
    glr = _dot_t(xn, wlr_t)
    y = jnp.dot(glr.astype(BF16), wlr2, preferred_element_type=F32) + blr
    ls = -(jnp.maximum(-y, 0.0) + jnp.log1p(jnp.exp(-jnp.abs(y))))
    return ls * (1.0 / GLA_TAU)


def _tile_rows(seq, n_time):
    NS = SEQ_PER_ROWGROUP
    return pl.ds((seq // NS) * n_time * NS + seq % NS, n_time, stride=NS)


def _prenorm_kernel(*refs, n_aliased):
    x_ref, w_ref, wlr_ref, wlr2_ref, blr_ref = refs[:5]
    xr_ref, xn_ref, g_ref, rows_scr = refs[5 + n_aliased:]
    n_seq, n_time, D = x_ref.shape
    for seq in range(n_seq):
        for j in range(D // LANES):
            rows_scr[j, _tile_rows(seq, n_time), :] = x_ref[seq, :, j * LANES:(j + 1) * LANES]
    x = jnp.concatenate([rows_scr[j] for j in range(D // LANES)], axis=1)
    xr_ref[...] = x
    xn = _rms(x, w_ref[...]).astype(BF16)
    xn_ref[...] = xn
    g_ref[...] = _gla_gate(xn, wlr_ref[...], wlr2_ref[...], blr_ref[...])


def _prenorm_call(x3, w, w_lr, w_lr2, b_lr, *, n_seq, n_time, row0, total_rows, prev=None):
    B, L, D = x3.shape
    R, N = w_lr2.shape[1:]
    assert w_lr.shape[1:] == (R, D)
    tm = n_seq * n_time
    assert B % n_seq == 0 and L % n_time == 0 and (B == n_seq or L == n_time) and row0 % tm == 0
    nb, nt = B // n_seq, L // n_time
    rb0 = row0 // tm
    row = lambda i: (rb0 + i, 0)
    const = lambda i: (0, 0)
    l0 = lambda i: (0, 0, 0)
    in_specs = [pl.BlockSpec((n_seq, n_time, D), lambda i: (i // nt, i % nt, 0)),
                pl.BlockSpec((1, D), const),
                pl.BlockSpec((None, R, D), l0),
                pl.BlockSpec((None, R, N), l0),
                pl.BlockSpec((1, N), const)]
    args = [x3, w.reshape(1, D), w_lr, w_lr2, b_lr.reshape(1, N)]
    aliases = {}
    if prev is not None:
        for k, p in enumerate(prev):
            in_specs.append(pl.BlockSpec(memory_space=pl.ANY))
            args.append(p)
            aliases[len(args) - 1] = k
    return pl.pallas_call(
        functools.partial(_prenorm_kernel, n_aliased=len(aliases)),
        grid=(nb * nt,),
        in_specs=in_specs,
        out_specs=[pl.BlockSpec((tm, D), row), pl.BlockSpec((tm, D), row), pl.BlockSpec((tm, N), row)],
        out_shape=[jax.ShapeDtypeStruct((total_rows, D), F32), jax.ShapeDtypeStruct((total_rows, D), BF16),
                   jax.ShapeDtypeStruct((total_rows, N), F32)],
        scratch_shapes=[pltpu.VMEM((D // LANES, tm, LANES), F32)],
        input_output_aliases=aliases,
        compiler_params=_cparams(("parallel",)),
        name="prenorm",
    )(*args)


def _gram_levels(C):
    out = []
    m = C // 2
    while m >= FINE_BLOCK:
        out.append(m)
        m //= 2
    return out


def _recurrence_unit_phases(u):
    C, nv = u.C, u.nv
    NS = SEQ_PER_ROWGROUP
    FINE = min(C, FINE_BLOCK)
    levels = _gram_levels(C)
    NL = len(levels)

    def rows(t):
        return pl.ds(t * NS, NS)

    def seq_rows(s):
        return pl.ds(s, C, stride=NS)

    def vector_phase(hh):
        acc = jnp.zeros((NS, LANES), F32)
        for t in range(C):
            if u.hgrn:
                z = u.p[u.slab(hh, 1), rows(t), :]
                e = jnp.exp(-jnp.abs(z))
                r = 1.0 / (1.0 + e)
                er = e * r
                pos = z >= 0.0
                sig_p = jnp.where(pos, r, er)
                sig_n = jnp.where(pos, er, r)
                oml = u.oml(hh)
                g_t = jnp.log(u.lbf(hh) + oml * sig_p)
                u.kk[hh, rows(t), :] = oml * sig_n
            else:
                g_t = u.g(hh, rows(t))
            acc = acc + g_t
            u.a[hh, t] = jnp.exp2(g_t * LOG2E)
            u.b[hh, t] = acc * LOG2E
        b_last = acc * LOG2E
        d_last = jnp.exp2(b_last)

        d_t = jnp.transpose(jnp.concatenate([d_last] * (LANES // NS), axis=0))
        for s in range(NS):
            u.dcol[hh, s] = jnp.broadcast_to(d_t[:, s:s + 1], (LANES, LANES))

        def q_rows(t):
            q_t = u.p[u.slab(hh, 0), rows(t), :]
            return q_t if u.q_scale == 1.0 else q_t * u.q_scale

        def k_rows(t):
            return u.kk[hh, rows(t), :] if u.hgrn else u.p[u.slab(hh, 1), rows(t), :]

        for t in range(C):
            q_t, k_t, b_t = q_rows(t), k_rows(t), u.b[hh, t]
            u.zs[hh, NL, rows(t), :] = q_t * jnp.exp2(b_t)
            u.zs[hh, NL + 1, rows(t), :] = k_t * jnp.exp2(b_last - b_t)
            for j, m in enumerate(levels):
                r_idx = (t // (2 * m)) * (2 * m) + m - 1
                if (t % (2 * m)) >= m:
                    u.zs[hh, j, rows(t), :] = q_t * jnp.exp2(b_t - u.b[hh, r_idx])
                else:
                    u.zs[hh, j, rows(t), :] = k_t * jnp.exp2(u.b[hh, r_idx] - b_t)

        for t0 in range(0, C, FINE):
            k_blk, v_blk, decay = {}, {}, {}
            for t in range(t0, t0 + FINE):
                q_t = q_rows(t)
                k_blk[t] = k_rows(t)
                v_blk[t] = [u.p[u.slab(hh, 2 + i), rows(t), :] for i in range(nv)]
                a_t = u.a[hh, t]
                for s in range(t0, t):
                    decay[s] = a_t if s == t - 1 else decay[s] * a_t
                w = jnp.sum(q_t * k_blk[t], axis=-1, keepdims=True)
                o_t = [w * v_blk[t][i] for i in range(nv)]
                for s in range(t0, t):
                    w = jnp.sum(q_t * (k_blk[s] * decay[s]), axis=-1, keepdims=True)
                    o_t = [o_t[i] + w * v_blk[s][i] for i in range(nv)]
                for i in range(nv):
                    u.oacc[hh, i, rows(t), :] = o_t[i]

    ti = lax.broadcasted_iota(jnp.int32, (C, C), 0)
    si = lax.broadcasted_iota(jnp.int32, (C, C), 1)
    xr = jnp.bitwise_xor(ti, si)
    lower = ti > si
    masks = [lower & (xr >= m) & (xr < 2 * m) for m in levels]

    def matmul_phase(hh):
        def load_v(s):
            vs = [u.p[u.slab(hh, 2 + i), seq_rows(s), :] for i in range(nv)]
            vb = vs[0] if nv == 1 else jnp.concatenate(vs, axis=1)
            return vb.astype(BF16)

        scores = []
        for s in range(NS if NL else 0):
            sc = jnp.zeros((C, C), F32)
            for j in range(NL):
                z = u.zs[hh, j, seq_rows(s), :].astype(BF16)
                gram = lax.dot_general(z, z, (((1,), (1,)), ((), ())), preferred_element_type=F32)
                sc = jnp.where(masks[j], gram, sc)
            scores.append(sc.astype(BF16))
        inter = []
        for s in range(NS):
            zq = u.zs[hh, NL, seq_rows(s), :].astype(BF16)
            zk = u.zs[hh, NL + 1, seq_rows(s), :].astype(BF16)
            S = u.S_in(s, hh)
            inter.append(jnp.dot(zq, S.astype(BF16), preferred_element_type=F32))
            upd = lax.dot_general(zk, load_v(s), (((0,), (0,)), ((), ())), preferred_element_type=F32)
            dc = u.dcol[hh, s]
            if nv > 1:
                dc = jnp.concatenate([dc] * nv, axis=1)
            u.S_out(s, hh, dc * S + upd)
        for s in range(NS):
            o = inter[s]
            if NL:
                o = o + jnp.dot(scores[s], load_v(s), preferred_element_type=F32)
            for i in range(nv):
                sl = slice(i * LANES, (i + 1) * LANES)
                u.oacc[hh, i, seq_rows(s), :] = u.oacc[hh, i, seq_rows(s), :] + o[:, sl]

    def gate_phase(hh):
        if nv == 1:
            o_all = u.oacc[hh, 0]
        else:
            o_all = jnp.concatenate([u.oacc[hh, i] for i in range(nv)], axis=1)
        y = o_all * lax.rsqrt(jnp.mean(o_all * o_all, axis=-1, keepdims=True) + EPS) * u.onw()
        gs = [u.p[u.slab(hh, 2 + nv + i)] for i in range(nv)]
        gt = gs[0] if nv == 1 else jnp.concatenate(gs, axis=1)
        u.out(hh, y * (gt * _sigmoid(gt)))

    return vector_phase, matmul_phase, gate_phase


def _recurrence_kernel(*refs, C, CS, dv, hgrn, q_scale, n_aliased, HB):
    nv = dv // LANES
    NS = SEQ_PER_ROWGROUP
    R = NS * C
    it = iter(refs)
    xcur_ref, xnext_ref, xs_cur_ref, xs_next_ref = next(it), next(it), next(it), next(it)
    w_refs = [next(it) for _ in range(4)]
    g_ref, gs_ref = (None, None) if hgrn else (next(it), next(it))
    onw_ref = next(it)
    lbf_ref, oml_ref, lbfs_ref, omls_ref = (next(it), next(it), next(it), next(it)) if hgrn else (None,) * 4
    s0_ref = next(it)
    for _ in range(n_aliased):
        next(it)
    o_ref, sout_ref, os_ref, ssout_ref = next(it), next(it), next(it), next(it)
    p_scr, pnext_scr, a_scr, b_scr, zs_scr = next(it), next(it), next(it), next(it), next(it)
    kk_scr = next(it) if hgrn else None
    oacc_scr, dcol_scr, S_scr = next(it), next(it), next(it)
    ps_scr, psnext_scr, as_scr, bs_scr, zss_scr = next(it), next(it), next(it), next(it), next(it)
    kks_scr = next(it) if hgrn else None
    oaccs_scr, dcols_scr = next(it), next(it)

    c_idx = pl.program_id(1)
    hs = lax.rem(c_idx, HB)

    @pl.when(c_idx == 0)
    def _():
        S_scr[...] = jnp.zeros_like(S_scr)

    def project(xp_ref, xs_ref):
        x = jnp.concatenate([xp_ref[...], xs_ref[...]], axis=0)
        base = 0
        for w_ref in w_refs:
            p = _dot_t(x, w_ref[...])
            n = w_ref.shape[0] // LANES
            for i in range(n):
                pnext_scr[base + i] = p[:R, i * LANES:(i + 1) * LANES]
                psnext_scr[base + i] = p[R:, i * LANES:(i + 1) * LANES]
            base += n

    @pl.when(c_idx == 0)
    def _():
        project(xcur_ref, xs_cur_ref)

    p_scr[...] = pnext_scr[...]
    ps_scr[...] = psnext_scr[...]

    def slab_of(head_slot, k):
        if k < 2:
            return k * HB + head_slot
        if k < 2 + nv:
            return 2 * HB + head_slot * nv + (k - 2)
        return (2 + nv) * HB + head_slot * nv + (k - 2 - nv)

    def prompt_out(hh, val):
        o_ref[:, hh * dv:(hh + 1) * dv] = val

    def prompt_S_out(s, hh, val):
        S_scr[s, hh] = val

    def sample_out(hh, val):
        os_ref[...] = val

    def sample_S_out(s, hh, val):
        ssout_ref[s] = val

    lane = lambda hh: slice(hh * LANES, (hh + 1) * LANES)
    prompt = SimpleNamespace(
        C=C, nv=nv, hgrn=hgrn, q_scale=q_scale, heads=range(HB), p=p_scr, slab=slab_of,
        g=lambda hh, r: g_ref[r, lane(hh)], lbf=lambda hh: lbf_ref[:, lane(hh)],
        oml=lambda hh: oml_ref[:, lane(hh)], onw=lambda: onw_ref[...],
        a=a_scr, b=b_scr, zs=zs_scr, kk=kk_scr, oacc=oacc_scr, dcol=dcol_scr,
        S_in=lambda s, hh: S_scr[s, hh], S_out=prompt_S_out, out=prompt_out)
    sample = SimpleNamespace(
        C=CS, nv=nv, hgrn=hgrn, q_scale=q_scale, heads=range(1), p=ps_scr,
        slab=lambda hh, k: slab_of(hs, k),
        g=lambda hh, r: gs_ref[r, :], lbf=lambda hh: lbfs_ref[...], oml=lambda hh: omls_ref[...],
        onw=lambda: onw_ref[...],
        a=as_scr, b=bs_scr, zs=zss_scr, kk=kks_scr, oacc=oaccs_scr, dcol=dcols_scr,
        S_in=lambda s, hh: s0_ref[s], S_out=sample_S_out, out=sample_out)

    units = [(un, _recurrence_unit_phases(un)) for un in (prompt, sample)]
    for un, (vector_phase, _, _) in units:
        for hh in un.heads:
            vector_phase(hh)
    project(xnext_ref, xs_next_ref)
    for un, (_, matmul_phase, _) in units:
        for hh in un.heads:
            matmul_phase(hh)
    for un, (_, _, gate_phase) in units:
        for hh in un.heads:
            gate_phase(hh)

    @pl.when(c_idx == pl.num_programs(1) - 1)
    def _():
        sout_ref[...] = S_scr[...]


def _recurrence_call(xn, w_c, w_col0, g_gla, s_in, onw, lbf, oml, *, layer, hgrn, C, CS, n_chunks, sample_row0,
                     n_groups, heads_per_step, prev_states=None, prev_sample_states=None):
    NS = SEQ_PER_ROWGROUP
    R, RS = NS * C, NS * CS
    H = HGRN_HEADS if hgrn else GLA_HEADS
    dv = HGRN_DV if hgrn else GLA_DV
    dk = LANES
    nv = dv // LANES
    T, D = xn.shape
    W = (2 + 2 * nv) * LANES
    HB = heads_per_step
    assert H % HB == 0 and n_chunks == n_groups * HB and sample_row0 % RS == 0
    rbs = sample_row0 // RS
    last = n_chunks - 1
    s_grp = lambda c: c // HB
    s_head = lambda h, c: h * HB + c % HB

    in_specs = [pl.BlockSpec((R, D), lambda h, c: (c, 0)),
                pl.BlockSpec((R, D), lambda h, c: (jnp.minimum(c + 1, last), 0)),
                pl.BlockSpec((RS, D), lambda h, c: (rbs, 0)),
                pl.BlockSpec((RS, D), lambda h, c: (rbs + s_grp(jnp.minimum(c + 1, last)), 0))]
    args = [xn, xn, xn, xn]
    col = w_col0
    for width in (LANES, LANES, dv, dv):
        blk = HB * width
        assert col % blk == 0
        in_specs.append(pl.BlockSpec((None, blk, D), lambda h, c, j=col // blk: (layer, j + h, 0)))
        args.append(w_c)
        col += H * width
    if not hgrn:
        in_specs += [pl.BlockSpec((R, HB * LANES), lambda h, c: (c, h)),
                     pl.BlockSpec((RS, LANES), lambda h, c: (rbs + s_grp(c), s_head(h, c)))]
        args += [g_gla, g_gla]
    in_specs.append(pl.BlockSpec((1, dv), lambda h, c: (0, 0)))
    args.append(onw.reshape(1, dv))
    if hgrn:
        in_specs += [pl.BlockSpec((1, HB * LANES), lambda h, c: (0, h)),
                     pl.BlockSpec((1, HB * LANES), lambda h, c: (0, h)),
                     pl.BlockSpec((1, LANES), lambda h, c: (0, s_head(h, c))),
                     pl.BlockSpec((1, LANES), lambda h, c: (0, s_head(h, c)))]
        args += [lbf.reshape(1, -1), oml.reshape(1, -1)] * 2
    ss_map = lambda h, c: (layer, s_grp(c), s_head(h, c), 0, 0)
    in_specs.append(pl.BlockSpec((None, NS, None, dk, dv), ss_map))
    args.append(s_in)
    aliases = {}
    for k, prev in ((1, prev_states), (3, prev_sample_states)):
        if prev is not None:
            in_specs.append(pl.BlockSpec(memory_space=pl.ANY))
            args.append(prev)
            aliases[len(args) - 1] = k

    out_specs = [pl.BlockSpec((R, HB * dv), lambda h, c: (c, h)),
                 pl.BlockSpec((None, NS, HB, dk, dv), lambda h, c: (layer, 0, h, 0, 0)),
                 pl.BlockSpec((RS, dv), lambda h, c: (s_grp(c), s_head(h, c))),
                 pl.BlockSpec((None, NS, None, dk, dv), ss_map)]
    out_shape = [jax.ShapeDtypeStruct((T, H * dv), F32),
                 jax.ShapeDtypeStruct((DEPTH, NS, H, dk, dv), F32),
                 jax.ShapeDtypeStruct((n_groups * RS, H * dv), F32),
                 jax.ShapeDtypeStruct((DEPTH, n_groups * NS, H, dk, dv), F32)]

    def unit_scratch(heads, Cu, prompt_state):
        Ru = NS * Cu
        sc = [pltpu.VMEM((HB * W // LANES, Ru, LANES), F32),
              pltpu.VMEM((HB * W // LANES, Ru, LANES), F32),
              pltpu.VMEM((heads, Cu, NS, LANES), F32),
              pltpu.VMEM((heads, Cu, NS, LANES), F32),
              pltpu.VMEM((heads, len(_gram_levels(Cu)) + 2, Ru, LANES), F32)]
        if hgrn:
            sc.append(pltpu.VMEM((heads, Ru, LANES), F32))
        sc += [pltpu.VMEM((heads, nv, Ru, LANES), F32),
               pltpu.VMEM((heads, NS, LANES, LANES), F32)]
        if prompt_state:
            sc.append(pltpu.VMEM((NS, heads, dk, dv), F32))
        return sc

    kern = functools.partial(_recurrence_kernel, C=C, CS=CS, dv=dv, hgrn=hgrn,
                             q_scale=1.0 if hgrn else GLA_DK ** -0.5, n_aliased=len(aliases), HB=HB)
    return pl.pallas_call(
        kern,
        grid=(H // HB, n_chunks),
        in_specs=in_specs,
        out_specs=out_specs,
        out_shape=out_shape,
        scratch_shapes=unit_scratch(HB, C, True) + unit_scratch(1, CS, False),
        input_output_aliases=aliases,
        compiler_params=_cparams(("parallel", "arbitrary")),
        name="hgrn" if hgrn else "gla",
    )(*args)


def _outproj_kernel(og_ref, oh_ref, xn_ref, x_ref, wga_ref, wgb_ref, wo_ref, npost_ref, npre_ref, h_ref, hn_ref):
    xn = xn_ref[...]
    ga = _dot_t(xn, wga_ref[...])
    gb = _dot_t(xn, wgb_ref[...])
    merged = og_ref[...] * _sigmoid(ga) + oh_ref[...] * _sigmoid(gb)
    m = jnp.dot(merged.astype(BF16), wo_ref[...], preferred_element_type=F32)
    h = x_ref[...] + _rms(m, npost_ref[...])
    h_ref[...] = h
    hn_ref[...] = _rms(h, npre_ref[...]).astype(BF16)


def _outproj_call(og, oh, xn, x, w_c, w_o, npost, npre, *, layer, tm):
    T, D = x.shape
    row = lambda i: (i, 0)
    const = lambda i: (0, 0)
    once = pl.Buffered(1)
    n_blk = w_c.shape[1] // D
    return pl.pallas_call(
        _outproj_kernel,
        grid=(T // tm,),
        in_specs=[pl.BlockSpec((tm, D), row),
                  pl.BlockSpec((tm, D), row),
                  pl.BlockSpec((tm, D), row),
                  pl.BlockSpec((tm, D), row),
                  pl.BlockSpec((None, D, D), lambda i: (layer, n_blk - 2, 0), pipeline_mode=once),
                  pl.BlockSpec((None, D, D), lambda i: (layer, n_blk - 1, 0), pipeline_mode=once),
                  pl.BlockSpec((None, D, D), lambda i: (layer, 0, 0), pipeline_mode=once),
                  pl.BlockSpec((1, D), const),
                  pl.BlockSpec((1, D), const)],
        out_specs=[pl.BlockSpec((tm, D), row), pl.BlockSpec((tm, D), row)],
        out_shape=[jax.ShapeDtypeStruct((T, D), F32), jax.ShapeDtypeStruct((T, D), BF16)],
        compiler_params=_cparams(("parallel",)),
        name="outproj",
    )(og, oh, xn, x, w_c, w_c, w_o, npost.reshape(1, D), npre.reshape(1, D))


def _ffn_kernel(*refs, splits, emit_next):
    hn_ref, h_ref, wa_ref, wu_ref, wf_ref, nfpost_ref = refs[:6]
    if emit_next:
        nnext_ref, wlr_ref, wlr2_ref, blr_ref, y_ref, yn_ref, g_ref = refs[6:]
    else:
        y_ref, rows_scr = refs[6:]
    hn = hn_ref[...]
    f = None
    for lo, hi in splits:
        sl = slice(lo, hi)
        a = jnp.dot(hn, wa_ref[:, sl], preferred_element_type=F32)
        u = jnp.dot(hn, wu_ref[:, sl], preferred_element_type=F32)
        g = (a * _sigmoid(a) * u).astype(BF16)
        part = jnp.dot(g, wf_ref[sl, :], preferred_element_type=F32)
        f = part if f is None else f + part
    y = h_ref[...] + _rms(f, nfpost_ref[...])
    if emit_next:
        y_ref[...] = y
        yn = _rms(y, nnext_ref[...]).astype(BF16)
        yn_ref[...] = yn
        g_ref[...] = _gla_gate(yn, wlr_ref[...], wlr2_ref[...], blr_ref[...])
    else:
        n_seq, n_time, D = y_ref.shape
        for j in range(D // LANES):
            rows_scr[j] = y[:, j * LANES:(j + 1) * LANES]
        for seq in range(n_seq):
            for j in range(D // LANES):
                y_ref[seq, :, j * LANES:(j + 1) * LANES] = rows_scr[j, _tile_rows(seq, n_time), :]


def _ffn_call(hn, h, w_ffn, w_f, nfpost, next_layer, *, layer, tm, final=None):
    T, D = h.shape
    F = w_f.shape[1]
    rb0, n_tiles = 0, T // tm
    if final is not None:
        row0, B, L, n_seq, n_time = final
        assert n_seq * n_time == tm and row0 % tm == 0 and (B == n_seq or L == n_time)
        rb0, n_tiles, nt = row0 // tm, B * L // tm, L // n_time
    row = lambda i: (rb0 + i, 0)
    const = lambda i: (0, 0)
    once = pl.Buffered(1)
    half = (F // MXU_COLS + 1) // 2 * MXU_COLS
    assert F % MXU_COLS == 0 and F % LANES == 0
    in_specs = [pl.BlockSpec((tm, D), row),
                pl.BlockSpec((tm, D), row),
                pl.BlockSpec((None, D, F), lambda i: (layer, 0, 0), pipeline_mode=once),
                pl.BlockSpec((None, D, F), lambda i: (layer, 0, 1), pipeline_mode=once),
                pl.BlockSpec((None, F, D), lambda i: (layer, 0, 0), pipeline_mode=once),
                pl.BlockSpec((1, D), const)]
    args = [hn, h, w_ffn, w_ffn, w_f, nfpost.reshape(1, D)]
    scratch = []
    if next_layer is None:
        out_specs = [pl.BlockSpec((n_seq, n_time, D), lambda i: (i // nt, i % nt, 0))]
        out_shape = [jax.ShapeDtypeStruct((B, L, D), F32)]
        scratch = [pltpu.VMEM((D // LANES, tm, LANES), F32)]
    else:
        out_specs = [pl.BlockSpec((tm, D), row)]
        out_shape = [jax.ShapeDtypeStruct((T, D), F32)]
        nnext, w_lr, w_lr2, b_lr = next_layer
        R, N = w_lr2.shape[1:]
        lnext = lambda i: (layer + 1, 0, 0)
        in_specs += [pl.BlockSpec((1, D), const),
                     pl.BlockSpec((None, R, D), lnext, pipeline_mode=once),
                     pl.BlockSpec((None, R, N), lnext, pipeline_mode=once),
                     pl.BlockSpec((1, N), const)]
        args += [nnext.reshape(1, D), w_lr, w_lr2, b_lr.reshape(1, N)]
        out_specs += [pl.BlockSpec((tm, D), row), pl.BlockSpec((tm, N), row)]
        out_shape += [jax.ShapeDtypeStruct((T, D), BF16), jax.ShapeDtypeStruct((T, N), F32)]
    return pl.pallas_call(
        functools.partial(_ffn_kernel, splits=((0, half), (half, F)), emit_next=next_layer is not None),
        grid=(n_tiles,),
        in_specs=in_specs,
        out_specs=out_specs,
        out_shape=out_shape,
        scratch_shapes=scratch,
        compiler_params=_cparams(("parallel",)),
        name="ffn",
    )(*args)


def kernel(x_prompt, x_sample, state_gla, state_hgrn, norm_mix_pre, norm_mix_post, norm_ffn_pre, norm_ffn_post, w_in, gla_w_lr2, gla_b_lr, gla_onorm, hgrn_lb, hgrn_onorm, w_out, w_ffn_in, w_ffn_out):
    B, L, D = x_prompt.shape
    BS, LS, _ = x_sample.shape
    NS = SEQ_PER_ROWGROUP
    assert B == NS and BS % NS == 0 and L % PROMPT_CHUNK == 0
    TP = B * L
    TS = BS * LS
    T = TP + TS
    assert TP % POST_ROWS == 0 and TS % POST_ROWS == 0

    sm = jax.nn.softmax(hgrn_lb.astype(F32), axis=0)
    lb = jnp.clip(jnp.cumsum(sm, axis=0) - sm[0:1], 0.0, 1.0 - 1e-6)
    lbf = jnp.maximum(lb, LB_FLOOR)
    oml = 1.0 - lb

    c_lr0 = GLA_HEADS * (2 * GLA_DK + 2 * GLA_DV)
    c_lr1 = c_lr0 + GLA_GATE_RANK
    w_t = jnp.swapaxes(w_in, 1, 2).astype(BF16)
    w_g = w_t[:, :c_lr0]
    w_h = w_t[:, c_lr1:]
    assert w_h.shape[1] == HGRN_HEADS * (2 * HGRN_EXPAND + 2 * HGRN_DV) + 2 * D
    w_lr = jnp.pad(w_t[:, c_lr0:c_lr1], ((0, 0), (0, LANES - GLA_GATE_RANK), (0, 0)))
    w_lr2 = jnp.pad(gla_w_lr2, ((0, 0), (0, LANES - GLA_GATE_RANK), (0, 0))).astype(BF16)
    w_o = w_out.astype(BF16)
    w_ffn = w_ffn_in.astype(BF16)
    w_f = w_ffn_out.astype(BF16)

    P_TIME = POST_ROWS // NS
    S_SEQ = POST_ROWS // LS
    pre = _prenorm_call(x_prompt, norm_mix_pre[0], w_lr, w_lr2, gla_b_lr[0],
                        n_seq=NS, n_time=P_TIME, row0=0, total_rows=T)
    x, xn, g_gla = _prenorm_call(x_sample, norm_mix_pre[0], w_lr, w_lr2, gla_b_lr[0],
                                 n_seq=S_SEQ, n_time=LS, row0=TP, total_rows=T, prev=pre)

    n_sample_groups = BS // NS
    gla_p = gla_s = hg_p = hg_s = None
    for l in range(DEPTH):
        og, gla_p, og_s, gla_s = _recurrence_call(
            xn, w_g, 0, g_gla, state_gla, gla_onorm[l], None, None, layer=l, hgrn=False,
            C=PROMPT_CHUNK, CS=LS, n_chunks=L // PROMPT_CHUNK, sample_row0=TP, n_groups=n_sample_groups,
            heads_per_step=HEADS_PER_STEP, prev_states=gla_p, prev_sample_states=gla_s)
        oh, hg_p, oh_s, hg_s = _recurrence_call(
            xn, w_h, 0, None, state_hgrn, hgrn_onorm[l], lbf[l], oml[l], layer=l, hgrn=True,
            C=PROMPT_CHUNK, CS=LS, n_chunks=L // PROMPT_CHUNK, sample_row0=TP, n_groups=n_sample_groups,
            heads_per_step=HEADS_PER_STEP, prev_states=hg_p, prev_sample_states=hg_s)
        og = lax.dynamic_update_slice(og, og_s, (TP, 0))
        oh = lax.dynamic_update_slice(oh, oh_s, (TP, 0))

        h, hn = _outproj_call(og, oh, xn, x, w_h, w_o, norm_mix_post[l], norm_ffn_pre[l], layer=l, tm=POST_ROWS)
        if l + 1 < DEPTH:
            nxt = (norm_mix_pre[l + 1], w_lr, w_lr2, gla_b_lr[l + 1])
            x, xn, g_gla = _ffn_call(hn, h, w_ffn, w_f, norm_ffn_post[l], nxt, layer=l, tm=POST_ROWS)
        else:
            (y_p,) = _ffn_call(hn, h, w_ffn, w_f, norm_ffn_post[l], None, layer=l, tm=POST_ROWS,
                               final=(0, B, L, NS, P_TIME))
            (y_s,) = _ffn_call(hn, h, w_ffn, w_f, norm_ffn_post[l], None, layer=l, tm=POST_ROWS,
                               final=(TP, BS, LS, S_SEQ, LS))

    return (y_p, y_s, gla_p, gla_s, hg_p, hg_s)
```

```python
import functools
from types import SimpleNamespace

import jax
import jax.numpy as jnp
from jax import lax
from jax.experimental import pallas as pl
from jax.experimental.pallas import tpu as pltpu

F32 = jnp.float32
BF16 = jnp.bfloat16

DEPTH = 4
GLA_HEADS = 4
GLA_DK = 128
GLA_DV = 256
GLA_GATE_RANK = 16
GLA_TAU = 16.0
HGRN_HEADS = 8
HGRN_EXPAND = 128
HGRN_DV = 128
LB_FLOOR = 1e-20
EPS = 1e-6
LOG2E = 1.4426950408889634

SUBLANES = 8
LANES = 128
MXU_COLS = 256
SEQ_PER_ROWGROUP = SUBLANES

PROMPT_CHUNK = 64
HEADS_PER_STEP = 2
FINE_BLOCK = SUBLANES
POST_ROWS = 512
VMEM_LIMIT = 52 * 1024 * 1024


def _cparams(sem):
    return pltpu.CompilerParams(dimension_semantics=sem, vmem_limit_bytes=VMEM_LIMIT)


def _rms(x, w):
    return x * lax.rsqrt(jnp.mean(x * x, axis=-1, keepdims=True) + EPS) * w


def _sigmoid(x):
    return 0.5 + 0.5 * jnp.tanh(0.5 * x)


def _dot_t(x, wt):
    return lax.dot_general(x, wt, (((1,), (1,)), ((), ())), preferred_element_type=F32)


def _gla_gate(xn, wlr_t, wlr2, blr):
    glr = _dot_t(xn, wlr_t)
    y = jnp.dot(glr.astype(BF16), wlr2, preferred_element_type=F32) + blr
    ls = -(jnp.maximum(-y, 0.0) + jnp.log1p(jnp.exp(-jnp.abs(y))))
    return ls * (1.0 / GLA_TAU)


def _tile_rows(seq, n_time):
    NS = SEQ_PER_ROWGROUP
    return pl.ds((seq // NS) * n_time * NS + seq % NS, n_time, stride=NS)


def _prenorm_kernel(*refs, n_aliased):
    x_ref, w_ref, wlr_ref, wlr2_ref, blr_ref = refs[:5]
    xr_ref, xn_ref, g_ref, rows_scr = refs[5 + n_aliased:]
    n_seq, n_time, D = x_ref.shape
    for seq in range(n_seq):
        for j in range(D // LANES):
            rows_scr[j, _tile_rows(seq, n_time), :] = x_ref[seq, :, j * LANES:(j + 1) * LANES]
    x = jnp.concatenate([rows_scr[j] for j in range(D // LANES)], axis=1)
    xr_ref[...] = x
    xn = _rms(x, w_ref[...]).astype(BF16)
    xn_ref[...] = xn
    g_ref[...] = _gla_gate(xn, wlr_ref[...], wlr2_ref[...], blr_ref[...])


def _prenorm_call(x3, w, w_lr, w_lr2, b_lr, *, n_seq, n_time, row0, total_rows, prev=None):
    B, L, D = x3.shape
    R, N = w_lr2.shape[1:]
    assert w_lr.shape[1:] == (R, D)
    tm = n_seq * n_time
    assert B % n_seq == 0 and L % n_time == 0 and (B == n_seq or L == n_time) and row0 % tm == 0
    nb, nt = B // n_seq, L // n_time
    rb0 = row0 // tm
    row = lambda i: (rb0 + i, 0)
    const = lambda i: (0, 0)
    l0 = lambda i: (0, 0, 0)
    in_specs = [pl.BlockSpec((n_seq, n_time, D), lambda i: (i // nt, i % nt, 0)),
                pl.BlockSpec((1, D), const),
                pl.BlockSpec((None, R, D), l0),
                pl.BlockSpec((None, R, N), l0),
                pl.BlockSpec((1, N), const)]
    args = [x3, w.reshape(1, D), w_lr, w_lr2, b_lr.reshape(1, N)]
    aliases = {}
    if prev is not None:
        for k, p in enumerate(prev):
            in_specs.append(pl.BlockSpec(memory_space=pl.ANY))
            args.append(p)
            aliases[len(args) - 1] = k
    return pl.pallas_call(
        functools.partial(_prenorm_kernel, n_aliased=len(aliases)),
        grid=(nb * nt,),
        in_specs=in_specs,
        out_specs=[pl.BlockSpec((tm, D), row), pl.BlockSpec((tm, D), row), pl.BlockSpec((tm, N), row)],
        out_shape=[jax.ShapeDtypeStruct((total_rows, D), F32), jax.ShapeDtypeStruct((total_rows, D), BF16),
                   jax.ShapeDtypeStruct((total_rows, N), F32)],
        scratch_shapes=[pltpu.VMEM((D // LANES, tm, LANES), F32)],
        input_output_aliases=aliases,
        compiler_params=_cparams(("parallel",)),
        name="prenorm",
    )(*args)


def _gram_levels(C):
    out = []
    m = C // 2
    while m >= FINE_BLOCK:
        out.append(m)
        m //= 2
    return out


def _recurrence_unit_phases(u):
    C, nv = u.C, u.nv
    NS = SEQ_PER_ROWGROUP
    FINE = min(C, FINE_BLOCK)
    levels = _gram_levels(C)
    NL = len(levels)

    def rows(t):
        return pl.ds(t * NS, NS)

    def seq_rows(s):
        return pl.ds(s, C, stride=NS)

    def vector_phase(hh):
        acc = jnp.zeros((NS, LANES), F32)
        for t in range(C):
            if u.hgrn:
                z = u.p[u.slab(hh, 1), rows(t), :]
                e = jnp.exp(-jnp.abs(z))
                r = 1.0 / (1.0 + e)
                er = e * r
                pos = z >= 0.0
                sig_p = jnp.where(pos, r, er)
                sig_n = jnp.where(pos, er, r)
                oml = u.oml(hh)
                g_t = jnp.log(u.lbf(hh) + oml * sig_p)
                u.kk[hh, rows(t), :] = oml * sig_n
            else:
                g_t = u.g(hh, rows(t))
            acc = acc + g_t
            u.a[hh, t] = jnp.exp2(g_t * LOG2E)
            u.b[hh, t] = acc * LOG2E
        b_last = acc * LOG2E
        d_last = jnp.exp2(b_last)

        d_t = jnp.transpose(jnp.concatenate([d_last] * (LANES // NS), axis=0))
        for s in range(NS):
            u.dcol[hh, s] = jnp.broadcast_to(d_t[:, s:s + 1], (LANES, LANES))

        def q_rows(t):
            q_t = u.p[u.slab(hh, 0), rows(t), :]
            return q_t if u.q_scale == 1.0 else q_t * u.q_scale

        def k_rows(t):
            return u.kk[hh, rows(t), :] if u.hgrn else u.p[u.slab(hh, 1), rows(t), :]

        for t in range(C):
            q_t, k_t, b_t = q_rows(t), k_rows(t), u.b[hh, t]
            u.zs[hh, NL, rows(t), :] = q_t * jnp.exp2(b_t)
            u.zs[hh, NL + 1, rows(t), :] = k_t * jnp.exp2(b_last - b_t)
            for j, m in enumerate(levels):
                r_idx = (t // (2 * m)) * (2 * m) + m - 1
                if (t % (2 * m)) >= m:
                    u.zs[hh, j, rows(t), :] = q_t * jnp.exp2(b_t - u.b[hh, r_idx])
                else:
                    u.zs[hh, j, rows(t), :] = k_t * jnp.exp2(u.b[hh, r_idx] - b_t)

        for t0 in range(0, C, FINE):
            k_blk, v_blk, decay = {}, {}, {}
            for t in range(t0, t0 + FINE):
                q_t = q_rows(t)
                k_blk[t] = k_rows(t)
                v_blk[t] = [u.p[u.slab(hh, 2 + i), rows(t), :] for i in range(nv)]
                a_t = u.a[hh, t]
                for s in range(t0, t):
                    decay[s] = a_t if s == t - 1 else decay[s] * a_t
                w = jnp.sum(q_t * k_blk[t], axis=-1, keepdims=True)
                o_t = [w * v_blk[t][i] for i in range(nv)]
                for s in range(t0, t):
                    w = jnp.sum(q_t * (k_blk[s] * decay[s]), axis=-1, keepdims=True)
                    o_t = [o_t[i] + w * v_blk[s][i] for i in range(nv)]
                for i in range(nv):
                    u.oacc[hh, i, rows(t), :] = o_t[i]

    ti = lax.broadcasted_iota(jnp.int32, (C, C), 0)
    si = lax.broadcasted_iota(jnp.int32, (C, C), 1)
    xr = jnp.bitwise_xor(ti, si)
    lower = ti > si
    masks = [lower & (xr >= m) & (xr < 2 * m) for m in levels]

    def matmul_phase(hh):
        def load_v(s):
            vs = [u.p[u.slab(hh, 2 + i), seq_rows(s), :] for i in range(nv)]
            vb = vs[0] if nv == 1 else jnp.concatenate(vs, axis=1)
            return vb.astype(BF16)

        scores = []
        for s in range(NS if NL else 0):
            sc = jnp.zeros((C, C), F32)
            for j in range(NL):
                z = u.zs[hh, j, seq_rows(s), :].astype(BF16)
                gram = lax.dot_general(z, z, (((1,), (1,)), ((), ())), preferred_element_type=F32)
                sc = jnp.where(masks[j], gram, sc)
            scores.append(sc.astype(BF16))
        one_dot = NL > 0 and nv > 1
        inter = []
        for s in range(NS):
            zq = u.zs[hh, NL, seq_rows(s), :].astype(BF16)
            S = u.S_in(s, hh)
            o = jnp.dot(zq, S.astype(BF16), preferred_element_type=F32)
            dc = u.dcol[hh, s]
            if nv > 1:
                dc = jnp.concatenate([dc] * nv, axis=1)
            if one_dot:
                inter.append((o, dc * S))
            else:
                zk = u.zs[hh, NL + 1, seq_rows(s), :].astype(BF16)
                upd = lax.dot_general(zk, load_v(s), (((0,), (0,)), ((), ())), preferred_element_type=F32)
                u.S_out(s, hh, dc * S + upd)
                inter.append((o, None))
        for s in range(NS):
            o, decayed = inter[s]
            if one_dot:
                zk_t = jnp.transpose(u.zs[hh, NL + 1, seq_rows(s), :]).astype(BF16)
                res = jnp.dot(jnp.concatenate([zk_t, scores[s]], axis=0), load_v(s), preferred_element_type=F32)
                u.S_out(s, hh, decayed + res[:LANES])
                o = o + res[LANES:]
            elif NL:
                o = o + jnp.dot(scores[s], load_v(s), preferred_element_type=F32)
            for i in range(nv):
                sl = slice(i * LANES, (i + 1) * LANES)
                u.oacc[hh, i, seq_rows(s), :] = u.oacc[hh, i, seq_rows(s), :] + o[:, sl]

    def gate_phase(hh):
        if nv == 1:
            o_all = u.oacc[hh, 0]
        else:
            o_all = jnp.concatenate([u.oacc[hh, i] for i in range(nv)], axis=1)
        y = o_all * lax.rsqrt(jnp.mean(o_all * o_all, axis=-1, keepdims=True) + EPS) * u.onw()
        gs = [u.p[u.slab(hh, 2 + nv + i)] for i in range(nv)]
        gt = gs[0] if nv == 1 else jnp.concatenate(gs, axis=1)
        u.out(hh, y * (gt * _sigmoid(gt)))

    return vector_phase, matmul_phase, gate_phase


def _recurrence_kernel(*refs, C, CS, dv, hgrn, q_scale, n_aliased, HB):
    nv = dv // LANES
    NS = SEQ_PER_ROWGROUP
    R = NS * C
    it = iter(refs)
    xcur_ref, xnext_ref, xs_cur_ref, xs_next_ref = next(it), next(it), next(it), next(it)
    w_refs = [next(it) for _ in range(4)]
    g_ref, gs_ref = (None, None) if hgrn else (next(it), next(it))
    onw_ref = next(it)
    lbf_ref, oml_ref, lbfs_ref, omls_ref = (next(it), next(it), next(it), next(it)) if hgrn else (None,) * 4
    s0_ref = next(it)
    for _ in range(n_aliased):
        next(it)
    o_ref, sout_ref, os_ref, ssout_ref = next(it), next(it), next(it), next(it)
    p_scr, pnext_scr, a_scr, b_scr, zs_scr = next(it), next(it), next(it), next(it), next(it)
    kk_scr = next(it) if hgrn else None
    oacc_scr, dcol_scr, S_scr = next(it), next(it), next(it)
    ps_scr, psnext_scr, as_scr, bs_scr, zss_scr = next(it), next(it), next(it), next(it), next(it)
    kks_scr = next(it) if hgrn else None
    oaccs_scr, dcols_scr = next(it), next(it)

    c_idx = pl.program_id(1)
    hs = lax.rem(c_idx, HB)

    @pl.when(c_idx == 0)
    def _():
        S_scr[...] = jnp.zeros_like(S_scr)

    def project(xp_ref, xs_ref):
        x = jnp.concatenate([xp_ref[...], xs_ref[...]], axis=0)
        base = 0
        for w_ref in w_refs:
            p = _dot_t(x, w_ref[...])
            n = w_ref.shape[0] // LANES
            for i in range(n):
                pnext_scr[base + i] = p[:R, i * LANES:(i + 1) * LANES]
                psnext_scr[base + i] = p[R:, i * LANES:(i + 1) * LANES]
            base += n

    @pl.when(c_idx == 0)
    def _():
        project(xcur_ref, xs_cur_ref)

    p_scr[...] = pnext_scr[...]
    ps_scr[...] = psnext_scr[...]

    def slab_of(head_slot, k):
        if k < 2:
            return k * HB + head_slot
        if k < 2 + nv:
            return 2 * HB + head_slot * nv + (k - 2)
        return (2 + nv) * HB + head_slot * nv + (k - 2 - nv)

    def prompt_out(hh, val):
        o_ref[:, hh * dv:(hh + 1) * dv] = val

    def prompt_S_out(s, hh, val):
        S_scr[s, hh] = val

    def sample_out(hh, val):
        os_ref[...] = val

    def sample_S_out(s, hh, val):
        ssout_ref[s] = val

    lane = lambda hh: slice(hh * LANES, (hh + 1) * LANES)
    prompt = SimpleNamespace(
        C=C, nv=nv, hgrn=hgrn, q_scale=q_scale, heads=range(HB), p=p_scr, slab=slab_of,
        g=lambda hh, r: g_ref[r, lane(hh)], lbf=lambda hh: lbf_ref[:, lane(hh)],
        oml=lambda hh: oml_ref[:, lane(hh)], onw=lambda: onw_ref[...],
        a=a_scr, b=b_scr, zs=zs_scr, kk=kk_scr, oacc=oacc_scr, dcol=dcol_scr,
        S_in=lambda s, hh: S_scr[s, hh], S_out=prompt_S_out, out=prompt_out)
    sample = SimpleNamespace(
        C=CS, nv=nv, hgrn=hgrn, q_scale=q_scale, heads=range(1), p=ps_scr,
        slab=lambda hh, k: slab_of(hs, k),
        g=lambda hh, r: gs_ref[r, :], lbf=lambda hh: lbfs_ref[...], oml=lambda hh: omls_ref[...],
        onw=lambda: onw_ref[...],
        a=as_scr, b=bs_scr, zs=zss_scr, kk=kks_scr, oacc=oaccs_scr, dcol=dcols_scr,
        S_in=lambda s, hh: s0_ref[s], S_out=sample_S_out, out=sample_out)

    units = [(un, _recurrence_unit_phases(un)) for un in (prompt, sample)]
    for un, (vector_phase, _, _) in units:
        for hh in un.heads:
            vector_phase(hh)
    project(xnext_ref, xs_next_ref)
    for un, (_, matmul_phase, _) in units:
        for hh in un.heads:
            matmul_phase(hh)
    for un, (_, _, gate_phase) in units:
        for hh in un.heads:
            gate_phase(hh)

    @pl.when(c_idx == pl.num_programs(1) - 1)
    def _():
        sout_ref[...] = S_scr[...]


def _recurrence_call(xn, w_c, w_col0, g_gla, s_in, onw, lbf, oml, *, layer, hgrn, C, CS, n_chunks, sample_row0,
                     n_groups, heads_per_step, prev_states=None, prev_sample_states=None):
    NS = SEQ_PER_ROWGROUP
    R, RS = NS * C, NS * CS
    H = HGRN_HEADS if hgrn else GLA_HEADS
    dv = HGRN_DV if hgrn else GLA_DV
    dk = LANES
    nv = dv // LANES
    T, D = xn.shape
    W = (2 + 2 * nv) * LANES
    HB = heads_per_step
    assert H % HB == 0 and n_chunks == n_groups * HB and sample_row0 % RS == 0
    rbs = sample_row0 // RS
    last = n_chunks - 1
    s_grp = lambda c: c // HB
    s_head = lambda h, c: h * HB + c % HB

    in_specs = [pl.BlockSpec((R, D), lambda h, c: (0, 0)),
                pl.BlockSpec((R, D), lambda h, c: (jnp.minimum(c + 1, last), 0)),
                pl.BlockSpec((RS, D), lambda h, c: (rbs, 0)),
                pl.BlockSpec((RS, D), lambda h, c: (rbs + s_grp(jnp.minimum(c + 1, last)), 0))]
    args = [xn, xn, xn, xn]
    col = w_col0
    for width in (LANES, LANES, dv, dv):
        blk = HB * width
        assert col % blk == 0
        in_specs.append(pl.BlockSpec((None, blk, D), lambda h, c, j=col // blk: (layer, j + h, 0)))
        args.append(w_c)
        col += H * width
    if not hgrn:
        in_specs += [pl.BlockSpec((R, HB * LANES), lambda h, c: (c, h)),
                     pl.BlockSpec((RS, LANES), lambda h, c: (rbs + s_grp(c), s_head(h, c)))]
        args += [g_gla, g_gla]
    in_specs.append(pl.BlockSpec((1, dv), lambda h, c: (0, 0)))
    args.append(onw.reshape(1, dv))
    if hgrn:
        in_specs += [pl.BlockSpec((1, HB * LANES), lambda h, c: (0, h)),
                     pl.BlockSpec((1, HB * LANES), lambda h, c: (0, h)),
                     pl.BlockSpec((1, LANES), lambda h, c: (0, s_head(h, c))),
                     pl.BlockSpec((1, LANES), lambda h, c: (0, s_head(h, c)))]
        args += [lbf.reshape(1, -1), oml.reshape(1, -1)] * 2
    ss_map = lambda h, c: (layer, s_grp(c), s_head(h, c), 0, 0)
    in_specs.append(pl.BlockSpec((None, NS, None, dk, dv), ss_map))
    args.append(s_in)
    aliases = {}
    for k, prev in ((1, prev_states), (3, prev_sample_states)):
        if prev is not None:
            in_specs.append(pl.BlockSpec(memory_space=pl.ANY))
            args.append(prev)
            aliases[len(args) - 1] = k

    out_specs = [pl.BlockSpec((R, HB * dv), lambda h, c: (c, h)),
                 pl.BlockSpec((None, NS, HB, dk, dv), lambda h, c: (layer, 0, h, 0, 0)),
                 pl.BlockSpec((RS, dv), lambda h, c: (s_grp(c), s_head(h, c))),
                 pl.BlockSpec((None, NS, None, dk, dv), ss_map)]
    out_shape = [jax.ShapeDtypeStruct((T, H * dv), F32),
                 jax.ShapeDtypeStruct((DEPTH, NS, H, dk, dv), F32),
                 jax.ShapeDtypeStruct((n_groups * RS, H * dv), F32),
                 jax.ShapeDtypeStruct((DEPTH, n_groups * NS, H, dk, dv), F32)]

    def unit_scratch(heads, Cu, prompt_state):
        Ru = NS * Cu
        sc = [pltpu.VMEM((HB * W // LANES, Ru, LANES), F32),
              pltpu.VMEM((HB * W // LANES, Ru, LANES), F32),
              pltpu.VMEM((heads, Cu, NS, LANES), F32),
              pltpu.VMEM((heads, Cu, NS, LANES), F32),
              pltpu.VMEM((heads, len(_gram_levels(Cu)) + 2, Ru, LANES), F32)]
        if hgrn:
            sc.append(pltpu.VMEM((heads, Ru, LANES), F32))
        sc += [pltpu.VMEM((heads, nv, Ru, LANES), F32),
               pltpu.VMEM((heads, NS, LANES, LANES), F32)]
        if prompt_state:
            sc.append(pltpu.VMEM((NS, heads, dk, dv), F32))
        return sc

    kern = functools.partial(_recurrence_kernel, C=C, CS=CS, dv=dv, hgrn=hgrn,
                             q_scale=1.0 if hgrn else GLA_DK ** -0.5, n_aliased=len(aliases), HB=HB)
    return pl.pallas_call(
        kern,
        grid=(H // HB, n_chunks),
        in_specs=in_specs,
        out_specs=out_specs,
        out_shape=out_shape,
        scratch_shapes=unit_scratch(HB, C, True) + unit_scratch(1, CS, False),
        input_output_aliases=aliases,
        compiler_params=_cparams(("parallel", "arbitrary")),
        name="hgrn" if hgrn else "gla",
    )(*args)


def _outproj_kernel(og_ref, oh_ref, xn_ref, x_ref, wga_ref, wgb_ref, wo_ref, npost_ref, npre_ref, h_ref, hn_ref):
    xn = xn_ref[...]
    ga = _dot_t(xn, wga_ref[...])
    gb = _dot_t(xn, wgb_ref[...])
    merged = og_ref[...] * _sigmoid(ga) + oh_ref[...] * _sigmoid(gb)
    m = jnp.dot(merged.astype(BF16), wo_ref[...], preferred_element_type=F32)
    h = x_ref[...] + _rms(m, npost_ref[...])
    h_ref[...] = h
    hn_ref[...] = _rms(h, npre_ref[...]).astype(BF16)


def _outproj_call(og, oh, xn, x, w_c, w_o, npost, npre, *, layer, tm):
    T, D = x.shape
    row = lambda i: (i, 0)
    const = lambda i: (0, 0)
    once = pl.Buffered(1)
    n_blk = w_c.shape[1] // D
    return pl.pallas_call(
        _outproj_kernel,
        grid=(T // tm,),
        in_specs=[pl.BlockSpec((tm, D), row),
                  pl.BlockSpec((tm, D), row),
                  pl.BlockSpec((tm, D), row),
                  pl.BlockSpec((tm, D), row),
                  pl.BlockSpec((None, D, D), lambda i: (layer, n_blk - 2, 0), pipeline_mode=once),
                  pl.BlockSpec((None, D, D), lambda i: (layer, n_blk - 1, 0), pipeline_mode=once),
                  pl.BlockSpec((None, D, D), lambda i: (layer, 0, 0), pipeline_mode=once),
                  pl.BlockSpec((1, D), const),
                  pl.BlockSpec((1, D), const)],
        out_specs=[pl.BlockSpec((tm, D), row), pl.BlockSpec((tm, D), row)],
        out_shape=[jax.ShapeDtypeStruct((T, D), F32), jax.ShapeDtypeStruct((T, D), BF16)],
        compiler_params=_cparams(("parallel",)),
        name="outproj",
    )(og, oh, xn, x, w_c, w_c, w_o, npost.reshape(1, D), npre.reshape(1, D))


def _ffn_kernel(*refs, splits, emit_next):
    hn_ref, h_ref, wa_ref, wu_ref, wf_ref, nfpost_ref = refs[:6]
    if emit_next:
        nnext_ref, wlr_ref, wlr2_ref, blr_ref, y_ref, yn_ref, g_ref = refs[6:]
    else:
        y_ref, rows_scr = refs[6:]
    hn = hn_ref[...]
    f = None
    for lo, hi in splits:
        sl = slice(lo, hi)
        a = jnp.dot(hn, wa_ref[:, sl], preferred_element_type=F32)
        u = jnp.dot(hn, wu_ref[:, sl], preferred_element_type=F32)
        g = (a * _sigmoid(a) * u).astype(BF16)
        part = jnp.dot(g, wf_ref[sl, :], preferred_element_type=F32)
        f = part if f is None else f + part
    y = h_ref[...] + _rms(f, nfpost_ref[...])
    if emit_next:
        y_ref[...] = y
        yn = _rms(y, nnext_ref[...]).astype(BF16)
        yn_ref[...] = yn
        g_ref[...] = _gla_gate(yn, wlr_ref[...], wlr2_ref[...], blr_ref[...])
    else:
        n_seq, n_time, D = y_ref.shape
        for j in range(D // LANES):
            rows_scr[j] = y[:, j * LANES:(j + 1) * LANES]
        for seq in range(n_seq):
            for j in range(D // LANES):
                y_ref[seq, :, j * LANES:(j + 1) * LANES] = rows_scr[j, _tile_rows(seq, n_time), :]


def _ffn_call(hn, h, w_ffn, w_f, nfpost, next_layer, *, layer, tm, final=None):
    T, D = h.shape
    F = w_f.shape[1]
    rb0, n_tiles = 0, T // tm
    if final is not None:
        row0, B, L, n_seq, n_time = final
        assert n_seq * n_time == tm and row0 % tm == 0 and (B == n_seq or L == n_time)
        rb0, n_tiles, nt = row0 // tm, B * L // tm, L // n_time
    row = lambda i: (rb0 + i, 0)
    const = lambda i: (0, 0)
    once = pl.Buffered(1)
    half = (F // MXU_COLS + 1) // 2 * MXU_COLS
    assert F % MXU_COLS == 0 and F % LANES == 0
    in_specs = [pl.BlockSpec((tm, D), row),
                pl.BlockSpec((tm, D), row),
                pl.BlockSpec((None, D, F), lambda i: (layer, 0, 0), pipeline_mode=once),
                pl.BlockSpec((None, D, F), lambda i: (layer, 0, 1), pipeline_mode=once),
                pl.BlockSpec((None, F, D), lambda i: (layer, 0, 0), pipeline_mode=once),
                pl.BlockSpec((1, D), const)]
    args = [hn, h, w_ffn, w_ffn, w_f, nfpost.reshape(1, D)]
    scratch = []
    if next_layer is None:
        out_specs = [pl.BlockSpec((n_seq, n_time, D), lambda i: (i // nt, i % nt, 0))]
        out_shape = [jax.ShapeDtypeStruct((B, L, D), F32)]
        scratch = [pltpu.VMEM((D // LANES, tm, LANES), F32)]
    else:
        out_specs = [pl.BlockSpec((tm, D), row)]
        out_shape = [jax.ShapeDtypeStruct((T, D), F32)]
        nnext, w_lr, w_lr2, b_lr = next_layer
        R, N = w_lr2.shape[1:]
        lnext = lambda i: (layer + 1, 0, 0)
        in_specs += [pl.BlockSpec((1, D), const),
                     pl.BlockSpec((None, R, D), lnext, pipeline_mode=once),
                     pl.BlockSpec((None, R, N), lnext, pipeline_mode=once),
                     pl.BlockSpec((1, N), const)]
        args += [nnext.reshape(1, D), w_lr, w_lr2, b_lr.reshape(1, N)]
        out_specs += [pl.BlockSpec((tm, D), row), pl.BlockSpec((tm, N), row)]
        out_shape += [jax.ShapeDtypeStruct((T, D), BF16), jax.ShapeDtypeStruct((T, N), F32)]
    return pl.pallas_call(
        functools.partial(_ffn_kernel, splits=((0, half), (half, F)), emit_next=next_layer is not None),
        grid=(n_tiles,),
        in_specs=in_specs,
        out_specs=out_specs,
        out_shape=out_shape,
        scratch_shapes=scratch,
        compiler_params=_cparams(("parallel",)),
        name="ffn",
    )(*args)


def kernel(x_prompt, x_sample, state_gla, state_hgrn, norm_mix_pre, norm_mix_post, norm_ffn_pre, norm_ffn_post, w_in, gla_w_lr2, gla_b_lr, gla_onorm, hgrn_lb, hgrn_onorm, w_out, w_ffn_in, w_ffn_out):
    B, L, D = x_prompt.shape
    BS, LS, _ = x_sample.shape
    NS = SEQ_PER_ROWGROUP
    assert B == NS and BS % NS == 0 and L % PROMPT_CHUNK == 0
    TP = B * L
    TS = BS * LS
    T = TP + TS
    assert TP % POST_ROWS == 0 and TS % POST_ROWS == 0

    sm = jax.nn.softmax(hgrn_lb.astype(F32), axis=0)
    lb = jnp.clip(jnp.cumsum(sm, axis=0) - sm[0:1], 0.0, 1.0 - 1e-6)
    lbf = jnp.maximum(lb, LB_FLOOR)
    oml = 1.0 - lb

    c_lr0 = GLA_HEADS * (2 * GLA_DK + 2 * GLA_DV)
    c_lr1 = c_lr0 + GLA_GATE_RANK
    w_t = jnp.swapaxes(w_in, 1, 2).astype(BF16)
    w_g = w_t
    w_h = w_t[:, c_lr1:]
    assert w_h.shape[1] == HGRN_HEADS * (2 * HGRN_EXPAND + 2 * HGRN_DV) + 2 * D
    w_lr = jnp.pad(w_t[:, c_lr0:c_lr1], ((0, 0), (0, LANES - GLA_GATE_RANK), (0, 0)))
    w_lr2 = jnp.pad(gla_w_lr2, ((0, 0), (0, LANES - GLA_GATE_RANK), (0, 0))).astype(BF16)
    w_o = w_out.astype(BF16)
    w_ffn = w_ffn_in.astype(BF16)
    w_f = w_ffn_out.astype(BF16)

    P_TIME = POST_ROWS // NS
    S_SEQ = POST_ROWS // LS
    pre = _prenorm_call(x_prompt, norm_mix_pre[0], w_lr, w_lr2, gla_b_lr[0],
                        n_seq=NS, n_time=P_TIME, row0=0, total_rows=T)
    x, xn, g_gla = _prenorm_call(x_sample, norm_mix_pre[0], w_lr, w_lr2, gla_b_lr[0],
                                 n_seq=S_SEQ, n_time=LS, row0=TP, total_rows=T, prev=pre)

    n_sample_groups = BS // NS
    gla_p = gla_s = hg_p = hg_s = None
    for l in range(DEPTH):
        og, gla_p, og_s, gla_s = _recurrence_call(
            xn, w_g, 0, g_gla, state_gla, gla_onorm[l], None, None, layer=l, hgrn=False,
            C=PROMPT_CHUNK, CS=LS, n_chunks=L // PROMPT_CHUNK, sample_row0=TP, n_groups=n_sample_groups,
            heads_per_step=HEADS_PER_STEP, prev_states=gla_p, prev_sample_states=gla_s)
        oh, hg_p, oh_s, hg_s = _recurrence_call(
            xn, w_h, 0, None, state_hgrn, hgrn_onorm[l], lbf[l], oml[l], layer=l, hgrn=True,
            C=PROMPT_CHUNK, CS=LS, n_chunks=L // PROMPT_CHUNK, sample_row0=TP, n_groups=n_sample_groups,
            heads_per_step=HEADS_PER_STEP, prev_states=hg_p, prev_sample_states=hg_s)
        og = lax.dynamic_update_slice(og, og_s, (TP, 0))
        oh = lax.dynamic_update_slice(oh, oh_s, (TP, 0))

        h, hn = _outproj_call(og, oh, xn, x, w_h, w_o, norm_mix_post[l], norm_ffn_pre[l], layer=l, tm=POST_ROWS)
        if l + 1 < DEPTH:
            nxt = (norm_mix_pre[l + 1], w_lr, w_lr2, gla_b_lr[l + 1])
            x, xn, g_gla = _ffn_call(hn, h, w_ffn, w_f, norm_ffn_post[l], nxt, layer=l, tm=POST_ROWS)
        else:
            (y_p,) = _ffn_call(hn, h, w_ffn, w_f, norm_ffn_post[l], None, layer=l, tm=POST_ROWS,
                               final=(0, B, L, NS, P_TIME))
            (y_s,) = _ffn_call(hn, h, w_ffn, w_f, norm_ffn_post[l], None, layer=l, tm=POST_ROWS,
                               final=(TP, BS, LS, S_SEQ, LS))

    return (y_p, y_s, gla_p, gla_s, hg_p, hg_s)
```

```python
import functools
from types import SimpleNamespace

import jax
import jax.numpy as jnp
from jax import lax
from jax.experimental import pallas as pl
from jax.experimental.pallas import tpu as pltpu

F32 = jnp.float32
BF16 = jnp.bfloat16

DEPTH = 4
GLA_HEADS = 4
GLA_DK = 128
GLA_DV = 256
GLA_GATE_RANK = 16
GLA_TAU = 16.0
HGRN_HEADS = 8
HGRN_EXPAND = 128
HGRN_DV = 128
LB_FLOOR = 1e-20
EPS = 1e-6
LOG2E = 1.4426950408889634

SUBLANES = 8
LANES = 128
MXU_COLS = 256
SEQ_PER_ROWGROUP = SUBLANES

PROMPT_CHUNK = 64
HEADS_PER_STEP = 2
FINE_BLOCK = SUBLANES
POST_ROWS = 512
VMEM_LIMIT = 52 * 1024 * 1024


def _cparams(sem):
    return pltpu.CompilerParams(dimension_semantics=sem, vmem_limit_bytes=VMEM_LIMIT)


def _rms(x, w):
    return x * lax.rsqrt(jnp.mean(x * x, axis=-1, keepdims=True) + EPS) * w


def _sigmoid(x):
    return 0.5 + 0.5 * jnp.tanh(0.5 * x)


def _dot_t(x, wt):
    return lax.dot_general(x, wt, (((1,), (1,)), ((), ())), preferred_element_type=F32)


def _gla_gate(xn, wlr_t, wlr2, blr):
    glr = _dot_t(xn, wlr_t)
    y = jnp.dot(glr.astype(BF16), wlr2, preferred_element_type=F32) + blr
    ls = -(jnp.maximum(-y, 0.0) + jnp.log1p(jnp.exp(-jnp.abs(y))))
    return ls * (1.0 / GLA_TAU)


def _tile_rows(seq, n_time):
    NS = SEQ_PER_ROWGROUP
    return pl.ds((seq // NS) * n_time * NS + seq % NS, n_time, stride=NS)


def _prenorm_kernel(*refs, n_aliased):
    x_ref, w_ref, wlr_ref, wlr2_ref, blr_ref = refs[:5]
    xr_ref, xn_ref, g_ref, rows_scr = refs[5 + n_aliased:]
    n_seq, n_time, D = x_ref.shape
    for seq in range(n_seq):
        for j in range(D // LANES):
            rows_scr[j, _tile_rows(seq, n_time), :] = x_ref[seq, :, j * LANES:(j + 1) * LANES]
    x = jnp.concatenate([rows_scr[j] for j in range(D // LANES)], axis=1)
    xr_ref[...] = x
    xn = _rms(x, w_ref[...]).astype(BF16)
    xn_ref[...] = xn
    g_ref[...] = _gla_gate(xn, wlr_ref[...], wlr2_ref[...], blr_ref[...])


def _prenorm_call(x3, w, w_lr, w_lr2, b_lr, *, n_seq, n_time, row0, total_rows, prev=None):
    B, L, D = x3.shape
    R, N = w_lr2.shape[1:]
    assert w_lr.shape[1:] == (R, D)
    tm = n_seq * n_time
    assert B % n_seq == 0 and L % n_time == 0 and (B == n_seq or L == n_time) and row0 % tm == 0
    nb, nt = B // n_seq, L // n_time
    rb0 = row0 // tm
    row = lambda i: (rb0 + i, 0)
    const = lambda i: (0, 0)
    l0 = lambda i: (0, 0, 0)
    in_specs = [pl.BlockSpec((n_seq, n_time, D), lambda i: (i // nt, i % nt, 0)),
                pl.BlockSpec((1, D), const),
                pl.BlockSpec((None, R, D), l0),
                pl.BlockSpec((None, R, N), l0),
                pl.BlockSpec((1, N), const)]
    args = [x3, w.reshape(1, D), w_lr, w_lr2, b_lr.reshape(1, N)]
    aliases = {}
    if prev is not None:
        for k, p in enumerate(prev):
            in_specs.append(pl.BlockSpec(memory_space=pl.ANY))
            args.append(p)
            aliases[len(args) - 1] = k
    return pl.pallas_call(
        functools.partial(_prenorm_kernel, n_aliased=len(aliases)),
        grid=(nb * nt,),
        in_specs=in_specs,
        out_specs=[pl.BlockSpec((tm, D), row), pl.BlockSpec((tm, D), row), pl.BlockSpec((tm, N), row)],
        out_shape=[jax.ShapeDtypeStruct((total_rows, D), F32), jax.ShapeDtypeStruct((total_rows, D), BF16),
                   jax.ShapeDtypeStruct((total_rows, N), F32)],
        scratch_shapes=[pltpu.VMEM((D // LANES, tm, LANES), F32)],
        input_output_aliases=aliases,
        compiler_params=_cparams(("parallel",)),
        name="prenorm",
    )(*args)


def _gram_levels(C):
    out = []
    m = C // 2
    while m >= FINE_BLOCK:
        out.append(m)
        m //= 2
    return out


def _recurrence_unit_phases(u):
    C, nv = u.C, u.nv
    NS = SEQ_PER_ROWGROUP
    FINE = min(C, FINE_BLOCK)
    levels = _gram_levels(C)
    NL = len(levels)

    def rows(t):
        return pl.ds(t * NS, NS)

    def seq_rows(s):
        return pl.ds(s, C, stride=NS)

    def vector_phase(hh):
        acc = jnp.zeros((NS, LANES), F32)
        for t in range(C):
            if u.hgrn:
                z = u.p[u.slab(hh, 1), rows(t), :]
                e = jnp.exp(-jnp.abs(z))
                r = 1.0 / (1.0 + e)
                er = e * r
                pos = z >= 0.0
                sig_p = jnp.where(pos, r, er)
                sig_n = jnp.where(pos, er, r)
                oml = u.oml(hh)
                g_t = jnp.log(u.lbf(hh) + oml * sig_p)
                u.kk[hh, rows(t), :] = oml * sig_n
            else:
                g_t = u.g(hh, rows(t))
            acc = acc + g_t
            u.a[hh, t] = jnp.exp2(g_t * LOG2E)
            u.b[hh, t] = acc * LOG2E
        b_last = acc * LOG2E
        d_last = jnp.exp2(b_last)

        d_t = jnp.transpose(jnp.concatenate([d_last] * (LANES // NS), axis=0))
        for s in range(NS):
            u.dcol[hh, s] = jnp.broadcast_to(d_t[:, s:s + 1], (LANES, LANES))

        def q_rows(t):
            q_t = u.p[u.slab(hh, 0), rows(t), :]
            return q_t if u.q_scale == 1.0 else q_t * u.q_scale

        def k_rows(t):
            return u.kk[hh, rows(t), :] if u.hgrn else u.p[u.slab(hh, 1), rows(t), :]

        for t in range(C):
            q_t, k_t, b_t = q_rows(t), k_rows(t), u.b[hh, t]
            u.zs[hh, NL, rows(t), :] = q_t * jnp.exp2(b_t)
            u.zs[hh, NL + 1, rows(t), :] = k_t * jnp.exp2(b_last - b_t)
            for j, m in enumerate(levels):
                r_idx = (t // (2 * m)) * (2 * m) + m - 1
                if (t % (2 * m)) >= m:
                    u.zs[hh, j, rows(t), :] = q_t * jnp.exp2(b_t - u.b[hh, r_idx])
                else:
                    u.zs[hh, j, rows(t), :] = k_t * jnp.exp2(u.b[hh, r_idx] - b_t)

        for t0 in range(0, C, FINE):
            k_blk, v_blk, decay = {}, {}, {}
            for t in range(t0, t0 + FINE):
                q_t = q_rows(t)
                k_blk[t] = k_rows(t)
                v_blk[t] = [u.p[u.slab(hh, 2 + i), rows(t), :] for i in range(nv)]
                a_t = u.a[hh, t]
                for s in range(t0, t):
                    decay[s] = a_t if s == t - 1 else decay[s] * a_t
                w = jnp.sum(q_t * k_blk[t], axis=-1, keepdims=True)
                o_t = [w * v_blk[t][i] for i in range(nv)]
                for s in range(t0, t):
                    w = jnp.sum(q_t * (k_blk[s] * decay[s]), axis=-1, keepdims=True)
                    o_t = [o_t[i] + w * v_blk[s][i] for i in range(nv)]
                for i in range(nv):
                    u.oacc[hh, i, rows(t), :] = o_t[i]

    ti = lax.broadcasted_iota(jnp.int32, (C, C), 0)
    si = lax.broadcasted_iota(jnp.int32, (C, C), 1)
    xr = jnp.bitwise_xor(ti, si)
    lower = ti > si
    masks = [lower & (xr >= m) & (xr < 2 * m) for m in levels]

    def matmul_phase(hh):
        def load_v(s):
            vs = [u.p[u.slab(hh, 2 + i), seq_rows(s), :] for i in range(nv)]
            vb = vs[0] if nv == 1 else jnp.concatenate(vs, axis=1)
            return vb.astype(BF16)

        scores = []
        for s in range(NS if NL else 0):
            sc = jnp.zeros((C, C), F32)
            for j in range(NL):
                z = u.zs[hh, j, seq_rows(s), :].astype(BF16)
                gram = lax.dot_general(z, z, (((1,), (1,)), ((), ())), preferred_element_type=F32)
                sc = jnp.where(masks[j], gram, sc)
            scores.append(sc.astype(BF16))
        one_dot = NL > 0 and nv > 1
        inter = []
        for s in range(NS):
            zq = u.zs[hh, NL, seq_rows(s), :].astype(BF16)
            S = u.S_in(s, hh)
            o = jnp.dot(zq, S.astype(BF16), preferred_element_type=F32)
            dc = u.dcol[hh, s]
            if nv > 1:
                dc = jnp.concatenate([dc] * nv, axis=1)
            if one_dot:
                inter.append((o, dc * S))
            else:
                zk = u.zs[hh, NL + 1, seq_rows(s), :].astype(BF16)
                upd = lax.dot_general(zk, load_v(s), (((0,), (0,)), ((), ())), preferred_element_type=F32)
                u.S_out(s, hh, dc * S + upd)
                inter.append((o, None))
        for s in range(NS):
            o, decayed = inter[s]
            if one_dot:
                zk_t = jnp.transpose(u.zs[hh, NL + 1, seq_rows(s), :]).astype(BF16)
                res = jnp.dot(jnp.concatenate([zk_t, scores[s]], axis=0), load_v(s), preferred_element_type=F32)
                u.S_out(s, hh, decayed + res[:LANES])
                o = o + res[LANES:]
            elif NL:
                o = o + jnp.dot(scores[s], load_v(s), preferred_element_type=F32)
            for i in range(nv):
                sl = slice(i * LANES, (i + 1) * LANES)
                u.oacc[hh, i, seq_rows(s), :] = u.oacc[hh, i, seq_rows(s), :] + o[:, sl]

    def gate_phase(hh):
        if nv == 1:
            o_all = u.oacc[hh, 0]
        else:
            o_all = jnp.concatenate([u.oacc[hh, i] for i in range(nv)], axis=1)
        y = o_all * lax.rsqrt(jnp.mean(o_all * o_all, axis=-1, keepdims=True) + EPS) * u.onw()
        gs = [u.p[u.slab(hh, 2 + nv + i)] for i in range(nv)]
        gt = gs[0] if nv == 1 else jnp.concatenate(gs, axis=1)
        u.out(hh, y * (gt * _sigmoid(gt)))

    return vector_phase, matmul_phase, gate_phase


def _recurrence_kernel(*refs, C, CS, dv, hgrn, q_scale, n_aliased, HB):
    nv = dv // LANES
    NS = SEQ_PER_ROWGROUP
    R = NS * C
    it = iter(refs)
    xcur_ref, xnext_ref, xs_cur_ref, xs_next_ref = next(it), next(it), next(it), next(it)
    w_refs = [next(it) for _ in range(4)]
    g_ref, gs_ref = (None, None) if hgrn else (next(it), next(it))
    onw_ref = next(it)
    lbf_ref, oml_ref, lbfs_ref, omls_ref = (next(it), next(it), next(it), next(it)) if hgrn else (None,) * 4
    s0_ref = next(it)
    for _ in range(n_aliased):
        next(it)
    o_ref, sout_ref, os_ref, ssout_ref = next(it), next(it), next(it), next(it)
    p_scr, pnext_scr, a_scr, b_scr, zs_scr = next(it), next(it), next(it), next(it), next(it)
    kk_scr = next(it) if hgrn else None
    oacc_scr, dcol_scr, S_scr = next(it), next(it), next(it)
    ps_scr, psnext_scr, as_scr, bs_scr, zss_scr = next(it), next(it), next(it), next(it), next(it)
    kks_scr = next(it) if hgrn else None
    oaccs_scr, dcols_scr = next(it), next(it)

    c_idx = pl.program_id(1)
    hs = lax.rem(c_idx, HB)

    @pl.when(c_idx == 0)
    def _():
        S_scr[...] = jnp.zeros_like(S_scr)

    def project(xp_ref, xs_ref, dst, dst_s):
        x = jnp.concatenate([xp_ref[...], xs_ref[...]], axis=0)
        base = 0
        for w_ref in w_refs:
            p = _dot_t(x, w_ref[...])
            n = w_ref.shape[0] // LANES
            for i in range(n):
                dst[base + i] = p[:R, i * LANES:(i + 1) * LANES]
                dst_s[base + i] = p[R:, i * LANES:(i + 1) * LANES]
            base += n

    @pl.when(c_idx == 0)
    def _():
        project(xcur_ref, xs_cur_ref, p_scr, ps_scr)

    def slab_of(head_slot, k):
        if k < 2:
            return k * HB + head_slot
        if k < 2 + nv:
            return 2 * HB + head_slot * nv + (k - 2)
        return (2 + nv) * HB + head_slot * nv + (k - 2 - nv)

    def prompt_out(hh, val):
        o_ref[:, hh * dv:(hh + 1) * dv] = val

    def prompt_S_out(s, hh, val):
        S_scr[s, hh] = val

    def sample_out(hh, val):
        os_ref[...] = val

    def sample_S_out(s, hh, val):
        ssout_ref[s] = val

    lane = lambda hh: slice(hh * LANES, (hh + 1) * LANES)

    def step(p_cur, ps_cur, p_nxt, ps_nxt):
        prompt = SimpleNamespace(
            C=C, nv=nv, hgrn=hgrn, q_scale=q_scale, heads=range(HB), p=p_cur, slab=slab_of,
            g=lambda hh, r: g_ref[r, lane(hh)], lbf=lambda hh: lbf_ref[:, lane(hh)],
            oml=lambda hh: oml_ref[:, lane(hh)], onw=lambda: onw_ref[...],
            a=a_scr, b=b_scr, zs=zs_scr, kk=kk_scr, oacc=oacc_scr, dcol=dcol_scr,
            S_in=lambda s, hh: S_scr[s, hh], S_out=prompt_S_out, out=prompt_out)
        sample = SimpleNamespace(
            C=CS, nv=nv, hgrn=hgrn, q_scale=q_scale, heads=range(1), p=ps_cur,
            slab=lambda hh, k: slab_of(hs, k),
            g=lambda hh, r: gs_ref[r, :], lbf=lambda hh: lbfs_ref[...], oml=lambda hh: omls_ref[...],
            onw=lambda: onw_ref[...],
            a=as_scr, b=bs_scr, zs=zss_scr, kk=kks_scr, oacc=oaccs_scr, dcol=dcols_scr,
            S_in=lambda s, hh: s0_ref[s], S_out=sample_S_out, out=sample_out)

        units = [(un, _recurrence_unit_phases(un)) for un in (prompt, sample)]
        for un, (vector_phase, _, _) in units:
            for hh in un.heads:
                vector_phase(hh)
        project(xnext_ref, xs_next_ref, p_nxt, ps_nxt)
        for un, (_, matmul_phase, _) in units:
            for hh in un.heads:
                matmul_phase(hh)
        for un, (_, _, gate_phase) in units:
            for hh in un.heads:
                gate_phase(hh)

    parity = lax.rem(c_idx, 2)

    @pl.when(parity == 0)
    def _():
        step(p_scr, ps_scr, pnext_scr, psnext_scr)

    @pl.when(parity == 1)
    def _():
        step(pnext_scr, psnext_scr, p_scr, ps_scr)

    @pl.when(c_idx == pl.num_programs(1) - 1)
    def _():
        sout_ref[...] = S_scr[...]


def _recurrence_call(xn, w_c, w_col0, g_gla, s_in, onw, lbf, oml, *, layer, hgrn, C, CS, n_chunks, sample_row0,
                     n_groups, heads_per_step, prev_states=None, prev_sample_states=None):
    NS = SEQ_PER_ROWGROUP
    R, RS = NS * C, NS * CS
    H = HGRN_HEADS if hgrn else GLA_HEADS
    dv = HGRN_DV if hgrn else GLA_DV
    dk = LANES
    nv = dv // LANES
    T, D = xn.shape
    W = (2 + 2 * nv) * LANES
    HB = heads_per_step
    assert H % HB == 0 and n_chunks == n_groups * HB and sample_row0 % RS == 0
    rbs = sample_row0 // RS
    last = n_chunks - 1
    s_grp = lambda c: c // HB
    s_head = lambda h, c: h * HB + c % HB

    in_specs = [pl.BlockSpec((R, D), lambda h, c: (0, 0)),
                pl.BlockSpec((R, D), lambda h, c: (jnp.minimum(c + 1, last), 0)),
                pl.BlockSpec((RS, D), lambda h, c: (rbs, 0)),
                pl.BlockSpec((RS, D), lambda h, c: (rbs + s_grp(jnp.minimum(c + 1, last)), 0))]
    args = [xn, xn, xn, xn]
    col = w_col0
    for width in (LANES, LANES, dv, dv):
        blk = HB * width
        assert col % blk == 0
        in_specs.append(pl.BlockSpec((None, blk, D), lambda h, c, j=col // blk: (layer, j + h, 0)))
        args.append(w_c)
        col += H * width
    if not hgrn:
        in_specs += [pl.BlockSpec((R, HB * LANES), lambda h, c: (c, h)),
                     pl.BlockSpec((RS, LANES), lambda h, c: (rbs + s_grp(c), s_head(h, c)))]
        args += [g_gla, g_gla]
    in_specs.append(pl.BlockSpec((1, dv), lambda h, c: (0, 0)))
    args.append(onw.reshape(1, dv))
    if hgrn:
        in_specs += [pl.BlockSpec((1, HB * LANES), lambda h, c: (0, h)),
                     pl.BlockSpec((1, HB * LANES), lambda h, c: (0, h)),
                     pl.BlockSpec((1, LANES), lambda h, c: (0, s_head(h, c))),
                     pl.BlockSpec((1, LANES), lambda h, c: (0, s_head(h, c)))]
        args += [lbf.reshape(1, -1), oml.reshape(1, -1)] * 2
    ss_map = lambda h, c: (layer, s_grp(c), s_head(h, c), 0, 0)
    in_specs.append(pl.BlockSpec((None, NS, None, dk, dv), ss_map))
    args.append(s_in)
    aliases = {}
    for k, prev in ((1, prev_states), (3, prev_sample_states)):
        if prev is not None:
            in_specs.append(pl.BlockSpec(memory_space=pl.ANY))
            args.append(prev)
            aliases[len(args) - 1] = k

    out_specs = [pl.BlockSpec((R, HB * dv), lambda h, c: (c, h)),
                 pl.BlockSpec((None, NS, HB, dk, dv), lambda h, c: (layer, 0, h, 0, 0)),
                 pl.BlockSpec((RS, dv), lambda h, c: (s_grp(c), s_head(h, c))),
                 pl.BlockSpec((None, NS, None, dk, dv), ss_map)]
    out_shape = [jax.ShapeDtypeStruct((T, H * dv), F32),
                 jax.ShapeDtypeStruct((DEPTH, NS, H, dk, dv), F32),
                 jax.ShapeDtypeStruct((n_groups * RS, H * dv), F32),
                 jax.ShapeDtypeStruct((DEPTH, n_groups * NS, H, dk, dv), F32)]

    def unit_scratch(heads, Cu, prompt_state):
        Ru = NS * Cu
        sc = [pltpu.VMEM((HB * W // LANES, Ru, LANES), F32),
              pltpu.VMEM((HB * W // LANES, Ru, LANES), F32),
              pltpu.VMEM((heads, Cu, NS, LANES), F32),
              pltpu.VMEM((heads, Cu, NS, LANES), F32),
              pltpu.VMEM((heads, len(_gram_levels(Cu)) + 2, Ru, LANES), F32)]
        if hgrn:
            sc.append(pltpu.VMEM((heads, Ru, LANES), F32))
        sc += [pltpu.VMEM((heads, nv, Ru, LANES), F32),
               pltpu.VMEM((heads, NS, LANES, LANES), F32)]
        if prompt_state:
            sc.append(pltpu.VMEM((NS, heads, dk, dv), F32))
        return sc

    kern = functools.partial(_recurrence_kernel, C=C, CS=CS, dv=dv, hgrn=hgrn,
                             q_scale=1.0 if hgrn else GLA_DK ** -0.5, n_aliased=len(aliases), HB=HB)
    return pl.pallas_call(
        kern,
        grid=(H // HB, n_chunks),
        in_specs=in_specs,
        out_specs=out_specs,
        out_shape=out_shape,
        scratch_shapes=unit_scratch(HB, C, True) + unit_scratch(1, CS, False),
        input_output_aliases=aliases,
        compiler_params=_cparams(("parallel", "arbitrary")),
        name="hgrn" if hgrn else "gla",
    )(*args)


def _outproj_kernel(og_ref, oh_ref, xn_ref, x_ref, wga_ref, wgb_ref, wo_ref, npost_ref, npre_ref, h_ref, hn_ref):
    xn = xn_ref[...]
    ga = _dot_t(xn, wga_ref[...])
    gb = _dot_t(xn, wgb_ref[...])
    merged = og_ref[...] * _sigmoid(ga) + oh_ref[...] * _sigmoid(gb)
    m = jnp.dot(merged.astype(BF16), wo_ref[...], preferred_element_type=F32)
    h = x_ref[...] + _rms(m, npost_ref[...])
    h_ref[...] = h
    hn_ref[...] = _rms(h, npre_ref[...]).astype(BF16)


def _outproj_call(og, oh, xn, x, w_c, w_o, npost, npre, *, layer, tm):
    T, D = x.shape
    row = lambda i: (i, 0)
    const = lambda i: (0, 0)
    once = pl.Buffered(1)
    n_blk = w_c.shape[1] // D
    return pl.pallas_call(
        _outproj_kernel,
        grid=(T // tm,),
        in_specs=[pl.BlockSpec((tm, D), row),
                  pl.BlockSpec((tm, D), row),
                  pl.BlockSpec((tm, D), row),
                  pl.BlockSpec((tm, D), row),
                  pl.BlockSpec((None, D, D), lambda i: (layer, n_blk - 2, 0), pipeline_mode=once),
                  pl.BlockSpec((None, D, D), lambda i: (layer, n_blk - 1, 0), pipeline_mode=once),
                  pl.BlockSpec((None, D, D), lambda i: (layer, 0, 0), pipeline_mode=once),
                  pl.BlockSpec((1, D), const),
                  pl.BlockSpec((1, D), const)],
        out_specs=[pl.BlockSpec((tm, D), row), pl.BlockSpec((tm, D), row)],
        out_shape=[jax.ShapeDtypeStruct((T, D), F32), jax.ShapeDtypeStruct((T, D), BF16)],
        compiler_params=_cparams(("parallel",)),
        name="outproj",
    )(og, oh, xn, x, w_c, w_c, w_o, npost.reshape(1, D), npre.reshape(1, D))


def _ffn_kernel(*refs, splits, emit_next):
    hn_ref, h_ref, wa_ref, wu_ref, wf_ref, nfpost_ref = refs[:6]
    if emit_next:
        nnext_ref, wlr_ref, wlr2_ref, blr_ref, y_ref, yn_ref, g_ref = refs[6:]
    else:
        y_ref, rows_scr = refs[6:]
    hn = hn_ref[...]
    f = None
    for lo, hi in splits:
        sl = slice(lo, hi)
        a = jnp.dot(hn, wa_ref[:, sl], preferred_element_type=F32)
        u = jnp.dot(hn, wu_ref[:, sl], preferred_element_type=F32)
        g = (a * _sigmoid(a) * u).astype(BF16)
        part = jnp.dot(g, wf_ref[sl, :], preferred_element_type=F32)
        f = part if f is None else f + part
    y = h_ref[...] + _rms(f, nfpost_ref[...])
    if emit_next:
        y_ref[...] = y
        yn = _rms(y, nnext_ref[...]).astype(BF16)
        yn_ref[...] = yn
        g_ref[...] = _gla_gate(yn, wlr_ref[...], wlr2_ref[...], blr_ref[...])
    else:
        n_seq, n_time, D = y_ref.shape
        for j in range(D // LANES):
            rows_scr[j] = y[:, j * LANES:(j + 1) * LANES]
        for seq in range(n_seq):
            for j in range(D // LANES):
                y_ref[seq, :, j * LANES:(j + 1) * LANES] = rows_scr[j, _tile_rows(seq, n_time), :]


def _ffn_call(hn, h, w_ffn, w_f, nfpost, next_layer, *, layer, tm, final=None):
    T, D = h.shape
    F = w_f.shape[1]
    rb0, n_tiles = 0, T // tm
    if final is not None:
        row0, B, L, n_seq, n_time = final
        assert n_seq * n_time == tm and row0 % tm == 0 and (B == n_seq or L == n_time)
        rb0, n_tiles, nt = row0 // tm, B * L // tm, L // n_time
    row = lambda i: (rb0 + i, 0)
    const = lambda i: (0, 0)
    once = pl.Buffered(1)
    half = (F // MXU_COLS + 1) // 2 * MXU_COLS
    assert F % MXU_COLS == 0 and F % LANES == 0
    in_specs = [pl.BlockSpec((tm, D), row),
                pl.BlockSpec((tm, D), row),
                pl.BlockSpec((None, D, F), lambda i: (layer, 0, 0), pipeline_mode=once),
                pl.BlockSpec((None, D, F), lambda i: (layer, 0, 1), pipeline_mode=once),
                pl.BlockSpec((None, F, D), lambda i: (layer, 0, 0), pipeline_mode=once),
                pl.BlockSpec((1, D), const)]
    args = [hn, h, w_ffn, w_ffn, w_f, nfpost.reshape(1, D)]
    scratch = []
    if next_layer is None:
        out_specs = [pl.BlockSpec((n_seq, n_time, D), lambda i: (i // nt, i % nt, 0))]
        out_shape = [jax.ShapeDtypeStruct((B, L, D), F32)]
        scratch = [pltpu.VMEM((D // LANES, tm, LANES), F32)]
    else:
        out_specs = [pl.BlockSpec((tm, D), row)]
        out_shape = [jax.ShapeDtypeStruct((T, D), F32)]
        nnext, w_lr, w_lr2, b_lr = next_layer
        R, N = w_lr2.shape[1:]
        lnext = lambda i: (layer + 1, 0, 0)
        in_specs += [pl.BlockSpec((1, D), const),
                     pl.BlockSpec((None, R, D), lnext, pipeline_mode=once),
                     pl.BlockSpec((None, R, N), lnext, pipeline_mode=once),
                     pl.BlockSpec((1, N), const)]
        args += [nnext.reshape(1, D), w_lr, w_lr2, b_lr.reshape(1, N)]
        out_specs += [pl.BlockSpec((tm, D), row), pl.BlockSpec((tm, N), row)]
        out_shape += [jax.ShapeDtypeStruct((T, D), BF16), jax.ShapeDtypeStruct((T, N), F32)]
    return pl.pallas_call(
        functools.partial(_ffn_kernel, splits=((0, half), (half, F)), emit_next=next_layer is not None),
        grid=(n_tiles,),
        in_specs=in_specs,
        out_specs=out_specs,
        out_shape=out_shape,
        scratch_shapes=scratch,
        compiler_params=_cparams(("parallel",)),
        name="ffn",
    )(*args)


def kernel(x_prompt, x_sample, state_gla, state_hgrn, norm_mix_pre, norm_mix_post, norm_ffn_pre, norm_ffn_post, w_in, gla_w_lr2, gla_b_lr, gla_onorm, hgrn_lb, hgrn_onorm, w_out, w_ffn_in, w_ffn_out):
    B, L, D = x_prompt.shape
    BS, LS, _ = x_sample.shape
    NS = SEQ_PER_ROWGROUP
    assert B == NS and BS % NS == 0 and L % PROMPT_CHUNK == 0
    TP = B * L
    TS = BS * LS
    T = TP + TS
    assert TP % POST_ROWS == 0 and TS % POST_ROWS == 0

    sm = jax.nn.softmax(hgrn_lb.astype(F32), axis=0)
    lb = jnp.clip(jnp.cumsum(sm, axis=0) - sm[0:1], 0.0, 1.0 - 1e-6)
    lbf = jnp.maximum(lb, LB_FLOOR)
    oml = 1.0 - lb

    c_lr0 = GLA_HEADS * (2 * GLA_DK + 2 * GLA_DV)
    c_lr1 = c_lr0 + GLA_GATE_RANK
    w_t = jnp.swapaxes(w_in, 1, 2).astype(BF16)
    w_g = w_t
    w_h = w_t[:, c_lr1:]
    assert w_h.shape[1] == HGRN_HEADS * (2 * HGRN_EXPAND + 2 * HGRN_DV) + 2 * D
    w_lr = jnp.pad(w_t[:, c_lr0:c_lr1], ((0, 0), (0, LANES - GLA_GATE_RANK), (0, 0)))
    w_lr2 = jnp.pad(gla_w_lr2, ((0, 0), (0, LANES - GLA_GATE_RANK), (0, 0))).astype(BF16)
    w_o = w_out.astype(BF16)
    w_ffn = w_ffn_in.astype(BF16)
    w_f = w_ffn_out.astype(BF16)

    P_TIME = POST_ROWS // NS
    S_SEQ = POST_ROWS // LS
    pre = _prenorm_call(x_prompt, norm_mix_pre[0], w_lr, w_lr2, gla_b_lr[0],
                        n_seq=NS, n_time=P_TIME, row0=0, total_rows=T)
    x, xn, g_gla = _prenorm_call(x_sample, norm_mix_pre[0], w_lr, w_lr2, gla_b_lr[0],
                                 n_seq=S_SEQ, n_time=LS, row0=TP, total_rows=T, prev=pre)

    n_sample_groups = BS // NS
    gla_p = gla_s = hg_p = hg_s = None
    for l in range(DEPTH):
        og, gla_p, og_s, gla_s = _recurrence_call(
            xn, w_g, 0, g_gla, state_gla, gla_onorm[l], None, None, layer=l, hgrn=False,
            C=PROMPT_CHUNK, CS=LS, n_chunks=L // PROMPT_CHUNK, sample_row0=TP, n_groups=n_sample_groups,
            heads_per_step=HEADS_PER_STEP, prev_states=gla_p, prev_sample_states=gla_s)
        oh, hg_p, oh_s, hg_s = _recurrence_call(
            xn, w_h, 0, None, state_hgrn, hgrn_onorm[l], lbf[l], oml[l], layer=l, hgrn=True,
            C=PROMPT_CHUNK, CS=LS, n_chunks=L // PROMPT_CHUNK, sample_row0=TP, n_groups=n_sample_groups,
            heads_per_step=HEADS_PER_STEP, prev_states=hg_p, prev_sample_states=hg_s)
        og = lax.dynamic_update_slice(og, og_s, (TP, 0))
        oh = lax.dynamic_update_slice(oh, oh_s, (TP, 0))

        h, hn = _outproj_call(og, oh, xn, x, w_h, w_o, norm_mix_post[l], norm_ffn_pre[l], layer=l, tm=POST_ROWS)
        if l + 1 < DEPTH:
            nxt = (norm_mix_pre[l + 1], w_lr, w_lr2, gla_b_lr[l + 1])
            x, xn, g_gla = _ffn_call(hn, h, w_ffn, w_f, norm_ffn_post[l], nxt, layer=l, tm=POST_ROWS)
        else:
            (y_p,) = _ffn_call(hn, h, w_ffn, w_f, norm_ffn_post[l], None, layer=l, tm=POST_ROWS,
                               final=(0, B, L, NS, P_TIME))
            (y_s,) = _ffn_call(hn, h, w_ffn, w_f, norm_ffn_post[l], None, layer=l, tm=POST_ROWS,
                               final=(TP, BS, LS, S_SEQ, LS))

    return (y_p, y_s, gla_p, gla_s, hg_p, hg_s)
```

```python
import functools
from types import SimpleNamespace

import jax
import jax.numpy as jnp
from jax import lax
from jax.experimental import pallas as pl
from jax.experimental.pallas import tpu as pltpu

F32 = jnp.float32
BF16 = jnp.bfloat16

DEPTH = 4
GLA_HEADS = 4
GLA_DK = 128
GLA_DV = 256
GLA_GATE_RANK = 16
GLA_TAU = 16.0
HGRN_HEADS = 8
HGRN_EXPAND = 128
HGRN_DV = 128
LB_FLOOR = 1e-20
EPS = 1e-6
LOG2E = 1.4426950408889634

SUBLANES = 8
LANES = 128
MXU_COLS = 256
SEQ_PER_ROWGROUP = SUBLANES

PROMPT_CHUNK = 64
HEADS_PER_STEP = 2
FINE_BLOCK = SUBLANES
POST_ROWS = 512
VMEM_LIMIT = 52 * 1024 * 1024


def _cparams(sem):
    return pltpu.CompilerParams(dimension_semantics=sem, vmem_limit_bytes=VMEM_LIMIT)


def _rms(x, w):
    return x * lax.rsqrt(jnp.mean(x * x, axis=-1, keepdims=True) + EPS) * w


def _sigmoid(x):
    return 0.5 + 0.5 * jnp.tanh(0.5 * x)


def _dot_t(x, wt):
    return lax.dot_general(x, wt, (((1,), (1,)), ((), ())), preferred_element_type=F32)


def _gla_gate(xn, wlr_t, wlr2, blr):
    glr = _dot_t(xn, wlr_t)
    y = jnp.dot(glr.astype(BF16), wlr2, preferred_element_type=F32) + blr
    ls = -(jnp.maximum(-y, 0.0) + jnp.log1p(jnp.exp(-jnp.abs(y))))
    return ls * (1.0 / GLA_TAU)


def _tile_rows(seq, n_time):
    NS = SEQ_PER_ROWGROUP
    return pl.ds((seq // NS) * n_time * NS + seq % NS, n_time, stride=NS)


def _prenorm_kernel(*refs, n_aliased):
    x_ref, w_ref, wlr_ref, wlr2_ref, blr_ref = refs[:5]
    xr_ref, xn_ref, g_ref, rows_scr = refs[5 + n_aliased:]
    n_seq, n_time, D = x_ref.shape
    for seq in range(n_seq):
        for j in range(D // LANES):
            rows_scr[j, _tile_rows(seq, n_time), :] = x_ref[seq, :, j * LANES:(j + 1) * LANES]
    x = jnp.concatenate([rows_scr[j] for j in range(D // LANES)], axis=1)
    xr_ref[...] = x
    xn = _rms(x, w_ref[...]).astype(BF16)
    xn_ref[...] = xn
    g_ref[...] = _gla_gate(xn, wlr_ref[...], wlr2_ref[...], blr_ref[...])


def _prenorm_call(x3, w, w_lr, w_lr2, b_lr, *, n_seq, n_time, row0, total_rows, prev=None):
    B, L, D = x3.shape
    R, N = w_lr2.shape[1:]
    assert w_lr.shape[1:] == (R, D)
    tm = n_seq * n_time
    assert B % n_seq == 0 and L % n_time == 0 and (B == n_seq or L == n_time) and row0 % tm == 0
    nb, nt = B // n_seq, L // n_time
    rb0 = row0 // tm
    row = lambda i: (rb0 + i, 0)
    const = lambda i: (0, 0)
    l0 = lambda i: (0, 0, 0)
    in_specs = [pl.BlockSpec((n_seq, n_time, D), lambda i: (i // nt, i % nt, 0)),
                pl.BlockSpec((1, D), const),
                pl.BlockSpec((None, R, D), l0),
                pl.BlockSpec((None, R, N), l0),
                pl.BlockSpec((1, N), const)]
    args = [x3, w.reshape(1, D), w_lr, w_lr2, b_lr.reshape(1, N)]
    aliases = {}
    if prev is not None:
        for k, p in enumerate(prev):
            in_specs.append(pl.BlockSpec(memory_space=pl.ANY))
            args.append(p)
            aliases[len(args) - 1] = k
    return pl.pallas_call(
        functools.partial(_prenorm_kernel, n_aliased=len(aliases)),
        grid=(nb * nt,),
        in_specs=in_specs,
        out_specs=[pl.BlockSpec((tm, D), row), pl.BlockSpec((tm, D), row), pl.BlockSpec((tm, N), row)],
        out_shape=[jax.ShapeDtypeStruct((total_rows, D), F32), jax.ShapeDtypeStruct((total_rows, D), BF16),
                   jax.ShapeDtypeStruct((total_rows, N), F32)],
        scratch_shapes=[pltpu.VMEM((D // LANES, tm, LANES), F32)],
        input_output_aliases=aliases,
        compiler_params=_cparams(("parallel",)),
        name="prenorm",
    )(*args)


def _gram_levels(C):
    out = []
    m = C // 2
    while m >= FINE_BLOCK:
        out.append(m)
        m //= 2
    return out


def _recurrence_unit_phases(u):
    C, nv = u.C, u.nv
    NS = SEQ_PER_ROWGROUP
    FINE = min(C, FINE_BLOCK)
    levels = _gram_levels(C)
    NL = len(levels)

    def rows(t):
        return pl.ds(t * NS, NS)

    def seq_rows(s):
        return pl.ds(s, C, stride=NS)

    def vector_phase(hh):
        acc = jnp.zeros((NS, LANES), F32)
        for t in range(C):
            if u.hgrn:
                z = u.p[u.slab(hh, 1), rows(t), :]
                e = jnp.exp(-jnp.abs(z))
                r = 1.0 / (1.0 + e)
                er = e * r
                pos = z >= 0.0
                sig_p = jnp.where(pos, r, er)
                sig_n = jnp.where(pos, er, r)
                oml = u.oml(hh)
                g_t = jnp.log(u.lbf(hh) + oml * sig_p)
                u.kk[hh, rows(t), :] = oml * sig_n
            else:
                g_t = u.g(hh, rows(t))
            acc = acc + g_t
            u.a[hh, t] = jnp.exp2(g_t * LOG2E)
            u.b[hh, t] = acc * LOG2E
        b_last = acc * LOG2E
        d_last = jnp.exp2(b_last)

        d_t = jnp.transpose(jnp.concatenate([d_last] * (LANES // NS), axis=0))
        for s in range(NS):
            u.dcol[hh, s] = jnp.broadcast_to(d_t[:, s:s + 1], (LANES, LANES))

        def q_rows(t):
            q_t = u.p[u.slab(hh, 0), rows(t), :]
            return q_t if u.q_scale == 1.0 else q_t * u.q_scale

        def k_rows(t):
            return u.kk[hh, rows(t), :] if u.hgrn else u.p[u.slab(hh, 1), rows(t), :]

        for t in range(C):
            q_t, k_t, b_t = q_rows(t), k_rows(t), u.b[hh, t]
            u.zs[hh, NL, rows(t), :] = q_t * jnp.exp2(b_t)
            u.zs[hh, NL + 1, rows(t), :] = k_t * jnp.exp2(b_last - b_t)
            for j, m in enumerate(levels):
                r_idx = (t // (2 * m)) * (2 * m) + m - 1
                if (t % (2 * m)) >= m:
                    u.zs[hh, j, rows(t), :] = q_t * jnp.exp2(b_t - u.b[hh, r_idx])
                else:
                    u.zs[hh, j, rows(t), :] = k_t * jnp.exp2(u.b[hh, r_idx] - b_t)

        for t0 in range(0, C, FINE):
            k_blk, v_blk, decay = {}, {}, {}
            for t in range(t0, t0 + FINE):
                q_t = q_rows(t)
                k_blk[t] = k_rows(t)
                v_blk[t] = [u.p[u.slab(hh, 2 + i), rows(t), :] for i in range(nv)]
                a_t = u.a[hh, t]
                for s in range(t0, t):
                    decay[s] = a_t if s == t - 1 else decay[s] * a_t
                w = jnp.sum(q_t * k_blk[t], axis=-1, keepdims=True)
                o_t = [w * v_blk[t][i] for i in range(nv)]
                for s in range(t0, t):
                    w = jnp.sum(q_t * (k_blk[s] * decay[s]), axis=-1, keepdims=True)
                    o_t = [o_t[i] + w * v_blk[s][i] for i in range(nv)]
                for i in range(nv):
                    u.oacc[hh, i, rows(t), :] = o_t[i]

    ti = lax.broadcasted_iota(jnp.int32, (C, C), 0)
    si = lax.broadcasted_iota(jnp.int32, (C, C), 1)
    xr = jnp.bitwise_xor(ti, si)
    lower = ti > si
    masks = [lower & (xr >= m) & (xr < 2 * m) for m in levels]

    def matmul_phase(hh):
        def load_v(s):
            vs = [u.p[u.slab(hh, 2 + i), seq_rows(s), :] for i in range(nv)]
            vb = vs[0] if nv == 1 else jnp.concatenate(vs, axis=1)
            return vb.astype(BF16)

        scores = []
        for s in range(NS if NL else 0):
            sc = jnp.zeros((C, C), F32)
            for j in range(NL):
                z = u.zs[hh, j, seq_rows(s), :].astype(BF16)
                gram = lax.dot_general(z, z, (((1,), (1,)), ((), ())), preferred_element_type=F32)
                sc = jnp.where(masks[j], gram, sc)
            scores.append(sc.astype(BF16))
        one_dot = NL > 0 and nv > 1
        inter = []
        for s in range(NS):
            zq = u.zs[hh, NL, seq_rows(s), :].astype(BF16)
            S = u.S_in(s, hh)
            o = jnp.dot(zq, S.astype(BF16), preferred_element_type=F32)
            dc = u.dcol[hh, s]
            if nv > 1:
                dc = jnp.concatenate([dc] * nv, axis=1)
            if one_dot:
                inter.append((o, dc * S))
            else:
                zk = u.zs[hh, NL + 1, seq_rows(s), :].astype(BF16)
                upd = lax.dot_general(zk, load_v(s), (((0,), (0,)), ((), ())), preferred_element_type=F32)
                u.S_out(s, hh, dc * S + upd)
                inter.append((o, None))
        for s in range(NS):
            o, decayed = inter[s]
            if one_dot:
                zk_t = jnp.transpose(u.zs[hh, NL + 1, seq_rows(s), :]).astype(BF16)
                res = jnp.dot(jnp.concatenate([zk_t, scores[s]], axis=0), load_v(s), preferred_element_type=F32)
                u.S_out(s, hh, decayed + res[:LANES])
                o = o + res[LANES:]
            elif NL:
                o = o + jnp.dot(scores[s], load_v(s), preferred_element_type=F32)
            for i in range(nv):
                sl = slice(i * LANES, (i + 1) * LANES)
                u.oacc[hh, i, seq_rows(s), :] = u.oacc[hh, i, seq_rows(s), :] + o[:, sl]

    def gate_phase(hh):
        if nv == 1:
            o_all = u.oacc[hh, 0]
        else:
            o_all = jnp.concatenate([u.oacc[hh, i] for i in range(nv)], axis=1)
        y = o_all * lax.rsqrt(jnp.mean(o_all * o_all, axis=-1, keepdims=True) + EPS) * u.onw()
        gs = [u.p[u.slab(hh, 2 + nv + i)] for i in range(nv)]
        gt = gs[0] if nv == 1 else jnp.concatenate(gs, axis=1)
        u.out(hh, y * (gt * _sigmoid(gt)))

    return vector_phase, matmul_phase, gate_phase


def _recurrence_kernel(*refs, C, CS, dv, hgrn, q_scale, n_aliased, HB):
    nv = dv // LANES
    NS = SEQ_PER_ROWGROUP
    R = NS * C
    it = iter(refs)
    xcur_ref, xnext_ref, xs_cur_ref, xs_next_ref = next(it), next(it), next(it), next(it)
    w_refs = [next(it) for _ in range(4)]
    g_ref, gs_ref = (None, None) if hgrn else (next(it), next(it))
    onw_ref = next(it)
    lbf_ref, oml_ref, lbfs_ref, omls_ref = (next(it), next(it), next(it), next(it)) if hgrn else (None,) * 4
    s0_ref = next(it)
    for _ in range(n_aliased):
        next(it)
    o_ref, sout_ref, os_ref, ssout_ref = next(it), next(it), next(it), next(it)
    p_scr, pnext_scr, a_scr, b_scr, zs_scr = next(it), next(it), next(it), next(it), next(it)
    kk_scr = next(it) if hgrn else None
    oacc_scr, dcol_scr, S_scr = next(it), next(it), next(it)
    ps_scr, psnext_scr, as_scr, bs_scr, zss_scr = next(it), next(it), next(it), next(it), next(it)
    kks_scr = next(it) if hgrn else None
    oaccs_scr, dcols_scr = next(it), next(it)

    c_idx = pl.program_id(1)
    hs = lax.rem(c_idx, HB)

    @pl.when(c_idx == 0)
    def _():
        S_scr[...] = jnp.zeros_like(S_scr)

    def project(xp_ref, xs_ref):
        x = jnp.concatenate([xp_ref[...], xs_ref[...]], axis=0)
        base = 0
        for w_ref in w_refs:
            p = _dot_t(x, w_ref[...])
            n = w_ref.shape[0] // LANES
            for i in range(n):
                pnext_scr[base + i] = p[:R, i * LANES:(i + 1) * LANES]
                psnext_scr[base + i] = p[R:, i * LANES:(i + 1) * LANES]
            base += n

    @pl.when(c_idx == 0)
    def _():
        project(xcur_ref, xs_cur_ref)

    p_scr[...] = pnext_scr[...]
    ps_scr[...] = psnext_scr[...]

    def slab_of(head_slot, k):
        if k < 2:
            return k * HB + head_slot
        if k < 2 + nv:
            return 2 * HB + head_slot * nv + (k - 2)
        return (2 + nv) * HB + head_slot * nv + (k - 2 - nv)

    def prompt_out(hh, val):
        o_ref[:, hh * dv:(hh + 1) * dv] = val

    def prompt_S_out(s, hh, val):
        S_scr[s, hh] = val

    def sample_out(hh, val):
        os_ref[...] = val

    def sample_S_out(s, hh, val):
        ssout_ref[s] = val

    lane = lambda hh: slice(hh * LANES, (hh + 1) * LANES)
    prompt = SimpleNamespace(
        C=C, nv=nv, hgrn=hgrn, q_scale=q_scale, heads=range(HB), p=p_scr, slab=slab_of,
        g=lambda hh, r: g_ref[r, lane(hh)], lbf=lambda hh: lbf_ref[:, lane(hh)],
        oml=lambda hh: oml_ref[:, lane(hh)], onw=lambda: onw_ref[...],
        a=a_scr, b=b_scr, zs=zs_scr, kk=kk_scr, oacc=oacc_scr, dcol=dcol_scr,
        S_in=lambda s, hh: S_scr[s, hh], S_out=prompt_S_out, out=prompt_out)
    sample = SimpleNamespace(
        C=CS, nv=nv, hgrn=hgrn, q_scale=q_scale, heads=range(1), p=ps_scr,
        slab=lambda hh, k: slab_of(hs, k),
        g=lambda hh, r: gs_ref[r, :], lbf=lambda hh: lbfs_ref[...], oml=lambda hh: omls_ref[...],
        onw=lambda: onw_ref[...],
        a=as_scr, b=bs_scr, zs=zss_scr, kk=kks_scr, oacc=oaccs_scr, dcol=dcols_scr,
        S_in=lambda s, hh: s0_ref[s], S_out=sample_S_out, out=sample_out)

    units = [(un, _recurrence_unit_phases(un)) for un in (prompt, sample)]
    for un, (vector_phase, _, _) in units:
        for hh in un.heads:
            vector_phase(hh)
    project(xnext_ref, xs_next_ref)
    for un, (_, matmul_phase, _) in units:
        for hh in un.heads:
            matmul_phase(hh)
    for un, (_, _, gate_phase) in units:
        for hh in un.heads:
            gate_phase(hh)

    @pl.when(c_idx == pl.num_programs(1) - 1)
    def _():
        sout_ref[...] = S_scr[...]


def _recurrence_call(xn, w_c, w_col0, g_gla, s_in, onw, lbf, oml, *, layer, hgrn, C, CS, n_chunks, sample_row0,
                     n_groups, heads_per_step, prev_states=None, prev_sample_states=None):
    NS = SEQ_PER_ROWGROUP
    R, RS = NS * C, NS * CS
    H = HGRN_HEADS if hgrn else GLA_HEADS
    dv = HGRN_DV if hgrn else GLA_DV
    dk = LANES
    nv = dv // LANES
    T, D = xn.shape
    W = (2 + 2 * nv) * LANES
    HB = heads_per_step
    assert H % HB == 0 and n_chunks == n_groups * HB and sample_row0 % RS == 0
    rbs = sample_row0 // RS
    last = n_chunks - 1
    s_grp = lambda c: c // HB
    s_head = lambda h, c: h * HB + c % HB

    in_specs = [pl.BlockSpec((R, D), lambda h, c: (0, 0)),
                pl.BlockSpec((R, D), lambda h, c: (jnp.minimum(c + 1, last), 0)),
                pl.BlockSpec((RS, D), lambda h, c: (rbs, 0)),
                pl.BlockSpec((RS, D), lambda h, c: (rbs + s_grp(jnp.minimum(c + 1, last)), 0))]
    args = [xn, xn, xn, xn]
    col = w_col0
    for width in (LANES, LANES, dv, dv):
        blk = HB * width
        assert col % blk == 0
        in_specs.append(pl.BlockSpec((None, blk, D), lambda h, c, j=col // blk: (layer, j + h, 0)))
        args.append(w_c)
        col += H * width
    if not hgrn:
        in_specs += [pl.BlockSpec((R, HB * LANES), lambda h, c: (c, h)),
                     pl.BlockSpec((RS, LANES), lambda h, c: (rbs + s_grp(c), s_head(h, c)))]
        args += [g_gla, g_gla]
    in_specs.append(pl.BlockSpec((1, dv), lambda h, c: (0, 0)))
    args.append(onw.reshape(1, dv))
    if hgrn:
        in_specs += [pl.BlockSpec((1, HB * LANES), lambda h, c: (0, h)),
                     pl.BlockSpec((1, HB * LANES), lambda h, c: (0, h)),
                     pl.BlockSpec((1, LANES), lambda h, c: (0, s_head(h, c))),
                     pl.BlockSpec((1, LANES), lambda h, c: (0, s_head(h, c)))]
        args += [lbf.reshape(1, -1), oml.reshape(1, -1)] * 2
    ss_map = lambda h, c: (layer, s_grp(c), s_head(h, c), 0, 0)
    in_specs.append(pl.BlockSpec((None, NS, None, dk, dv), ss_map))
    args.append(s_in)
    aliases = {}
    for k, prev in ((1, prev_states), (3, prev_sample_states)):
        if prev is not None:
            in_specs.append(pl.BlockSpec(memory_space=pl.ANY))
            args.append(prev)
            aliases[len(args) - 1] = k

    out_specs = [pl.BlockSpec((R, HB * dv), lambda h, c: (c, h)),
                 pl.BlockSpec((None, NS, HB, dk, dv), lambda h, c: (layer, 0, h, 0, 0)),
                 pl.BlockSpec((RS, dv), lambda h, c: (s_grp(c), s_head(h, c))),
                 pl.BlockSpec((None, NS, None, dk, dv), ss_map)]
    out_shape = [jax.ShapeDtypeStruct((n_chunks * R, H * dv), F32),
                 jax.ShapeDtypeStruct((DEPTH, NS, H, dk, dv), F32),
                 jax.ShapeDtypeStruct((n_groups * RS, H * dv), F32),
                 jax.ShapeDtypeStruct((DEPTH, n_groups * NS, H, dk, dv), F32)]

    def unit_scratch(heads, Cu, prompt_state):
        Ru = NS * Cu
        sc = [pltpu.VMEM((HB * W // LANES, Ru, LANES), F32),
              pltpu.VMEM((HB * W // LANES, Ru, LANES), F32),
              pltpu.VMEM((heads, Cu, NS, LANES), F32),
              pltpu.VMEM((heads, Cu, NS, LANES), F32),
              pltpu.VMEM((heads, len(_gram_levels(Cu)) + 2, Ru, LANES), F32)]
        if hgrn:
            sc.append(pltpu.VMEM((heads, Ru, LANES), F32))
        sc += [pltpu.VMEM((heads, nv, Ru, LANES), F32),
               pltpu.VMEM((heads, NS, LANES, LANES), F32)]
        if prompt_state:
            sc.append(pltpu.VMEM((NS, heads, dk, dv), F32))
        return sc

    kern = functools.partial(_recurrence_kernel, C=C, CS=CS, dv=dv, hgrn=hgrn,
                             q_scale=1.0 if hgrn else GLA_DK ** -0.5, n_aliased=len(aliases), HB=HB)
    return pl.pallas_call(
        kern,
        grid=(H // HB, n_chunks),
        in_specs=in_specs,
        out_specs=out_specs,
        out_shape=out_shape,
        scratch_shapes=unit_scratch(HB, C, True) + unit_scratch(1, CS, False),
        input_output_aliases=aliases,
        compiler_params=_cparams(("parallel", "arbitrary")),
        name="hgrn" if hgrn else "gla",
    )(*args)


def _outproj_kernel(og_ref, ogs_ref, oh_ref, ohs_ref, xn_ref, x_ref, wga_ref, wgb_ref, wo_ref, npost_ref, npre_ref,
                    h_ref, hn_ref, *, n_prompt_tiles):
    xn = xn_ref[...]
    ga = _dot_t(xn, wga_ref[...])
    gb = _dot_t(xn, wgb_ref[...])
    is_sample = pl.program_id(0) >= n_prompt_tiles
    og = jnp.where(is_sample, ogs_ref[...], og_ref[...])
    oh = jnp.where(is_sample, ohs_ref[...], oh_ref[...])
    merged = og * _sigmoid(ga) + oh * _sigmoid(gb)
    m = jnp.dot(merged.astype(BF16), wo_ref[...], preferred_element_type=F32)
    h = x_ref[...] + _rms(m, npost_ref[...])
    h_ref[...] = h
    hn_ref[...] = _rms(h, npre_ref[...]).astype(BF16)


def _outproj_call(og, og_s, oh, oh_s, xn, x, w_c, w_o, npost, npre, *, layer, tm):
    T, D = x.shape
    n_p = og.shape[0] // tm
    assert og.shape[0] % tm == 0 and og_s.shape[0] % tm == 0 and og.shape[0] + og_s.shape[0] == T
    row = lambda i: (i, 0)
    prow = lambda i: (jnp.minimum(i, n_p - 1), 0)
    srow = lambda i: (jnp.maximum(i - n_p, 0), 0)
    const = lambda i: (0, 0)
    once = pl.Buffered(1)
    n_blk = w_c.shape[1] // D
    return pl.pallas_call(
        functools.partial(_outproj_kernel, n_prompt_tiles=n_p),
        grid=(T // tm,),
        in_specs=[pl.BlockSpec((tm, D), prow),
                  pl.BlockSpec((tm, D), srow),
                  pl.BlockSpec((tm, D), prow),
                  pl.BlockSpec((tm, D), srow),
                  pl.BlockSpec((tm, D), row),
                  pl.BlockSpec((tm, D), row),
                  pl.BlockSpec((None, D, D), lambda i: (layer, n_blk - 2, 0), pipeline_mode=once),
                  pl.BlockSpec((None, D, D), lambda i: (layer, n_blk - 1, 0), pipeline_mode=once),
                  pl.BlockSpec((None, D, D), lambda i: (layer, 0, 0), pipeline_mode=once),
                  pl.BlockSpec((1, D), const),
                  pl.BlockSpec((1, D), const)],
        out_specs=[pl.BlockSpec((tm, D), row), pl.BlockSpec((tm, D), row)],
        out_shape=[jax.ShapeDtypeStruct((T, D), F32), jax.ShapeDtypeStruct((T, D), BF16)],
        compiler_params=_cparams(("parallel",)),
        name="outproj",
    )(og, og_s, oh, oh_s, xn, x, w_c, w_c, w_o, npost.reshape(1, D), npre.reshape(1, D))


def _ffn_kernel(*refs, splits, emit_next):
    hn_ref, h_ref, wa_ref, wu_ref, wf_ref, nfpost_ref = refs[:6]
    if emit_next:
        nnext_ref, wlr_ref, wlr2_ref, blr_ref, y_ref, yn_ref, g_ref = refs[6:]
    else:
        y_ref, rows_scr = refs[6:]
    hn = hn_ref[...]
    f = None
    for lo, hi in splits:
        sl = slice(lo, hi)
        a = jnp.dot(hn, wa_ref[:, sl], preferred_element_type=F32)
        u = jnp.dot(hn, wu_ref[:, sl], preferred_element_type=F32)
        g = (a * _sigmoid(a) * u).astype(BF16)
        part = jnp.dot(g, wf_ref[sl, :], preferred_element_type=F32)
        f = part if f is None else f + part
    y = h_ref[...] + _rms(f, nfpost_ref[...])
    if emit_next:
        y_ref[...] = y
        yn = _rms(y, nnext_ref[...]).astype(BF16)
        yn_ref[...] = yn
        g_ref[...] = _gla_gate(yn, wlr_ref[...], wlr2_ref[...], blr_ref[...])
    else:
        n_seq, n_time, D = y_ref.shape
        for j in range(D // LANES):
            rows_scr[j] = y[:, j * LANES:(j + 1) * LANES]
        for seq in range(n_seq):
            for j in range(D // LANES):
                y_ref[seq, :, j * LANES:(j + 1) * LANES] = rows_scr[j, _tile_rows(seq, n_time), :]


def _ffn_call(hn, h, w_ffn, w_f, nfpost, next_layer, *, layer, tm, final=None):
    T, D = h.shape
    F = w_f.shape[1]
    rb0, n_tiles = 0, T // tm
    if final is not None:
        row0, B, L, n_seq, n_time = final
        assert n_seq * n_time == tm and row0 % tm == 0 and (B == n_seq or L == n_time)
        rb0, n_tiles, nt = row0 // tm, B * L // tm, L // n_time
    row = lambda i: (rb0 + i, 0)
    const = lambda i: (0, 0)
    once = pl.Buffered(1)
    half = (F // MXU_COLS + 1) // 2 * MXU_COLS
    assert F % MXU_COLS == 0 and F % LANES == 0
    in_specs = [pl.BlockSpec((tm, D), row),
                pl.BlockSpec((tm, D), row),
                pl.BlockSpec((None, D, F), lambda i: (layer, 0, 0), pipeline_mode=once),
                pl.BlockSpec((None, D, F), lambda i: (layer, 0, 1), pipeline_mode=once),
                pl.BlockSpec((None, F, D), lambda i: (layer, 0, 0), pipeline_mode=once),
                pl.BlockSpec((1, D), const)]
    args = [hn, h, w_ffn, w_ffn, w_f, nfpost.reshape(1, D)]
    scratch = []
    if next_layer is None:
        out_specs = [pl.BlockSpec((n_seq, n_time, D), lambda i: (i // nt, i % nt, 0))]
        out_shape = [jax.ShapeDtypeStruct((B, L, D), F32)]
        scratch = [pltpu.VMEM((D // LANES, tm, LANES), F32)]
    else:
        out_specs = [pl.BlockSpec((tm, D), row)]
        out_shape = [jax.ShapeDtypeStruct((T, D), F32)]
        nnext, w_lr, w_lr2, b_lr = next_layer
        R, N = w_lr2.shape[1:]
        lnext = lambda i: (layer + 1, 0, 0)
        in_specs += [pl.BlockSpec((1, D), const),
                     pl.BlockSpec((None, R, D), lnext, pipeline_mode=once),
                     pl.BlockSpec((None, R, N), lnext, pipeline_mode=once),
                     pl.BlockSpec((1, N), const)]
        args += [nnext.reshape(1, D), w_lr, w_lr2, b_lr.reshape(1, N)]
        out_specs += [pl.BlockSpec((tm, D), row), pl.BlockSpec((tm, N), row)]
        out_shape += [jax.ShapeDtypeStruct((T, D), BF16), jax.ShapeDtypeStruct((T, N), F32)]
    return pl.pallas_call(
        functools.partial(_ffn_kernel, splits=((0, half), (half, F)), emit_next=next_layer is not None),
        grid=(n_tiles,),
        in_specs=in_specs,
        out_specs=out_specs,
        out_shape=out_shape,
        scratch_shapes=scratch,
        compiler_params=_cparams(("parallel",)),
        name="ffn",
    )(*args)


def kernel(x_prompt, x_sample, state_gla, state_hgrn, norm_mix_pre, norm_mix_post, norm_ffn_pre, norm_ffn_post, w_in, gla_w_lr2, gla_b_lr, gla_onorm, hgrn_lb, hgrn_onorm, w_out, w_ffn_in, w_ffn_out):
    B, L, D = x_prompt.shape
    BS, LS, _ = x_sample.shape
    NS = SEQ_PER_ROWGROUP
    assert B == NS and BS % NS == 0 and L % PROMPT_CHUNK == 0
    TP = B * L
    TS = BS * LS
    T = TP + TS
    assert TP % POST_ROWS == 0 and TS % POST_ROWS == 0

    sm = jax.nn.softmax(hgrn_lb.astype(F32), axis=0)
    lb = jnp.clip(jnp.cumsum(sm, axis=0) - sm[0:1], 0.0, 1.0 - 1e-6)
    lbf = jnp.maximum(lb, LB_FLOOR)
    oml = 1.0 - lb

    c_lr0 = GLA_HEADS * (2 * GLA_DK + 2 * GLA_DV)
    c_lr1 = c_lr0 + GLA_GATE_RANK
    w_t = jnp.swapaxes(w_in, 1, 2).astype(BF16)
    w_g = w_t
    w_h = w_t[:, c_lr1:]
    assert w_h.shape[1] == HGRN_HEADS * (2 * HGRN_EXPAND + 2 * HGRN_DV) + 2 * D
    w_lr = jnp.pad(w_t[:, c_lr0:c_lr1], ((0, 0), (0, LANES - GLA_GATE_RANK), (0, 0)))
    w_lr2 = jnp.pad(gla_w_lr2, ((0, 0), (0, LANES - GLA_GATE_RANK), (0, 0))).astype(BF16)
    w_o = w_out.astype(BF16)
    w_ffn = w_ffn_in.astype(BF16)
    w_f = w_ffn_out.astype(BF16)

    P_TIME = POST_ROWS // NS
    S_SEQ = POST_ROWS // LS
    pre = _prenorm_call(x_prompt, norm_mix_pre[0], w_lr, w_lr2, gla_b_lr[0],
                        n_seq=NS, n_time=P_TIME, row0=0, total_rows=T)
    x, xn, g_gla = _prenorm_call(x_sample, norm_mix_pre[0], w_lr, w_lr2, gla_b_lr[0],
                                 n_seq=S_SEQ, n_time=LS, row0=TP, total_rows=T, prev=pre)

    n_sample_groups = BS // NS
    gla_p = gla_s = hg_p = hg_s = None
    for l in range(DEPTH):
        og, gla_p, og_s, gla_s = _recurrence_call(
            xn, w_g, 0, g_gla, state_gla, gla_onorm[l], None, None, layer=l, hgrn=False,
            C=PROMPT_CHUNK, CS=LS, n_chunks=L // PROMPT_CHUNK, sample_row0=TP, n_groups=n_sample_groups,
            heads_per_step=HEADS_PER_STEP, prev_states=gla_p, prev_sample_states=gla_s)
        oh, hg_p, oh_s, hg_s = _recurrence_call(
            xn, w_h, 0, None, state_hgrn, hgrn_onorm[l], lbf[l], oml[l], layer=l, hgrn=True,
            C=PROMPT_CHUNK, CS=LS, n_chunks=L // PROMPT_CHUNK, sample_row0=TP, n_groups=n_sample_groups,
            heads_per_step=HEADS_PER_STEP, prev_states=hg_p, prev_sample_states=hg_s)
        h, hn = _outproj_call(og, og_s, oh, oh_s, xn, x, w_h, w_o, norm_mix_post[l], norm_ffn_pre[l], layer=l, tm=POST_ROWS)
        if l + 1 < DEPTH:
            nxt = (norm_mix_pre[l + 1], w_lr, w_lr2, gla_b_lr[l + 1])
            x, xn, g_gla = _ffn_call(hn, h, w_ffn, w_f, norm_ffn_post[l], nxt, layer=l, tm=POST_ROWS)
        else:
            (y_p,) = _ffn_call(hn, h, w_ffn, w_f, norm_ffn_post[l], None, layer=l, tm=POST_ROWS,
                               final=(0, B, L, NS, P_TIME))
            (y_s,) = _ffn_call(hn, h, w_ffn, w_f, norm_ffn_post[l], None, layer=l, tm=POST_ROWS,
                               final=(TP, BS, LS, S_SEQ, LS))

    return (y_p, y_s, gla_p, gla_s, hg_p, hg_s)
```

```python
import functools
from types import SimpleNamespace

import jax
import jax.numpy as jnp
from jax import lax
from jax.experimental import pallas as pl
from jax.experimental.pallas import tpu as pltpu

F32 = jnp.float32
BF16 = jnp.bfloat16

DEPTH = 4
GLA_HEADS = 4
GLA_DK = 128
GLA_DV = 256
GLA_GATE_RANK = 16
GLA_TAU = 16.0
HGRN_HEADS = 8
HGRN_EXPAND = 128
HGRN_DV = 128
LB_FLOOR = 1e-20
EPS = 1e-6
LOG2E = 1.4426950408889634

SUBLANES = 8
LANES = 128
MXU_COLS = 256
SEQ_PER_ROWGROUP = SUBLANES

PROMPT_CHUNK = 64
HEADS_PER_STEP = 2
FINE_BLOCK = SUBLANES
POST_ROWS = 512
VMEM_LIMIT = 52 * 1024 * 1024


def _cparams(sem):
    return pltpu.CompilerParams(dimension_semantics=sem, vmem_limit_bytes=VMEM_LIMIT)


def _rms(x, w):
    return x * lax.rsqrt(jnp.mean(x * x, axis=-1, keepdims=True) + EPS) * w


def _sigmoid(x):
    return 0.5 + 0.5 * jnp.tanh(0.5 * x)


def _dot_t(x, wt):
    return lax.dot_general(x, wt, (((1,), (1,)), ((), ())), preferred_element_type=F32)


def _gla_gate(xn, wlr_t, wlr2, blr):
    glr = _dot_t(xn, wlr_t)
    y = jnp.dot(glr.astype(BF16), wlr2, preferred_element_type=F32) + blr
    ls = -(jnp.maximum(-y, 0.0) + jnp.log1p(jnp.exp(-jnp.abs(y))))
    return ls * (1.0 / GLA_TAU)


def _tile_rows(seq, n_time):
    NS = SEQ_PER_ROWGROUP
    return pl.ds((seq // NS) * n_time * NS + seq % NS, n_time, stride=NS)


def _prenorm_kernel(*refs, n_aliased):
    x_ref, w_ref, wlr_ref, wlr2_ref, blr_ref = refs[:5]
    xr_ref, xn_ref, g_ref, rows_scr = refs[5 + n_aliased:]
    n_seq, n_time, D = x_ref.shape
    for seq in range(n_seq):
        for j in range(D // LANES):
            rows_scr[j, _tile_rows(seq, n_time), :] = x_ref[seq, :, j * LANES:(j + 1) * LANES]
    x = jnp.concatenate([rows_scr[j] for j in range(D // LANES)], axis=1)
    xr_ref[...] = x
    xn = _rms(x, w_ref[...]).astype(BF16)
    xn_ref[...] = xn
    g_ref[...] = _gla_gate(xn, wlr_ref[...], wlr2_ref[...], blr_ref[...])


def _prenorm_call(x3, w, w_lr, w_lr2, b_lr, *, n_seq, n_time, row0, total_rows, prev=None):
    B, L, D = x3.shape
    R, N = w_lr2.shape[1:]
    assert w_lr.shape[1:] == (R, D)
    tm = n_seq * n_time
    assert B % n_seq == 0 and L % n_time == 0 and (B == n_seq or L == n_time) and row0 % tm == 0
    nb, nt = B // n_seq, L // n_time
    rb0 = row0 // tm
    row = lambda i: (rb0 + i, 0)
    const = lambda i: (0, 0)
    l0 = lambda i: (0, 0, 0)
    in_specs = [pl.BlockSpec((n_seq, n_time, D), lambda i: (i // nt, i % nt, 0)),
                pl.BlockSpec((1, D), const),
                pl.BlockSpec((None, R, D), l0),
                pl.BlockSpec((None, R, N), l0),
                pl.BlockSpec((1, N), const)]
    args = [x3, w.reshape(1, D), w_lr, w_lr2, b_lr.reshape(1, N)]
    aliases = {}
    if prev is not None:
        for k, p in enumerate(prev):
            in_specs.append(pl.BlockSpec(memory_space=pl.ANY))
            args.append(p)
            aliases[len(args) - 1] = k
    return pl.pallas_call(
        functools.partial(_prenorm_kernel, n_aliased=len(aliases)),
        grid=(nb * nt,),
        in_specs=in_specs,
        out_specs=[pl.BlockSpec((tm, D), row), pl.BlockSpec((tm, D), row), pl.BlockSpec((tm, N), row)],
        out_shape=[jax.ShapeDtypeStruct((total_rows, D), F32), jax.ShapeDtypeStruct((total_rows, D), BF16),
                   jax.ShapeDtypeStruct((total_rows, N), F32)],
        scratch_shapes=[pltpu.VMEM((D // LANES, tm, LANES), F32)],
        input_output_aliases=aliases,
        compiler_params=_cparams(("parallel",)),
        name="prenorm",
    )(*args)


def _gram_levels(C):
    out = []
    m = C // 2
    while m >= FINE_BLOCK:
        out.append(m)
        m //= 2
    return out


def _recurrence_unit_phases(u):
    C, nv = u.C, u.nv
    NS = SEQ_PER_ROWGROUP
    FINE = min(C, FINE_BLOCK)
    levels = _gram_levels(C)
    NL = len(levels)

    def rows(t):
        return pl.ds(t * NS, NS)

    def seq_rows(s):
        return pl.ds(s, C, stride=NS)

    def vector_phase(hh):
        acc = jnp.zeros((NS, LANES), F32)
        for t in range(C):
            if u.hgrn:
                z = u.p[u.slab(hh, 1), rows(t), :]
                e = jnp.exp(-jnp.abs(z))
                r = 1.0 / (1.0 + e)
                er = e * r
                pos = z >= 0.0
                sig_p = jnp.where(pos, r, er)
                sig_n = jnp.where(pos, er, r)
                oml = u.oml(hh)
                g_t = jnp.log(u.lbf(hh) + oml * sig_p)
                u.kk[hh, rows(t), :] = oml * sig_n
            else:
                g_t = u.g(hh, rows(t))
            acc = acc + g_t
            u.a[hh, t] = jnp.exp2(g_t * LOG2E)
            u.b[hh, t] = acc * LOG2E
        b_last = acc * LOG2E
        d_last = jnp.exp2(b_last)

        d_t = jnp.transpose(jnp.concatenate([d_last] * (LANES // NS), axis=0))
        for s in range(NS):
            u.dcol[hh, s] = jnp.broadcast_to(d_t[:, s:s + 1], (LANES, LANES))

        def q_rows(t):
            q_t = u.p[u.slab(hh, 0), rows(t), :]
            return q_t if u.q_scale == 1.0 else q_t * u.q_scale

        def k_rows(t):
            return u.kk[hh, rows(t), :] if u.hgrn else u.p[u.slab(hh, 1), rows(t), :]

        for t in range(C):
            q_t, k_t, b_t = q_rows(t), k_rows(t), u.b[hh, t]
            u.zs[hh, NL, rows(t), :] = q_t * jnp.exp2(b_t)
            u.zs[hh, NL + 1, rows(t), :] = k_t * jnp.exp2(b_last - b_t)
            for j, m in enumerate(levels):
                r_idx = (t // (2 * m)) * (2 * m) + m - 1
                if (t % (2 * m)) >= m:
                    u.zs[hh, j, rows(t), :] = q_t * jnp.exp2(b_t - u.b[hh, r_idx])
                else:
                    u.zs[hh, j, rows(t), :] = k_t * jnp.exp2(u.b[hh, r_idx] - b_t)

        for t0 in range(0, C, FINE):
            k_blk, v_blk, decay = {}, {}, {}
            for t in range(t0, t0 + FINE):
                q_t = q_rows(t)
                k_blk[t] = k_rows(t)
                v_blk[t] = [u.p[u.slab(hh, 2 + i), rows(t), :] for i in range(nv)]
                a_t = u.a[hh, t]
                for s in range(t0, t):
                    decay[s] = a_t if s == t - 1 else decay[s] * a_t
                w = jnp.sum(q_t * k_blk[t], axis=-1, keepdims=True)
                o_t = [w * v_blk[t][i] for i in range(nv)]
                for s in range(t0, t):
                    w = jnp.sum(q_t * (k_blk[s] * decay[s]), axis=-1, keepdims=True)
                    o_t = [o_t[i] + w * v_blk[s][i] for i in range(nv)]
                for i in range(nv):
                    u.oacc[hh, i, rows(t), :] = o_t[i]

    ti = lax.broadcasted_iota(jnp.int32, (C, C), 0)
    si = lax.broadcasted_iota(jnp.int32, (C, C), 1)
    xr = jnp.bitwise_xor(ti, si)
    lower = ti > si
    masks = [lower & (xr >= m) & (xr < 2 * m) for m in levels]

    def matmul_phase(hh):
        def load_v(s):
            vs = [u.p[u.slab(hh, 2 + i), seq_rows(s), :] for i in range(nv)]
            vb = vs[0] if nv == 1 else jnp.concatenate(vs, axis=1)
            return vb.astype(BF16)

        scores = []
        for s in range(NS if NL else 0):
            sc = jnp.zeros((C, C), F32)
            for j in range(1, NL):
                z = u.zs[hh, j, seq_rows(s), :].astype(BF16)
                gram = lax.dot_general(z, z, (((1,), (1,)), ((), ())), preferred_element_type=F32)
                sc = jnp.where(masks[j], gram, sc)
            z = u.zs[hh, 0, seq_rows(s), :].astype(BF16)
            quad = lax.dot_general(z[C // 2:], z[:C // 2], (((1,), (1,)), ((), ())), preferred_element_type=F32)
            low = jnp.concatenate([quad, sc[C // 2:, C // 2:]], axis=1)
            scores.append(jnp.concatenate([sc[:C // 2], low], axis=0).astype(BF16))
        one_dot = NL > 0 and nv > 1
        inter = []
        for s in range(NS):
            zq = u.zs[hh, NL, seq_rows(s), :].astype(BF16)
            S = u.S_in(s, hh)
            o = jnp.dot(zq, S.astype(BF16), preferred_element_type=F32)
            dc = u.dcol[hh, s]
            if nv > 1:
                dc = jnp.concatenate([dc] * nv, axis=1)
            if one_dot:
                inter.append((o, dc * S))
            else:
                zk = u.zs[hh, NL + 1, seq_rows(s), :].astype(BF16)
                upd = lax.dot_general(zk, load_v(s), (((0,), (0,)), ((), ())), preferred_element_type=F32)
                u.S_out(s, hh, dc * S + upd)
                inter.append((o, None))
        for s in range(NS):
            o, decayed = inter[s]
            if one_dot:
                zk_t = jnp.transpose(u.zs[hh, NL + 1, seq_rows(s), :]).astype(BF16)
                res = jnp.dot(jnp.concatenate([zk_t, scores[s]], axis=0), load_v(s), preferred_element_type=F32)
                u.S_out(s, hh, decayed + res[:LANES])
                o = o + res[LANES:]
            elif NL:
                o = o + jnp.dot(scores[s], load_v(s), preferred_element_type=F32)
            for i in range(nv):
                sl = slice(i * LANES, (i + 1) * LANES)
                u.oacc[hh, i, seq_rows(s), :] = u.oacc[hh, i, seq_rows(s), :] + o[:, sl]

    def gate_phase(hh):
        if nv == 1:
            o_all = u.oacc[hh, 0]
        else:
            o_all = jnp.concatenate([u.oacc[hh, i] for i in range(nv)], axis=1)
        y = o_all * lax.rsqrt(jnp.mean(o_all * o_all, axis=-1, keepdims=True) + EPS) * u.onw()
        gs = [u.p[u.slab(hh, 2 + nv + i)] for i in range(nv)]
        gt = gs[0] if nv == 1 else jnp.concatenate(gs, axis=1)
        u.out(hh, y * (gt * _sigmoid(gt)))

    return vector_phase, matmul_phase, gate_phase


def _recurrence_kernel(*refs, C, CS, dv, hgrn, q_scale, n_aliased, HB):
    nv = dv // LANES
    NS = SEQ_PER_ROWGROUP
    R = NS * C
    it = iter(refs)
    xcur_ref, xnext_ref, xs_cur_ref, xs_next_ref = next(it), next(it), next(it), next(it)
    w_refs = [next(it) for _ in range(4)]
    g_ref, gs_ref = (None, None) if hgrn else (next(it), next(it))
    onw_ref = next(it)
    lbf_ref, oml_ref, lbfs_ref, omls_ref = (next(it), next(it), next(it), next(it)) if hgrn else (None,) * 4
    s0_ref = next(it)
    for _ in range(n_aliased):
        next(it)
    o_ref, sout_ref, os_ref, ssout_ref = next(it), next(it), next(it), next(it)
    p_scr, pnext_scr, a_scr, b_scr, zs_scr = next(it), next(it), next(it), next(it), next(it)
    kk_scr = next(it) if hgrn else None
    oacc_scr, dcol_scr, S_scr = next(it), next(it), next(it)
    ps_scr, psnext_scr, as_scr, bs_scr, zss_scr = next(it), next(it), next(it), next(it), next(it)
    kks_scr = next(it) if hgrn else None
    oaccs_scr, dcols_scr = next(it), next(it)

    c_idx = pl.program_id(1)
    hs = lax.rem(c_idx, HB)

    @pl.when(c_idx == 0)
    def _():
        S_scr[...] = jnp.zeros_like(S_scr)

    def project(xp_ref, xs_ref):
        x = jnp.concatenate([xp_ref[...], xs_ref[...]], axis=0)
        base = 0
        for w_ref in w_refs:
            p = _dot_t(x, w_ref[...])
            n = w_ref.shape[0] // LANES
            for i in range(n):
                pnext_scr[base + i] = p[:R, i * LANES:(i + 1) * LANES]
                psnext_scr[base + i] = p[R:, i * LANES:(i + 1) * LANES]
            base += n

    @pl.when(c_idx == 0)
    def _():
        project(xcur_ref, xs_cur_ref)

    p_scr[...] = pnext_scr[...]
    ps_scr[...] = psnext_scr[...]

    def slab_of(head_slot, k):
        if k < 2:
            return k * HB + head_slot
        if k < 2 + nv:
            return 2 * HB + head_slot * nv + (k - 2)
        return (2 + nv) * HB + head_slot * nv + (k - 2 - nv)

    def prompt_out(hh, val):
        o_ref[:, hh * dv:(hh + 1) * dv] = val

    def prompt_S_out(s, hh, val):
        S_scr[s, hh] = val

    def sample_out(hh, val):
        os_ref[...] = val

    def sample_S_out(s, hh, val):
        ssout_ref[s] = val

    lane = lambda hh: slice(hh * LANES, (hh + 1) * LANES)
    prompt = SimpleNamespace(
        C=C, nv=nv, hgrn=hgrn, q_scale=q_scale, heads=range(HB), p=p_scr, slab=slab_of,
        g=lambda hh, r: g_ref[r, lane(hh)], lbf=lambda hh: lbf_ref[:, lane(hh)],
        oml=lambda hh: oml_ref[:, lane(hh)], onw=lambda: onw_ref[...],
        a=a_scr, b=b_scr, zs=zs_scr, kk=kk_scr, oacc=oacc_scr, dcol=dcol_scr,
        S_in=lambda s, hh: S_scr[s, hh], S_out=prompt_S_out, out=prompt_out)
    sample = SimpleNamespace(
        C=CS, nv=nv, hgrn=hgrn, q_scale=q_scale, heads=range(1), p=ps_scr,
        slab=lambda hh, k: slab_of(hs, k),
        g=lambda hh, r: gs_ref[r, :], lbf=lambda hh: lbfs_ref[...], oml=lambda hh: omls_ref[...],
        onw=lambda: onw_ref[...],
        a=as_scr, b=bs_scr, zs=zss_scr, kk=kks_scr, oacc=oaccs_scr, dcol=dcols_scr,
        S_in=lambda s, hh: s0_ref[s], S_out=sample_S_out, out=sample_out)

    units = [(un, _recurrence_unit_phases(un)) for un in (prompt, sample)]
    for un, (vector_phase, _, _) in units:
        for hh in un.heads:
            vector_phase(hh)
    project(xnext_ref, xs_next_ref)
    for un, (_, matmul_phase, _) in units:
        for hh in un.heads:
            matmul_phase(hh)
    for un, (_, _, gate_phase) in units:
        for hh in un.heads:
            gate_phase(hh)

    @pl.when(c_idx == pl.num_programs(1) - 1)
    def _():
        sout_ref[...] = S_scr[...]


def _recurrence_call(xn, w_c, w_col0, g_gla, s_in, onw, lbf, oml, *, layer, hgrn, C, CS, n_chunks, sample_row0,
                     n_groups, heads_per_step, prev_states=None, prev_sample_states=None):
    NS = SEQ_PER_ROWGROUP
    R, RS = NS * C, NS * CS
    H = HGRN_HEADS if hgrn else GLA_HEADS
    dv = HGRN_DV if hgrn else GLA_DV
    dk = LANES
    nv = dv // LANES
    T, D = xn.shape
    W = (2 + 2 * nv) * LANES
    HB = heads_per_step
    assert H % HB == 0 and n_chunks == n_groups * HB and sample_row0 % RS == 0
    rbs = sample_row0 // RS
    last = n_chunks - 1
    s_grp = lambda c: c // HB
    s_head = lambda h, c: h * HB + c % HB

    in_specs = [pl.BlockSpec((R, D), lambda h, c: (0, 0)),
                pl.BlockSpec((R, D), lambda h, c: (jnp.minimum(c + 1, last), 0)),
                pl.BlockSpec((RS, D), lambda h, c: (rbs, 0)),
                pl.BlockSpec((RS, D), lambda h, c: (rbs + s_grp(jnp.minimum(c + 1, last)), 0))]
    args = [xn, xn, xn, xn]
    col = w_col0
    for width in (LANES, LANES, dv, dv):
        blk = HB * width
        assert col % blk == 0
        in_specs.append(pl.BlockSpec((None, blk, D), lambda h, c, j=col // blk: (layer, j + h, 0)))
        args.append(w_c)
        col += H * width
    if not hgrn:
        in_specs += [pl.BlockSpec((R, HB * LANES), lambda h, c: (c, h)),
                     pl.BlockSpec((RS, LANES), lambda h, c: (rbs + s_grp(c), s_head(h, c)))]
        args += [g_gla, g_gla]
    in_specs.append(pl.BlockSpec((1, dv), lambda h, c: (0, 0)))
    args.append(onw.reshape(1, dv))
    if hgrn:
        in_specs += [pl.BlockSpec((1, HB * LANES), lambda h, c: (0, h)),
                     pl.BlockSpec((1, HB * LANES), lambda h, c: (0, h)),
                     pl.BlockSpec((1, LANES), lambda h, c: (0, s_head(h, c))),
                     pl.BlockSpec((1, LANES), lambda h, c: (0, s_head(h, c)))]
        args += [lbf.reshape(1, -1), oml.reshape(1, -1)] * 2
    ss_map = lambda h, c: (layer, s_grp(c), s_head(h, c), 0, 0)
    in_specs.append(pl.BlockSpec((None, NS, None, dk, dv), ss_map))
    args.append(s_in)
    aliases = {}
    for k, prev in ((1, prev_states), (3, prev_sample_states)):
        if prev is not None:
            in_specs.append(pl.BlockSpec(memory_space=pl.ANY))
            args.append(prev)
            aliases[len(args) - 1] = k

    out_specs = [pl.BlockSpec((R, HB * dv), lambda h, c: (c, h)),
                 pl.BlockSpec((None, NS, HB, dk, dv), lambda h, c: (layer, 0, h, 0, 0)),
                 pl.BlockSpec((RS, dv), lambda h, c: (s_grp(c), s_head(h, c))),
                 pl.BlockSpec((None, NS, None, dk, dv), ss_map)]
    out_shape = [jax.ShapeDtypeStruct((T, H * dv), F32),
                 jax.ShapeDtypeStruct((DEPTH, NS, H, dk, dv), F32),
                 jax.ShapeDtypeStruct((n_groups * RS, H * dv), F32),
                 jax.ShapeDtypeStruct((DEPTH, n_groups * NS, H, dk, dv), F32)]

    def unit_scratch(heads, Cu, prompt_state):
        Ru = NS * Cu
        sc = [pltpu.VMEM((HB * W // LANES, Ru, LANES), F32),
              pltpu.VMEM((HB * W // LANES, Ru, LANES), F32),
              pltpu.VMEM((heads, Cu, NS, LANES), F32),
              pltpu.VMEM((heads, Cu, NS, LANES), F32),
              pltpu.VMEM((heads, len(_gram_levels(Cu)) + 2, Ru, LANES), F32)]
        if hgrn:
            sc.append(pltpu.VMEM((heads, Ru, LANES), F32))
        sc += [pltpu.VMEM((heads, nv, Ru, LANES), F32),
               pltpu.VMEM((heads, NS, LANES, LANES), F32)]
        if prompt_state:
            sc.append(pltpu.VMEM((NS, heads, dk, dv), F32))
        return sc

    kern = functools.partial(_recurrence_kernel, C=C, CS=CS, dv=dv, hgrn=hgrn,
                             q_scale=1.0 if hgrn else GLA_DK ** -0.5, n_aliased=len(aliases), HB=HB)
    return pl.pallas_call(
        kern,
        grid=(H // HB, n_chunks),
        in_specs=in_specs,
        out_specs=out_specs,
        out_shape=out_shape,
        scratch_shapes=unit_scratch(HB, C, True) + unit_scratch(1, CS, False),
        input_output_aliases=aliases,
        compiler_params=_cparams(("parallel", "arbitrary")),
        name="hgrn" if hgrn else "gla",
    )(*args)


def _outproj_kernel(og_ref, oh_ref, xn_ref, x_ref, wga_ref, wgb_ref, wo_ref, npost_ref, npre_ref, h_ref, hn_ref):
    half = x_ref.shape[0] // 2
    for r in (slice(0, half), slice(half, 2 * half)):
        xn = xn_ref[r, :]
        ga = _dot_t(xn, wga_ref[...])
        gb = _dot_t(xn, wgb_ref[...])
        merged = og_ref[r, :] * _sigmoid(ga) + oh_ref[r, :] * _sigmoid(gb)
        m = jnp.dot(merged.astype(BF16), wo_ref[...], preferred_element_type=F32)
        h = x_ref[r, :] + _rms(m, npost_ref[...])
        h_ref[r, :] = h
        hn_ref[r, :] = _rms(h, npre_ref[...]).astype(BF16)


def _outproj_call(og, oh, xn, x, w_c, w_o, npost, npre, *, layer, tm):
    T, D = x.shape
    row = lambda i: (i, 0)
    const = lambda i: (0, 0)
    once = pl.Buffered(1)
    n_blk = w_c.shape[1] // D
    return pl.pallas_call(
        _outproj_kernel,
        grid=(T // tm,),
        in_specs=[pl.BlockSpec((tm, D), row),
                  pl.BlockSpec((tm, D), row),
                  pl.BlockSpec((tm, D), row),
                  pl.BlockSpec((tm, D), row),
                  pl.BlockSpec((None, D, D), lambda i: (layer, n_blk - 2, 0), pipeline_mode=once),
                  pl.BlockSpec((None, D, D), lambda i: (layer, n_blk - 1, 0), pipeline_mode=once),
                  pl.BlockSpec((None, D, D), lambda i: (layer, 0, 0), pipeline_mode=once),
                  pl.BlockSpec((1, D), const),
                  pl.BlockSpec((1, D), const)],
        out_specs=[pl.BlockSpec((tm, D), row), pl.BlockSpec((tm, D), row)],
        out_shape=[jax.ShapeDtypeStruct((T, D), F32), jax.ShapeDtypeStruct((T, D), BF16)],
        compiler_params=_cparams(("parallel",)),
        name="outproj",
    )(og, oh, xn, x, w_c, w_c, w_o, npost.reshape(1, D), npre.reshape(1, D))


def _ffn_kernel(*refs, splits, emit_next):
    hn_ref, h_ref, wa_ref, wu_ref, wf_ref, nfpost_ref = refs[:6]
    if emit_next:
        nnext_ref, wlr_ref, wlr2_ref, blr_ref, y_ref, yn_ref, g_ref = refs[6:]
    else:
        y_ref, rows_scr = refs[6:]
    hn = hn_ref[...]
    f = None
    for lo, hi in splits:
        sl = slice(lo, hi)
        a = jnp.dot(hn, wa_ref[:, sl], preferred_element_type=F32)
        u = jnp.dot(hn, wu_ref[:, sl], preferred_element_type=F32)
        g = (a * _sigmoid(a) * u).astype(BF16)
        part = jnp.dot(g, wf_ref[sl, :], preferred_element_type=F32)
        f = part if f is None else f + part
    y = h_ref[...] + _rms(f, nfpost_ref[...])
    if emit_next:
        y_ref[...] = y
        yn = _rms(y, nnext_ref[...]).astype(BF16)
        yn_ref[...] = yn
        g_ref[...] = _gla_gate(yn, wlr_ref[...], wlr2_ref[...], blr_ref[...])
    else:
        n_seq, n_time, D = y_ref.shape
        for j in range(D // LANES):
            rows_scr[j] = y[:, j * LANES:(j + 1) * LANES]
        for seq in range(n_seq):
            for j in range(D // LANES):
                y_ref[seq, :, j * LANES:(j + 1) * LANES] = rows_scr[j, _tile_rows(seq, n_time), :]


def _ffn_call(hn, h, w_ffn, w_f, nfpost, next_layer, *, layer, tm, final=None):
    T, D = h.shape
    F = w_f.shape[1]
    rb0, n_tiles = 0, T // tm
    if final is not None:
        row0, B, L, n_seq, n_time = final
        assert n_seq * n_time == tm and row0 % tm == 0 and (B == n_seq or L == n_time)
        rb0, n_tiles, nt = row0 // tm, B * L // tm, L // n_time
    row = lambda i: (rb0 + i, 0)
    const = lambda i: (0, 0)
    once = pl.Buffered(1)
    half = (F // MXU_COLS + 1) // 2 * MXU_COLS
    assert F % MXU_COLS == 0 and F % LANES == 0
    in_specs = [pl.BlockSpec((tm, D), row),
                pl.BlockSpec((tm, D), row),
                pl.BlockSpec((None, D, F), lambda i: (layer, 0, 0), pipeline_mode=once),
                pl.BlockSpec((None, D, F), lambda i: (layer, 0, 1), pipeline_mode=once),
                pl.BlockSpec((None, F, D), lambda i: (layer, 0, 0), pipeline_mode=once),
                pl.BlockSpec((1, D), const)]
    args = [hn, h, w_ffn, w_ffn, w_f, nfpost.reshape(1, D)]
    scratch = []
    if next_layer is None:
        out_specs = [pl.BlockSpec((n_seq, n_time, D), lambda i: (i // nt, i % nt, 0))]
        out_shape = [jax.ShapeDtypeStruct((B, L, D), F32)]
        scratch = [pltpu.VMEM((D // LANES, tm, LANES), F32)]
    else:
        out_specs = [pl.BlockSpec((tm, D), row)]
        out_shape = [jax.ShapeDtypeStruct((T, D), F32)]
        nnext, w_lr, w_lr2, b_lr = next_layer
        R, N = w_lr2.shape[1:]
        lnext = lambda i: (layer + 1, 0, 0)
        in_specs += [pl.BlockSpec((1, D), const),
                     pl.BlockSpec((None, R, D), lnext, pipeline_mode=once),
                     pl.BlockSpec((None, R, N), lnext, pipeline_mode=once),
                     pl.BlockSpec((1, N), const)]
        args += [nnext.reshape(1, D), w_lr, w_lr2, b_lr.reshape(1, N)]
        out_specs += [pl.BlockSpec((tm, D), row), pl.BlockSpec((tm, N), row)]
        out_shape += [jax.ShapeDtypeStruct((T, D), BF16), jax.ShapeDtypeStruct((T, N), F32)]
    return pl.pallas_call(
        functools.partial(_ffn_kernel, splits=((0, half), (half, F)), emit_next=next_layer is not None),
        grid=(n_tiles,),
        in_specs=in_specs,
        out_specs=out_specs,
        out_shape=out_shape,
        scratch_shapes=scratch,
        compiler_params=_cparams(("parallel",)),
        name="ffn",
    )(*args)


def kernel(x_prompt, x_sample, state_gla, state_hgrn, norm_mix_pre, norm_mix_post, norm_ffn_pre, norm_ffn_post, w_in, gla_w_lr2, gla_b_lr, gla_onorm, hgrn_lb, hgrn_onorm, w_out, w_ffn_in, w_ffn_out):
    B, L, D = x_prompt.shape
    BS, LS, _ = x_sample.shape
    NS = SEQ_PER_ROWGROUP
    assert B == NS and BS % NS == 0 and L % PROMPT_CHUNK == 0
    TP = B * L
    TS = BS * LS
    T = TP + TS
    assert TP % POST_ROWS == 0 and TS % POST_ROWS == 0

    sm = jax.nn.softmax(hgrn_lb.astype(F32), axis=0)
    lb = jnp.clip(jnp.cumsum(sm, axis=0) - sm[0:1], 0.0, 1.0 - 1e-6)
    lbf = jnp.maximum(lb, LB_FLOOR)
    oml = 1.0 - lb

    c_lr0 = GLA_HEADS * (2 * GLA_DK + 2 * GLA_DV)
    c_lr1 = c_lr0 + GLA_GATE_RANK
    w_t = jnp.swapaxes(w_in, 1, 2).astype(BF16)
    w_g = w_t
    w_h = w_t[:, c_lr1:]
    assert w_h.shape[1] == HGRN_HEADS * (2 * HGRN_EXPAND + 2 * HGRN_DV) + 2 * D
    w_lr = jnp.pad(w_t[:, c_lr0:c_lr1], ((0, 0), (0, LANES - GLA_GATE_RANK), (0, 0)))
    w_lr2 = jnp.pad(gla_w_lr2, ((0, 0), (0, LANES - GLA_GATE_RANK), (0, 0))).astype(BF16)
    w_o = w_out.astype(BF16)
    w_ffn = w_ffn_in.astype(BF16)
    w_f = w_ffn_out.astype(BF16)

    P_TIME = POST_ROWS // NS
    S_SEQ = POST_ROWS // LS
    pre = _prenorm_call(x_prompt, norm_mix_pre[0], w_lr, w_lr2, gla_b_lr[0],
                        n_seq=NS, n_time=P_TIME, row0=0, total_rows=T)
    x, xn, g_gla = _prenorm_call(x_sample, norm_mix_pre[0], w_lr, w_lr2, gla_b_lr[0],
                                 n_seq=S_SEQ, n_time=LS, row0=TP, total_rows=T, prev=pre)

    n_sample_groups = BS // NS
    gla_p = gla_s = hg_p = hg_s = None
    for l in range(DEPTH):
        og, gla_p, og_s, gla_s = _recurrence_call(
            xn, w_g, 0, g_gla, state_gla, gla_onorm[l], None, None, layer=l, hgrn=False,
            C=PROMPT_CHUNK, CS=LS, n_chunks=L // PROMPT_CHUNK, sample_row0=TP, n_groups=n_sample_groups,
            heads_per_step=HEADS_PER_STEP, prev_states=gla_p, prev_sample_states=gla_s)
        oh, hg_p, oh_s, hg_s = _recurrence_call(
            xn, w_h, 0, None, state_hgrn, hgrn_onorm[l], lbf[l], oml[l], layer=l, hgrn=True,
            C=PROMPT_CHUNK, CS=LS, n_chunks=L // PROMPT_CHUNK, sample_row0=TP, n_groups=n_sample_groups,
            heads_per_step=HEADS_PER_STEP, prev_states=hg_p, prev_sample_states=hg_s)
        og = lax.dynamic_update_slice(og, og_s, (TP, 0))
        oh = lax.dynamic_update_slice(oh, oh_s, (TP, 0))

        h, hn = _outproj_call(og, oh, xn, x, w_h, w_o, norm_mix_post[l], norm_ffn_pre[l], layer=l, tm=POST_ROWS)
        if l + 1 < DEPTH:
            nxt = (norm_mix_pre[l + 1], w_lr, w_lr2, gla_b_lr[l + 1])
            x, xn, g_gla = _ffn_call(hn, h, w_ffn, w_f, norm_ffn_post[l], nxt, layer=l, tm=POST_ROWS)
        else:
            (y_p,) = _ffn_call(hn, h, w_ffn, w_f, norm_ffn_post[l], None, layer=l, tm=POST_ROWS,
                               final=(0, B, L, NS, P_TIME))
            (y_s,) = _ffn_call(hn, h, w_ffn, w_f, norm_ffn_post[l], None, layer=l, tm=POST_ROWS,
                               final=(TP, BS, LS, S_SEQ, LS))

    return (y_p, y_s, gla_p, gla_s, hg_p, hg_s)
```

```python
import functools
from types import SimpleNamespace

import jax
import jax.numpy as jnp
from jax import lax
from jax.experimental import pallas as pl
from jax.experimental.pallas import tpu as pltpu

F32 = jnp.float32
BF16 = jnp.bfloat16

DEPTH = 4
GLA_HEADS = 4
GLA_DK = 128
GLA_DV = 256
GLA_GATE_RANK = 16
GLA_TAU = 16.0
HGRN_HEADS = 8
HGRN_EXPAND = 128
HGRN_DV = 128
LB_FLOOR = 1e-20
EPS = 1e-6
LOG2E = 1.4426950408889634

SUBLANES = 8
LANES = 128
MXU_COLS = 256
SEQ_PER_ROWGROUP = SUBLANES

PROMPT_CHUNK = 64
HEADS_PER_STEP = 2
FINE_BLOCK = SUBLANES
POST_ROWS = 512
VMEM_LIMIT = 52 * 1024 * 1024


def _cparams(sem):
    return pltpu.CompilerParams(dimension_semantics=sem, vmem_limit_bytes=VMEM_LIMIT)


def _rms(x, w):
    return x * lax.rsqrt(jnp.mean(x * x, axis=-1, keepdims=True) + EPS) * w


def _sigmoid(x):
    return 0.5 + 0.5 * jnp.tanh(0.5 * x)


def _dot_t(x, wt):
    return lax.dot_general(x, wt, (((1,), (1,)), ((), ())), preferred_element_type=F32)


def _gla_gate(xn, wlr_t, wlr2, blr):
    glr = _dot_t(xn, wlr_t)
    y = jnp.dot(glr.astype(BF16), wlr2, preferred_element_type=F32) + blr
    ls = -(jnp.maximum(-y, 0.0) + jnp.log1p(jnp.exp(-jnp.abs(y))))
    return ls * (1.0 / GLA_TAU)


def _tile_rows(seq, n_time):
    NS = SEQ_PER_ROWGROUP
    return pl.ds((seq // NS) * n_time * NS + seq % NS, n_time, stride=NS)


def _prenorm_kernel(*refs, n_aliased):
    x_ref, w_ref, wlr_ref, wlr2_ref, blr_ref = refs[:5]
    xr_ref, xn_ref, g_ref, rows_scr = refs[5 + n_aliased:]
    n_seq, n_time, D = x_ref.shape
    for seq in range(n_seq):
        for j in range(D // LANES):
            rows_scr[j, _tile_rows(seq, n_time), :] = x_ref[seq, :, j * LANES:(j + 1) * LANES]
    x = jnp.concatenate([rows_scr[j] for j in range(D // LANES)], axis=1)
    xr_ref[...] = x
    xn = _rms(x, w_ref[...]).astype(BF16)
    xn_ref[...] = xn
    g_ref[...] = _gla_gate(xn, wlr_ref[...], wlr2_ref[...], blr_ref[...])


def _prenorm_call(x3, w, w_lr, w_lr2, b_lr, *, n_seq, n_time, row0, total_rows, prev=None):
    B, L, D = x3.shape
    R, N = w_lr2.shape[1:]
    assert w_lr.shape[1:] == (R, D)
    tm = n_seq * n_time
    assert B % n_seq == 0 and L % n_time == 0 and (B == n_seq or L == n_time) and row0 % tm == 0
    nb, nt = B // n_seq, L // n_time
    rb0 = row0 // tm
    row = lambda i: (rb0 + i, 0)
    const = lambda i: (0, 0)
    l0 = lambda i: (0, 0, 0)
    in_specs = [pl.BlockSpec((n_seq, n_time, D), lambda i: (i // nt, i % nt, 0)),
                pl.BlockSpec((1, D), const),
                pl.BlockSpec((None, R, D), l0),
                pl.BlockSpec((None, R, N), l0),
                pl.BlockSpec((1, N), const)]
    args = [x3, w.reshape(1, D), w_lr, w_lr2, b_lr.reshape(1, N)]
    aliases = {}
    if prev is not None:
        for k, p in enumerate(prev):
            in_specs.append(pl.BlockSpec(memory_space=pl.ANY))
            args.append(p)
            aliases[len(args) - 1] = k
    return pl.pallas_call(
        functools.partial(_prenorm_kernel, n_aliased=len(aliases)),
        grid=(nb * nt,),
        in_specs=in_specs,
        out_specs=[pl.BlockSpec((tm, D), row), pl.BlockSpec((tm, D), row), pl.BlockSpec((tm, N), row)],
        out_shape=[jax.ShapeDtypeStruct((total_rows, D), F32), jax.ShapeDtypeStruct((total_rows, D), BF16),
                   jax.ShapeDtypeStruct((total_rows, N), F32)],
        scratch_shapes=[pltpu.VMEM((D // LANES, tm, LANES), F32)],
        input_output_aliases=aliases,
        compiler_params=_cparams(("parallel",)),
        name="prenorm",
    )(*args)


def _gram_levels(C):
    out = []
    m = C // 2
    while m >= FINE_BLOCK:
        out.append(m)
        m //= 2
    return out


def _recurrence_unit_phases(u):
    C, nv = u.C, u.nv
    NS = SEQ_PER_ROWGROUP
    FINE = min(C, FINE_BLOCK)
    levels = _gram_levels(C)
    NL = len(levels)

    def rows(t):
        return pl.ds(t * NS, NS)

    def seq_rows(s):
        return pl.ds(s, C, stride=NS)

    def vector_phase(hh):
        acc = jnp.zeros((NS, LANES), F32)
        for t in range(C):
            if u.hgrn:
                z = u.p[u.slab(hh, 1), rows(t), :]
                e = jnp.exp(-jnp.abs(z))
                r = 1.0 / (1.0 + e)
                er = e * r
                pos = z >= 0.0
                sig_p = jnp.where(pos, r, er)
                sig_n = jnp.where(pos, er, r)
                oml = u.oml(hh)
                g_t = jnp.log(u.lbf(hh) + oml * sig_p)
                u.kk[hh, rows(t), :] = oml * sig_n
            else:
                g_t = u.g(hh, rows(t))
            acc = acc + g_t
            u.a[hh, t] = jnp.exp2(g_t * LOG2E)
            u.b[hh, t] = acc * LOG2E
        b_last = acc * LOG2E
        d_last = jnp.exp2(b_last)

        d_t = jnp.transpose(jnp.concatenate([d_last] * (LANES // NS), axis=0))
        for s in range(NS):
            u.dcol[hh, s] = jnp.broadcast_to(d_t[:, s:s + 1], (LANES, LANES))

        def q_rows(t):
            q_t = u.p[u.slab(hh, 0), rows(t), :]
            return q_t if u.q_scale == 1.0 else q_t * u.q_scale

        def k_rows(t):
            return u.kk[hh, rows(t), :] if u.hgrn else u.p[u.slab(hh, 1), rows(t), :]

        for t in range(C):
            q_t, k_t, b_t = q_rows(t), k_rows(t), u.b[hh, t]
            u.zs[hh, NL, rows(t), :] = q_t * jnp.exp2(b_t)
            u.zs[hh, NL + 1, rows(t), :] = k_t * jnp.exp2(b_last - b_t)
            for j, m in enumerate(levels):
                r_idx = (t // (2 * m)) * (2 * m) + m - 1
                if (t % (2 * m)) >= m:
                    u.zs[hh, j, rows(t), :] = q_t * jnp.exp2(b_t - u.b[hh, r_idx])
                else:
                    u.zs[hh, j, rows(t), :] = k_t * jnp.exp2(u.b[hh, r_idx] - b_t)

        for t0 in range(0, C, FINE):
            k_blk, v_blk, decay = {}, {}, {}
            for t in range(t0, t0 + FINE):
                q_t = q_rows(t)
                k_blk[t] = k_rows(t)
                v_blk[t] = [u.p[u.slab(hh, 2 + i), rows(t), :] for i in range(nv)]
                a_t = u.a[hh, t]
                for s in range(t0, t):
                    decay[s] = a_t if s == t - 1 else decay[s] * a_t
                w = jnp.sum(q_t * k_blk[t], axis=-1, keepdims=True)
                o_t = [w * v_blk[t][i] for i in range(nv)]
                for s in range(t0, t):
                    w = jnp.sum(q_t * (k_blk[s] * decay[s]), axis=-1, keepdims=True)
                    o_t = [o_t[i] + w * v_blk[s][i] for i in range(nv)]
                for i in range(nv):
                    u.oacc[hh, i, rows(t), :] = o_t[i]

    ti = lax.broadcasted_iota(jnp.int32, (C, C), 0)
    si = lax.broadcasted_iota(jnp.int32, (C, C), 1)
    xr = jnp.bitwise_xor(ti, si)
    lower = ti > si
    masks = [lower & (xr >= m) & (xr < 2 * m) for m in levels]

    def matmul_phase(hh):
        def load_v(s):
            vs = [u.p[u.slab(hh, 2 + i), seq_rows(s), :] for i in range(nv)]
            vb = vs[0] if nv == 1 else jnp.concatenate(vs, axis=1)
            return vb.astype(BF16)

        scores = []
        for s in range(NS if NL else 0):
            sc = jnp.zeros((C, C), F32)
            for j in range(NL):
                z = u.zs[hh, j, seq_rows(s), :].astype(BF16)
                gram = lax.dot_general(z, z, (((1,), (1,)), ((), ())), preferred_element_type=F32)
                sc = jnp.where(masks[j], gram, sc)
            scores.append(sc.astype(BF16))
        one_dot = NL > 0 and nv > 1
        inter = []
        for s in range(NS):
            zq = u.zs[hh, NL, seq_rows(s), :].astype(BF16)
            S = u.S_in(s, hh)
            o = jnp.dot(zq, S.astype(BF16), preferred_element_type=F32)
            dc = u.dcol[hh, s]
            if nv > 1:
                dc = jnp.concatenate([dc] * nv, axis=1)
            if one_dot:
                inter.append((o, dc * S))
            else:
                zk = u.zs[hh, NL + 1, seq_rows(s), :].astype(BF16)
                upd = lax.dot_general(zk, load_v(s), (((0,), (0,)), ((), ())), preferred_element_type=F32)
                u.S_out(s, hh, dc * S + upd)
                inter.append((o, None))
        for s in range(NS):
            o, decayed = inter[s]
            if one_dot:
                zk_t = jnp.transpose(u.zs[hh, NL + 1, seq_rows(s), :]).astype(BF16)
                res = jnp.dot(jnp.concatenate([zk_t, scores[s]], axis=0), load_v(s), preferred_element_type=F32)
                u.S_out(s, hh, decayed + res[:LANES])
                o = o + res[LANES:]
            elif NL:
                o = o + jnp.dot(scores[s], load_v(s), preferred_element_type=F32)
            for i in range(nv):
                sl = slice(i * LANES, (i + 1) * LANES)
                u.oacc[hh, i, seq_rows(s), :] = u.oacc[hh, i, seq_rows(s), :] + o[:, sl]

    def gate_phase(hh):
        if nv == 1:
            o_all = u.oacc[hh, 0]
        else:
            o_all = jnp.concatenate([u.oacc[hh, i] for i in range(nv)], axis=1)
        y = o_all * lax.rsqrt(jnp.mean(o_all * o_all, axis=-1, keepdims=True) + EPS) * u.onw()
        gs = [u.p[u.slab(hh, 2 + nv + i)] for i in range(nv)]
        gt = gs[0] if nv == 1 else jnp.concatenate(gs, axis=1)
        u.out(hh, y * (gt * _sigmoid(gt)))

    return vector_phase, matmul_phase, gate_phase


def _recurrence_kernel(*refs, C, CS, dv, hgrn, q_scale, n_aliased, HB):
    nv = dv // LANES
    NS = SEQ_PER_ROWGROUP
    R = NS * C
    it = iter(refs)
    xcur_ref, xnext_ref, xs_cur_ref, xs_next_ref = next(it), next(it), next(it), next(it)
    w_refs = [next(it) for _ in range(4)]
    g_ref, gs_ref = (None, None) if hgrn else (next(it), next(it))
    onw_ref = next(it)
    lbf_ref, oml_ref, lbfs_ref, omls_ref = (next(it), next(it), next(it), next(it)) if hgrn else (None,) * 4
    s0_ref = next(it)
    for _ in range(n_aliased):
        next(it)
    o_ref, sout_ref, os_ref, ssout_ref = next(it), next(it), next(it), next(it)
    p_scr, pnext_scr, a_scr, b_scr, zs_scr = next(it), next(it), next(it), next(it), next(it)
    kk_scr = next(it) if hgrn else None
    oacc_scr, dcol_scr, S_scr = next(it), next(it), next(it)
    ps_scr, psnext_scr, as_scr, bs_scr, zss_scr = next(it), next(it), next(it), next(it), next(it)
    kks_scr = next(it) if hgrn else None
    oaccs_scr, dcols_scr = next(it), next(it)

    c_idx = pl.program_id(1)
    hs = lax.rem(c_idx, HB)

    @pl.when(c_idx == 0)
    def _():
        S_scr[...] = jnp.zeros_like(S_scr)

    swap_by_parity = nv > 1

    def project(xp_ref, xs_ref, dst, dst_s):
        x = jnp.concatenate([xp_ref[...], xs_ref[...]], axis=0)
        base = 0
        for w_ref in w_refs:
            p = _dot_t(x, w_ref[...])
            n = w_ref.shape[0] // LANES
            for i in range(n):
                dst[base + i] = p[:R, i * LANES:(i + 1) * LANES]
                dst_s[base + i] = p[R:, i * LANES:(i + 1) * LANES]
            base += n

    @pl.when(c_idx == 0)
    def _():
        if swap_by_parity:
            project(xcur_ref, xs_cur_ref, p_scr, ps_scr)
        else:
            project(xcur_ref, xs_cur_ref, pnext_scr, psnext_scr)

    def slab_of(head_slot, k):
        if k < 2:
            return k * HB + head_slot
        if k < 2 + nv:
            return 2 * HB + head_slot * nv + (k - 2)
        return (2 + nv) * HB + head_slot * nv + (k - 2 - nv)

    def prompt_out(hh, val):
        o_ref[:, hh * dv:(hh + 1) * dv] = val

    def prompt_S_out(s, hh, val):
        S_scr[s, hh] = val

    def sample_out(hh, val):
        os_ref[...] = val

    def sample_S_out(s, hh, val):
        ssout_ref[s] = val

    lane = lambda hh: slice(hh * LANES, (hh + 1) * LANES)

    def step(p_cur, ps_cur, p_nxt, ps_nxt):
        prompt = SimpleNamespace(
            C=C, nv=nv, hgrn=hgrn, q_scale=q_scale, heads=range(HB), p=p_cur, slab=slab_of,
            g=lambda hh, r: g_ref[r, lane(hh)], lbf=lambda hh: lbf_ref[:, lane(hh)],
            oml=lambda hh: oml_ref[:, lane(hh)], onw=lambda: onw_ref[...],
            a=a_scr, b=b_scr, zs=zs_scr, kk=kk_scr, oacc=oacc_scr, dcol=dcol_scr,
            S_in=lambda s, hh: S_scr[s, hh], S_out=prompt_S_out, out=prompt_out)
        sample = SimpleNamespace(
            C=CS, nv=nv, hgrn=hgrn, q_scale=q_scale, heads=range(1), p=ps_cur,
            slab=lambda hh, k: slab_of(hs, k),
            g=lambda hh, r: gs_ref[r, :], lbf=lambda hh: lbfs_ref[...], oml=lambda hh: omls_ref[...],
            onw=lambda: onw_ref[...],
            a=as_scr, b=bs_scr, zs=zss_scr, kk=kks_scr, oacc=oaccs_scr, dcol=dcols_scr,
            S_in=lambda s, hh: s0_ref[s], S_out=sample_S_out, out=sample_out)

        units = [(un, _recurrence_unit_phases(un)) for un in (prompt, sample)]
        for un, (vector_phase, _, _) in units:
            for hh in un.heads:
                vector_phase(hh)
        project(xnext_ref, xs_next_ref, p_nxt, ps_nxt)
        for un, (_, matmul_phase, _) in units:
            for hh in un.heads:
                matmul_phase(hh)
        for un, (_, _, gate_phase) in units:
            for hh in un.heads:
                gate_phase(hh)

    if swap_by_parity:
        parity = lax.rem(c_idx, 2)

        @pl.when(parity == 0)
        def _():
            step(p_scr, ps_scr, pnext_scr, psnext_scr)

        @pl.when(parity == 1)
        def _():
            step(pnext_scr, psnext_scr, p_scr, ps_scr)
    else:
        p_scr[...] = pnext_scr[...]
        ps_scr[...] = psnext_scr[...]
        step(p_scr, ps_scr, pnext_scr, psnext_scr)

    @pl.when(c_idx == pl.num_programs(1) - 1)
    def _():
        sout_ref[...] = S_scr[...]


def _recurrence_call(xn, w_c, w_col0, g_gla, s_in, onw, lbf, oml, *, layer, hgrn, C, CS, n_chunks, sample_row0,
                     n_groups, heads_per_step, prev_states=None, prev_sample_states=None):
    NS = SEQ_PER_ROWGROUP
    R, RS = NS * C, NS * CS
    H = HGRN_HEADS if hgrn else GLA_HEADS
    dv = HGRN_DV if hgrn else GLA_DV
    dk = LANES
    nv = dv // LANES
    T, D = xn.shape
    W = (2 + 2 * nv) * LANES
    HB = heads_per_step
    assert H % HB == 0 and n_chunks == n_groups * HB and sample_row0 % RS == 0
    rbs = sample_row0 // RS
    last = n_chunks - 1
    s_grp = lambda c: c // HB
    s_head = lambda h, c: h * HB + c % HB

    in_specs = [pl.BlockSpec((R, D), lambda h, c: (0, 0)),
                pl.BlockSpec((R, D), lambda h, c: (jnp.minimum(c + 1, last), 0)),
                pl.BlockSpec((RS, D), lambda h, c: (rbs, 0)),
                pl.BlockSpec((RS, D), lambda h, c: (rbs + s_grp(jnp.minimum(c + 1, last)), 0))]
    args = [xn, xn, xn, xn]
    col = w_col0
    for width in (LANES, LANES, dv, dv):
        blk = HB * width
        assert col % blk == 0
        in_specs.append(pl.BlockSpec((None, blk, D), lambda h, c, j=col // blk: (layer, j + h, 0)))
        args.append(w_c)
        col += H * width
    if not hgrn:
        in_specs += [pl.BlockSpec((R, HB * LANES), lambda h, c: (c, h)),
                     pl.BlockSpec((RS, LANES), lambda h, c: (rbs + s_grp(c), s_head(h, c)))]
        args += [g_gla, g_gla]
    in_specs.append(pl.BlockSpec((1, dv), lambda h, c: (0, 0)))
    args.append(onw.reshape(1, dv))
    if hgrn:
        in_specs += [pl.BlockSpec((1, HB * LANES), lambda h, c: (0, h)),
                     pl.BlockSpec((1, HB * LANES), lambda h, c: (0, h)),
                     pl.BlockSpec((1, LANES), lambda h, c: (0, s_head(h, c))),
                     pl.BlockSpec((1, LANES), lambda h, c: (0, s_head(h, c)))]
        args += [lbf.reshape(1, -1), oml.reshape(1, -1)] * 2
    ss_map = lambda h, c: (layer, s_grp(c), s_head(h, c), 0, 0)
    in_specs.append(pl.BlockSpec((None, NS, None, dk, dv), ss_map))
    args.append(s_in)
    aliases = {}
    for k, prev in ((1, prev_states), (3, prev_sample_states)):
        if prev is not None:
            in_specs.append(pl.BlockSpec(memory_space=pl.ANY))
            args.append(prev)
            aliases[len(args) - 1] = k

    out_specs = [pl.BlockSpec((R, HB * dv), lambda h, c: (c, h)),
                 pl.BlockSpec((None, NS, HB, dk, dv), lambda h, c: (layer, 0, h, 0, 0)),
                 pl.BlockSpec((RS, dv), lambda h, c: (s_grp(c), s_head(h, c))),
                 pl.BlockSpec((None, NS, None, dk, dv), ss_map)]
    out_shape = [jax.ShapeDtypeStruct((T, H * dv), F32),
                 jax.ShapeDtypeStruct((DEPTH, NS, H, dk, dv), F32),
                 jax.ShapeDtypeStruct((n_groups * RS, H * dv), F32),
                 jax.ShapeDtypeStruct((DEPTH, n_groups * NS, H, dk, dv), F32)]

    def unit_scratch(heads, Cu, prompt_state):
        Ru = NS * Cu
        sc = [pltpu.VMEM((HB * W // LANES, Ru, LANES), F32),
              pltpu.VMEM((HB * W // LANES, Ru, LANES), F32),
              pltpu.VMEM((heads, Cu, NS, LANES), F32),
              pltpu.VMEM((heads, Cu, NS, LANES), F32),
              pltpu.VMEM((heads, len(_gram_levels(Cu)) + 2, Ru, LANES), F32)]
        if hgrn:
            sc.append(pltpu.VMEM((heads, Ru, LANES), F32))
        sc += [pltpu.VMEM((heads, nv, Ru, LANES), F32),
               pltpu.VMEM((heads, NS, LANES, LANES), F32)]
        if prompt_state:
            sc.append(pltpu.VMEM((NS, heads, dk, dv), F32))
        return sc

    kern = functools.partial(_recurrence_kernel, C=C, CS=CS, dv=dv, hgrn=hgrn,
                             q_scale=1.0 if hgrn else GLA_DK ** -0.5, n_aliased=len(aliases), HB=HB)
    return pl.pallas_call(
        kern,
        grid=(H // HB, n_chunks),
        in_specs=in_specs,
        out_specs=out_specs,
        out_shape=out_shape,
        scratch_shapes=unit_scratch(HB, C, True) + unit_scratch(1, CS, False),
        input_output_aliases=aliases,
        compiler_params=_cparams(("parallel", "arbitrary")),
        name="hgrn" if hgrn else "gla",
    )(*args)


def _outproj_kernel(og_ref, oh_ref, xn_ref, x_ref, wga_ref, wgb_ref, wo_ref, npost_ref, npre_ref, h_ref, hn_ref):
    xn = xn_ref[...]
    ga = _dot_t(xn, wga_ref[...])
    gb = _dot_t(xn, wgb_ref[...])
    merged = og_ref[...] * _sigmoid(ga) + oh_ref[...] * _sigmoid(gb)
    m = jnp.dot(merged.astype(BF16), wo_ref[...], preferred_element_type=F32)
    h = x_ref[...] + _rms(m, npost_ref[...])
    h_ref[...] = h
    hn_ref[...] = _rms(h, npre_ref[...]).astype(BF16)


def _outproj_call(og, oh, xn, x, w_c, w_o, npost, npre, *, layer, tm):
    T, D = x.shape
    row = lambda i: (i, 0)
    const = lambda i: (0, 0)
    once = pl.Buffered(1)
    n_blk = w_c.shape[1] // D
    return pl.pallas_call(
        _outproj_kernel,
        grid=(T // tm,),
        in_specs=[pl.BlockSpec((tm, D), row),
                  pl.BlockSpec((tm, D), row),
                  pl.BlockSpec((tm, D), row),
                  pl.BlockSpec((tm, D), row),
                  pl.BlockSpec((None, D, D), lambda i: (layer, n_blk - 2, 0), pipeline_mode=once),
                  pl.BlockSpec((None, D, D), lambda i: (layer, n_blk - 1, 0), pipeline_mode=once),
                  pl.BlockSpec((None, D, D), lambda i: (layer, 0, 0), pipeline_mode=once),
                  pl.BlockSpec((1, D), const),
                  pl.BlockSpec((1, D), const)],
        out_specs=[pl.BlockSpec((tm, D), row), pl.BlockSpec((tm, D), row)],
        out_shape=[jax.ShapeDtypeStruct((T, D), F32), jax.ShapeDtypeStruct((T, D), BF16)],
        compiler_params=_cparams(("parallel",)),
        name="outproj",
    )(og, oh, xn, x, w_c, w_c, w_o, npost.reshape(1, D), npre.reshape(1, D))


def _ffn_kernel(*refs, splits, emit_next):
    hn_ref, h_ref, wa_ref, wu_ref, wf_ref, nfpost_ref = refs[:6]
    if emit_next:
        nnext_ref, wlr_ref, wlr2_ref, blr_ref, y_ref, yn_ref, g_ref = refs[6:]
    else:
        y_ref, rows_scr = refs[6:]
    hn = hn_ref[...]
    f = None
    for lo, hi in splits:
        sl = slice(lo, hi)
        a = jnp.dot(hn, wa_ref[:, sl], preferred_element_type=F32)
        u = jnp.dot(hn, wu_ref[:, sl], preferred_element_type=F32)
        g = (a * _sigmoid(a) * u).astype(BF16)
        part = jnp.dot(g, wf_ref[sl, :], preferred_element_type=F32)
        f = part if f is None else f + part
    y = h_ref[...] + _rms(f, nfpost_ref[...])
    if emit_next:
        y_ref[...] = y
        yn = _rms(y, nnext_ref[...]).astype(BF16)
        yn_ref[...] = yn
        g_ref[...] = _gla_gate(yn, wlr_ref[...], wlr2_ref[...], blr_ref[...])
    else:
        n_seq, n_time, D = y_ref.shape
        for j in range(D // LANES):
            rows_scr[j] = y[:, j * LANES:(j + 1) * LANES]
        for seq in range(n_seq):
            for j in range(D // LANES):
                y_ref[seq, :, j * LANES:(j + 1) * LANES] = rows_scr[j, _tile_rows(seq, n_time), :]


def _ffn_call(hn, h, w_ffn, w_f, nfpost, next_layer, *, layer, tm, final=None):
    T, D = h.shape
    F = w_f.shape[1]
    rb0, n_tiles = 0, T // tm
    if final is not None:
        row0, B, L, n_seq, n_time = final
        assert n_seq * n_time == tm and row0 % tm == 0 and (B == n_seq or L == n_time)
        rb0, n_tiles, nt = row0 // tm, B * L // tm, L // n_time
    row = lambda i: (rb0 + i, 0)
    const = lambda i: (0, 0)
    once = pl.Buffered(1)
    half = (F // MXU_COLS + 1) // 2 * MXU_COLS
    assert F % MXU_COLS == 0 and F % LANES == 0
    in_specs = [pl.BlockSpec((tm, D), row),
                pl.BlockSpec((tm, D), row),
                pl.BlockSpec((None, D, F), lambda i: (layer, 0, 0), pipeline_mode=once),
                pl.BlockSpec((None, D, F), lambda i: (layer, 0, 1), pipeline_mode=once),
                pl.BlockSpec((None, F, D), lambda i: (layer, 0, 0), pipeline_mode=once),
                pl.BlockSpec((1, D), const)]
    args = [hn, h, w_ffn, w_ffn, w_f, nfpost.reshape(1, D)]
    scratch = []
    if next_layer is None:
        out_specs = [pl.BlockSpec((n_seq, n_time, D), lambda i: (i // nt, i % nt, 0))]
        out_shape = [jax.ShapeDtypeStruct((B, L, D), F32)]
        scratch = [pltpu.VMEM((D // LANES, tm, LANES), F32)]
    else:
        out_specs = [pl.BlockSpec((tm, D), row)]
        out_shape = [jax.ShapeDtypeStruct((T, D), F32)]
        nnext, w_lr, w_lr2, b_lr = next_layer
        R, N = w_lr2.shape[1:]
        lnext = lambda i: (layer + 1, 0, 0)
        in_specs += [pl.BlockSpec((1, D), const),
                     pl.BlockSpec((None, R, D), lnext, pipeline_mode=once),
                     pl.BlockSpec((None, R, N), lnext, pipeline_mode=once),
                     pl.BlockSpec((1, N), const)]
        args += [nnext.reshape(1, D), w_lr, w_lr2, b_lr.reshape(1, N)]
        out_specs += [pl.BlockSpec((tm, D), row), pl.BlockSpec((tm, N), row)]
        out_shape += [jax.ShapeDtypeStruct((T, D), BF16), jax.ShapeDtypeStruct((T, N), F32)]
    return pl.pallas_call(
        functools.partial(_ffn_kernel, splits=((0, half), (half, F)), emit_next=next_layer is not None),
        grid=(n_tiles,),
        in_specs=in_specs,
        out_specs=out_specs,
        out_shape=out_shape,
        scratch_shapes=scratch,
        compiler_params=_cparams(("parallel",)),
        name="ffn",
    )(*args)


def kernel(x_prompt, x_sample, state_gla, state_hgrn, norm_mix_pre, norm_mix_post, norm_ffn_pre, norm_ffn_post, w_in, gla_w_lr2, gla_b_lr, gla_onorm, hgrn_lb, hgrn_onorm, w_out, w_ffn_in, w_ffn_out):
    B, L, D = x_prompt.shape
    BS, LS, _ = x_sample.shape
    NS = SEQ_PER_ROWGROUP
    assert B == NS and BS % NS == 0 and L % PROMPT_CHUNK == 0
    TP = B * L
    TS = BS * LS
    T = TP + TS
    assert TP % POST_ROWS == 0 and TS % POST_ROWS == 0

    sm = jax.nn.softmax(hgrn_lb.astype(F32), axis=0)
    lb = jnp.clip(jnp.cumsum(sm, axis=0) - sm[0:1], 0.0, 1.0 - 1e-6)
    lbf = jnp.maximum(lb, LB_FLOOR)
    oml = 1.0 - lb

    c_lr0 = GLA_HEADS * (2 * GLA_DK + 2 * GLA_DV)
    c_lr1 = c_lr0 + GLA_GATE_RANK
    w_t = jnp.swapaxes(w_in, 1, 2).astype(BF16)
    w_g = w_t
    w_h = w_t[:, c_lr1:]
    assert w_h.shape[1] == HGRN_HEADS * (2 * HGRN_EXPAND + 2 * HGRN_DV) + 2 * D
    w_lr = jnp.pad(w_t[:, c_lr0:c_lr1], ((0, 0), (0, LANES - GLA_GATE_RANK), (0, 0)))
    w_lr2 = jnp.pad(gla_w_lr2, ((0, 0), (0, LANES - GLA_GATE_RANK), (0, 0))).astype(BF16)
    w_o = w_out.astype(BF16)
    w_ffn = w_ffn_in.astype(BF16)
    w_f = w_ffn_out.astype(BF16)

    P_TIME = POST_ROWS // NS
    S_SEQ = POST_ROWS // LS
    pre = _prenorm_call(x_prompt, norm_mix_pre[0], w_lr, w_lr2, gla_b_lr[0],
                        n_seq=NS, n_time=P_TIME, row0=0, total_rows=T)
    x, xn, g_gla = _prenorm_call(x_sample, norm_mix_pre[0], w_lr, w_lr2, gla_b_lr[0],
                                 n_seq=S_SEQ, n_time=LS, row0=TP, total_rows=T, prev=pre)

    n_sample_groups = BS // NS
    gla_p = gla_s = hg_p = hg_s = None
    for l in range(DEPTH):
        og, gla_p, og_s, gla_s = _recurrence_call(
            xn, w_g, 0, g_gla, state_gla, gla_onorm[l], None, None, layer=l, hgrn=False,
            C=PROMPT_CHUNK, CS=LS, n_chunks=L // PROMPT_CHUNK, sample_row0=TP, n_groups=n_sample_groups,
            heads_per_step=HEADS_PER_STEP, prev_states=gla_p, prev_sample_states=gla_s)
        oh, hg_p, oh_s, hg_s = _recurrence_call(
            xn, w_h, 0, None, state_hgrn, hgrn_onorm[l], lbf[l], oml[l], layer=l, hgrn=True,
            C=PROMPT_CHUNK, CS=LS, n_chunks=L // PROMPT_CHUNK, sample_row0=TP, n_groups=n_sample_groups,
            heads_per_step=HEADS_PER_STEP, prev_states=hg_p, prev_sample_states=hg_s)
        og = lax.dynamic_update_slice(og, og_s, (TP, 0))
        oh = lax.dynamic_update_slice(oh, oh_s, (TP, 0))

        h, hn = _outproj_call(og, oh, xn, x, w_h, w_o, norm_mix_post[l], norm_ffn_pre[l], layer=l, tm=POST_ROWS)
        if l + 1 < DEPTH:
            nxt = (norm_mix_pre[l + 1], w_lr, w_lr2, gla_b_lr[l + 1])
            x, xn, g_gla = _ffn_call(hn, h, w_ffn, w_f, norm_ffn_post[l], nxt, layer=l, tm=POST_ROWS)
        else:
            (y_p,) = _ffn_call(hn, h, w_ffn, w_f, norm_ffn_post[l], None, layer=l, tm=POST_ROWS,
                               final=(0, B, L, NS, P_TIME))
            (y_s,) = _ffn_call(hn, h, w_ffn, w_f, norm_ffn_post[l], None, layer=l, tm=POST_ROWS,
                               final=(TP, BS, LS, S_SEQ, LS))

    return (y_p, y_s, gla_p, gla_s, hg_p, hg_s)
```

```python
import functools
from types import SimpleNamespace

import jax
import jax.numpy as jnp
from jax import lax
from jax.experimental import pallas as pl
from jax.experimental.pallas import tpu as pltpu

F32 = jnp.float32
BF16 = jnp.bfloat16

DEPTH = 4
GLA_HEADS = 4
GLA_DK = 128
GLA_DV = 256
GLA_GATE_RANK = 16
GLA_TAU = 16.0
HGRN_HEADS = 8
HGRN_EXPAND = 128
HGRN_DV = 128
LB_FLOOR = 1e-20
EPS = 1e-6
LOG2E = 1.4426950408889634

SUBLANES = 8
LANES = 128
MXU_COLS = 256
SEQ_PER_ROWGROUP = SUBLANES

PROMPT_CHUNK = 64
HEADS_PER_STEP = 2
FINE_BLOCK = SUBLANES
POST_ROWS = 512
VMEM_LIMIT = 52 * 1024 * 1024


def _cparams(sem):
    return pltpu.CompilerParams(dimension_semantics=sem, vmem_limit_bytes=VMEM_LIMIT)


def _rms(x, w):
    return x * lax.rsqrt(jnp.mean(x * x, axis=-1, keepdims=True) + EPS) * w


def _sigmoid(x):
    return 0.5 + 0.5 * jnp.tanh(0.5 * x)


def _dot_t(x, wt):
    return lax.dot_general(x, wt, (((1,), (1,)), ((), ())), preferred_element_type=F32)


def _gla_gate(xn, wlr_t, wlr2, blr):
    glr = _dot_t(xn, wlr_t)
    y = jnp.dot(glr.astype(BF16), wlr2, preferred_element_type=F32) + blr
    ls = -(jnp.maximum(-y, 0.0) + jnp.log1p(jnp.exp(-jnp.abs(y))))
    return ls * (1.0 / GLA_TAU)


def _tile_rows(seq, n_time):
    NS = SEQ_PER_ROWGROUP
    return pl.ds((seq // NS) * n_time * NS + seq % NS, n_time, stride=NS)


def _prenorm_kernel(*refs, n_aliased):
    x_ref, w_ref, wlr_ref, wlr2_ref, blr_ref = refs[:5]
    xr_ref, xn_ref, g_ref, rows_scr = refs[5 + n_aliased:]
    n_seq, n_time, D = x_ref.shape
    for seq in range(n_seq):
        for j in range(D // LANES):
            rows_scr[j, _tile_rows(seq, n_time), :] = x_ref[seq, :, j * LANES:(j + 1) * LANES]
    x = jnp.concatenate([rows_scr[j] for j in range(D // LANES)], axis=1)
    xr_ref[...] = x
    xn = _rms(x, w_ref[...]).astype(BF16)
    xn_ref[...] = xn
    g_ref[...] = _gla_gate(xn, wlr_ref[...], wlr2_ref[...], blr_ref[...])


def _prenorm_call(x3, w, w_lr, w_lr2, b_lr, *, n_seq, n_time, row0, total_rows, prev=None):
    B, L, D = x3.shape
    R, N = w_lr2.shape[1:]
    assert w_lr.shape[1:] == (R, D)
    tm = n_seq * n_time
    assert B % n_seq == 0 and L % n_time == 0 and (B == n_seq or L == n_time) and row0 % tm == 0
    nb, nt = B // n_seq, L // n_time
    rb0 = row0 // tm
    row = lambda i: (rb0 + i, 0)
    const = lambda i: (0, 0)
    l0 = lambda i: (0, 0, 0)
    in_specs = [pl.BlockSpec((n_seq, n_time, D), lambda i: (i // nt, i % nt, 0)),
                pl.BlockSpec((1, D), const),
                pl.BlockSpec((None, R, D), l0),
                pl.BlockSpec((None, R, N), l0),
                pl.BlockSpec((1, N), const)]
    args = [x3, w.reshape(1, D), w_lr, w_lr2, b_lr.reshape(1, N)]
    aliases = {}
    if prev is not None:
        for k, p in enumerate(prev):
            in_specs.append(pl.BlockSpec(memory_space=pl.ANY))
            args.append(p)
            aliases[len(args) - 1] = k
    return pl.pallas_call(
        functools.partial(_prenorm_kernel, n_aliased=len(aliases)),
        grid=(nb * nt,),
        in_specs=in_specs,
        out_specs=[pl.BlockSpec((tm, D), row), pl.BlockSpec((tm, D), row), pl.BlockSpec((tm, N), row)],
        out_shape=[jax.ShapeDtypeStruct((total_rows, D), F32), jax.ShapeDtypeStruct((total_rows, D), BF16),
                   jax.ShapeDtypeStruct((total_rows, N), F32)],
        scratch_shapes=[pltpu.VMEM((D // LANES, tm, LANES), F32)],
        input_output_aliases=aliases,
        compiler_params=_cparams(("parallel",)),
        name="prenorm",
    )(*args)


def _gram_levels(C):
    out = []
    m = C // 2
    while m >= FINE_BLOCK:
        out.append(m)
        m //= 2
    return out


def _recurrence_unit_phases(u):
    C, nv = u.C, u.nv
    NS = SEQ_PER_ROWGROUP
    FINE = min(C, FINE_BLOCK)
    levels = _gram_levels(C)
    NL = len(levels)

    def rows(t):
        return pl.ds(t * NS, NS)

    def seq_rows(s):
        return pl.ds(s, C, stride=NS)

    def vector_phase(hh):
        acc = jnp.zeros((NS, LANES), F32)
        for t in range(C):
            if u.hgrn:
                z = u.p[u.slab(hh, 1), rows(t), :]
                e = jnp.exp(-jnp.abs(z))
                r = 1.0 / (1.0 + e)
                er = e * r
                pos = z >= 0.0
                sig_p = jnp.where(pos, r, er)
                sig_n = jnp.where(pos, er, r)
                oml = u.oml(hh)
                g_t = jnp.log(u.lbf(hh) + oml * sig_p)
                u.kk[hh, rows(t), :] = oml * sig_n
            else:
                g_t = u.g(hh, rows(t))
            acc = acc + g_t
            u.a[hh, t] = jnp.exp2(g_t * LOG2E)
            u.b[hh, t] = acc * LOG2E
        b_last = acc * LOG2E
        d_last = jnp.exp2(b_last)

        d_t = jnp.transpose(jnp.concatenate([d_last] * (LANES // NS), axis=0))
        for s in range(NS):
            u.dcol[hh, s] = jnp.broadcast_to(d_t[:, s:s + 1], (LANES, LANES))

        def q_rows(t):
            q_t = u.p[u.slab(hh, 0), rows(t), :]
            return q_t if u.q_scale == 1.0 else q_t * u.q_scale

        def k_rows(t):
            return u.kk[hh, rows(t), :] if u.hgrn else u.p[u.slab(hh, 1), rows(t), :]

        for t in range(C):
            q_t, k_t, b_t = q_rows(t), k_rows(t), u.b[hh, t]
            u.zs[hh, NL, rows(t), :] = q_t * jnp.exp2(b_t)
            u.zs[hh, NL + 1, rows(t), :] = k_t * jnp.exp2(b_last - b_t)
            for j, m in enumerate(levels):
                r_idx = (t // (2 * m)) * (2 * m) + m - 1
                if (t % (2 * m)) >= m:
                    u.zs[hh, j, rows(t), :] = q_t * jnp.exp2(b_t - u.b[hh, r_idx])
                else:
                    u.zs[hh, j, rows(t), :] = k_t * jnp.exp2(u.b[hh, r_idx] - b_t)

        for t0 in range(0, C, FINE):
            k_blk, v_blk, decay = {}, {}, {}
            for t in range(t0, t0 + FINE):
                q_t = q_rows(t)
                k_blk[t] = k_rows(t)
                v_blk[t] = [u.p[u.slab(hh, 2 + i), rows(t), :] for i in range(nv)]
                a_t = u.a[hh, t]
                for s in range(t0, t):
                    decay[s] = a_t if s == t - 1 else decay[s] * a_t
                w = jnp.sum(q_t * k_blk[t], axis=-1, keepdims=True)
                o_t = [w * v_blk[t][i] for i in range(nv)]
                for s in range(t0, t):
                    w = jnp.sum(q_t * (k_blk[s] * decay[s]), axis=-1, keepdims=True)
                    o_t = [o_t[i] + w * v_blk[s][i] for i in range(nv)]
                for i in range(nv):
                    u.oacc[hh, i, rows(t), :] = o_t[i]

    ti = lax.broadcasted_iota(jnp.int32, (C, C), 0)
    si = lax.broadcasted_iota(jnp.int32, (C, C), 1)
    xr = jnp.bitwise_xor(ti, si)
    lower = ti > si
    masks = [lower & (xr >= m) & (xr < 2 * m) for m in levels]

    def matmul_phase(hh):
        def load_v(s):
            vs = [u.p[u.slab(hh, 2 + i), seq_rows(s), :] for i in range(nv)]
            vb = vs[0] if nv == 1 else jnp.concatenate(vs, axis=1)
            return vb.astype(BF16)

        scores = []
        for s in range(NS if NL else 0):
            sc = jnp.zeros((C, C), F32)
            quadrant_top = nv > 1
            for j in range(1 if quadrant_top else 0, NL):
                z = u.zs[hh, j, seq_rows(s), :].astype(BF16)
                gram = lax.dot_general(z, z, (((1,), (1,)), ((), ())), preferred_element_type=F32)
                sc = jnp.where(masks[j], gram, sc)
            if quadrant_top:
                z = u.zs[hh, 0, seq_rows(s), :].astype(BF16)
                quad = lax.dot_general(z[C // 2:], z[:C // 2], (((1,), (1,)), ((), ())),
                                       preferred_element_type=F32)
                low = jnp.concatenate([quad, sc[C // 2:, C // 2:]], axis=1)
                sc = jnp.concatenate([sc[:C // 2], low], axis=0)
            scores.append(sc.astype(BF16))
        one_dot = NL > 0 and nv > 1
        inter = []
        for s in range(NS):
            zq = u.zs[hh, NL, seq_rows(s), :].astype(BF16)
            S = u.S_in(s, hh)
            o = jnp.dot(zq, S.astype(BF16), preferred_element_type=F32)
            dc = u.dcol[hh, s]
            if nv > 1:
                dc = jnp.concatenate([dc] * nv, axis=1)
            if one_dot:
                inter.append((o, dc * S))
            else:
                zk = u.zs[hh, NL + 1, seq_rows(s), :].astype(BF16)
                upd = lax.dot_general(zk, load_v(s), (((0,), (0,)), ((), ())), preferred_element_type=F32)
                u.S_out(s, hh, dc * S + upd)
                inter.append((o, None))
        for s in range(NS):
            o, decayed = inter[s]
            if one_dot:
                zk_t = jnp.transpose(u.zs[hh, NL + 1, seq_rows(s), :]).astype(BF16)
                res = jnp.dot(jnp.concatenate([zk_t, scores[s]], axis=0), load_v(s), preferred_element_type=F32)
                u.S_out(s, hh, decayed + res[:LANES])
                o = o + res[LANES:]
            elif NL:
                o = o + jnp.dot(scores[s], load_v(s), preferred_element_type=F32)
            for i in range(nv):
                sl = slice(i * LANES, (i + 1) * LANES)
                u.oacc[hh, i, seq_rows(s), :] = u.oacc[hh, i, seq_rows(s), :] + o[:, sl]

    def gate_phase(hh):
        if nv == 1:
            o_all = u.oacc[hh, 0]
        else:
            o_all = jnp.concatenate([u.oacc[hh, i] for i in range(nv)], axis=1)
        y = o_all * lax.rsqrt(jnp.mean(o_all * o_all, axis=-1, keepdims=True) + EPS) * u.onw()
        gs = [u.p[u.slab(hh, 2 + nv + i)] for i in range(nv)]
        gt = gs[0] if nv == 1 else jnp.concatenate(gs, axis=1)
        u.out(hh, y * (gt * _sigmoid(gt)))

    return vector_phase, matmul_phase, gate_phase


def _recurrence_kernel(*refs, C, CS, dv, hgrn, q_scale, n_aliased, HB):
    nv = dv // LANES
    NS = SEQ_PER_ROWGROUP
    R = NS * C
    it = iter(refs)
    xcur_ref, xnext_ref, xs_cur_ref, xs_next_ref = next(it), next(it), next(it), next(it)
    w_refs = [next(it) for _ in range(4)]
    g_ref, gs_ref = (None, None) if hgrn else (next(it), next(it))
    onw_ref = next(it)
    lbf_ref, oml_ref, lbfs_ref, omls_ref = (next(it), next(it), next(it), next(it)) if hgrn else (None,) * 4
    s0_ref = next(it)
    for _ in range(n_aliased):
        next(it)
    o_ref, sout_ref, os_ref, ssout_ref = next(it), next(it), next(it), next(it)
    p_scr, pnext_scr, a_scr, b_scr, zs_scr = next(it), next(it), next(it), next(it), next(it)
    kk_scr = next(it) if hgrn else None
    oacc_scr, dcol_scr, S_scr = next(it), next(it), next(it)
    ps_scr, psnext_scr, as_scr, bs_scr, zss_scr = next(it), next(it), next(it), next(it), next(it)
    kks_scr = next(it) if hgrn else None
    oaccs_scr, dcols_scr = next(it), next(it)

    c_idx = pl.program_id(1)
    hs = lax.rem(c_idx, HB)

    @pl.when(c_idx == 0)
    def _():
        S_scr[...] = jnp.zeros_like(S_scr)

    swap_by_parity = nv > 1

    def project(xp_ref, xs_ref, dst, dst_s):
        x = jnp.concatenate([xp_ref[...], xs_ref[...]], axis=0)
        base = 0
        for w_ref in w_refs:
            p = _dot_t(x, w_ref[...])
            n = w_ref.shape[0] // LANES
            for i in range(n):
                dst[base + i] = p[:R, i * LANES:(i + 1) * LANES]
                dst_s[base + i] = p[R:, i * LANES:(i + 1) * LANES]
            base += n

    @pl.when(c_idx == 0)
    def _():
        if swap_by_parity:
            project(xcur_ref, xs_cur_ref, p_scr, ps_scr)
        else:
            project(xcur_ref, xs_cur_ref, pnext_scr, psnext_scr)

    def slab_of(head_slot, k):
        if k < 2:
            return k * HB + head_slot
        if k < 2 + nv:
            return 2 * HB + head_slot * nv + (k - 2)
        return (2 + nv) * HB + head_slot * nv + (k - 2 - nv)

    def prompt_out(hh, val):
        o_ref[:, hh * dv:(hh + 1) * dv] = val

    def prompt_S_out(s, hh, val):
        S_scr[s, hh] = val

    def sample_out(hh, val):
        os_ref[...] = val

    def sample_S_out(s, hh, val):
        ssout_ref[s] = val

    lane = lambda hh: slice(hh * LANES, (hh + 1) * LANES)

    def step(p_cur, ps_cur, p_nxt, ps_nxt):
        prompt = SimpleNamespace(
            C=C, nv=nv, hgrn=hgrn, q_scale=q_scale, heads=range(HB), p=p_cur, slab=slab_of,
            g=lambda hh, r: g_ref[r, lane(hh)], lbf=lambda hh: lbf_ref[:, lane(hh)],
            oml=lambda hh: oml_ref[:, lane(hh)], onw=lambda: onw_ref[...],
            a=a_scr, b=b_scr, zs=zs_scr, kk=kk_scr, oacc=oacc_scr, dcol=dcol_scr,
            S_in=lambda s, hh: S_scr[s, hh], S_out=prompt_S_out, out=prompt_out)
        sample = SimpleNamespace(
            C=CS, nv=nv, hgrn=hgrn, q_scale=q_scale, heads=range(1), p=ps_cur,
            slab=lambda hh, k: slab_of(hs, k),
            g=lambda hh, r: gs_ref[r, :], lbf=lambda hh: lbfs_ref[...], oml=lambda hh: omls_ref[...],
            onw=lambda: onw_ref[...],
            a=as_scr, b=bs_scr, zs=zss_scr, kk=kks_scr, oacc=oaccs_scr, dcol=dcols_scr,
            S_in=lambda s, hh: s0_ref[s], S_out=sample_S_out, out=sample_out)

        units = [(un, _recurrence_unit_phases(un)) for un in (prompt, sample)]
        for un, (vector_phase, _, _) in units:
            for hh in un.heads:
                vector_phase(hh)
        project(xnext_ref, xs_next_ref, p_nxt, ps_nxt)
        for un, (_, matmul_phase, _) in units:
            for hh in un.heads:
                matmul_phase(hh)
        for un, (_, _, gate_phase) in units:
            for hh in un.heads:
                gate_phase(hh)

    if swap_by_parity:
        parity = lax.rem(c_idx, 2)

        @pl.when(parity == 0)
        def _():
            step(p_scr, ps_scr, pnext_scr, psnext_scr)

        @pl.when(parity == 1)
        def _():
            step(pnext_scr, psnext_scr, p_scr, ps_scr)
    else:
        p_scr[...] = pnext_scr[...]
        ps_scr[...] = psnext_scr[...]
        step(p_scr, ps_scr, pnext_scr, psnext_scr)

    @pl.when(c_idx == pl.num_programs(1) - 1)
    def _():
        sout_ref[...] = S_scr[...]


def _recurrence_call(xn, w_c, w_col0, g_gla, s_in, onw, lbf, oml, *, layer, hgrn, C, CS, n_chunks, sample_row0,
                     n_groups, heads_per_step, prev_states=None, prev_sample_states=None):
    NS = SEQ_PER_ROWGROUP
    R, RS = NS * C, NS * CS
    H = HGRN_HEADS if hgrn else GLA_HEADS
    dv = HGRN_DV if hgrn else GLA_DV
    dk = LANES
    nv = dv // LANES
    T, D = xn.shape
    W = (2 + 2 * nv) * LANES
    HB = heads_per_step
    assert H % HB == 0 and n_chunks == n_groups * HB and sample_row0 % RS == 0
    rbs = sample_row0 // RS
    last = n_chunks - 1
    s_grp = lambda c: c // HB
    s_head = lambda h, c: h * HB + c % HB

    in_specs = [pl.BlockSpec((R, D), lambda h, c: (0, 0)),
                pl.BlockSpec((R, D), lambda h, c: (jnp.minimum(c + 1, last), 0)),
                pl.BlockSpec((RS, D), lambda h, c: (rbs, 0)),
                pl.BlockSpec((RS, D), lambda h, c: (rbs + s_grp(jnp.minimum(c + 1, last)), 0))]
    args = [xn, xn, xn, xn]
    col = w_col0
    for width in (LANES, LANES, dv, dv):
        blk = HB * width
        assert col % blk == 0
        in_specs.append(pl.BlockSpec((None, blk, D), lambda h, c, j=col // blk: (layer, j + h, 0)))
        args.append(w_c)
        col += H * width
    if not hgrn:
        in_specs += [pl.BlockSpec((R, HB * LANES), lambda h, c: (c, h)),
                     pl.BlockSpec((RS, LANES), lambda h, c: (rbs + s_grp(c), s_head(h, c)))]
        args += [g_gla, g_gla]
    in_specs.append(pl.BlockSpec((1, dv), lambda h, c: (0, 0)))
    args.append(onw.reshape(1, dv))
    if hgrn:
        in_specs += [pl.BlockSpec((1, HB * LANES), lambda h, c: (0, h)),
                     pl.BlockSpec((1, HB * LANES), lambda h, c: (0, h)),
                     pl.BlockSpec((1, LANES), lambda h, c: (0, s_head(h, c))),
                     pl.BlockSpec((1, LANES), lambda h, c: (0, s_head(h, c)))]
        args += [lbf.reshape(1, -1), oml.reshape(1, -1)] * 2
    ss_map = lambda h, c: (layer, s_grp(c), s_head(h, c), 0, 0)
    in_specs.append(pl.BlockSpec((None, NS, None, dk, dv), ss_map))
    args.append(s_in)
    aliases = {}
    for k, prev in ((1, prev_states), (3, prev_sample_states)):
        if prev is not None:
            in_specs.append(pl.BlockSpec(memory_space=pl.ANY))
            args.append(prev)
            aliases[len(args) - 1] = k

    out_specs = [pl.BlockSpec((R, HB * dv), lambda h, c: (c, h)),
                 pl.BlockSpec((None, NS, HB, dk, dv), lambda h, c: (layer, 0, h, 0, 0)),
                 pl.BlockSpec((RS, dv), lambda h, c: (s_grp(c), s_head(h, c))),
                 pl.BlockSpec((None, NS, None, dk, dv), ss_map)]
    out_shape = [jax.ShapeDtypeStruct((T, H * dv), F32),
                 jax.ShapeDtypeStruct((DEPTH, NS, H, dk, dv), F32),
                 jax.ShapeDtypeStruct((n_groups * RS, H * dv), F32),
                 jax.ShapeDtypeStruct((DEPTH, n_groups * NS, H, dk, dv), F32)]

    def unit_scratch(heads, Cu, prompt_state):
        Ru = NS * Cu
        sc = [pltpu.VMEM((HB * W // LANES, Ru, LANES), F32),
              pltpu.VMEM((HB * W // LANES, Ru, LANES), F32),
              pltpu.VMEM((heads, Cu, NS, LANES), F32),
              pltpu.VMEM((heads, Cu, NS, LANES), F32),
              pltpu.VMEM((heads, len(_gram_levels(Cu)) + 2, Ru, LANES), F32)]
        if hgrn:
            sc.append(pltpu.VMEM((heads, Ru, LANES), F32))
        sc += [pltpu.VMEM((heads, nv, Ru, LANES), F32),
               pltpu.VMEM((heads, NS, LANES, LANES), F32)]
        if prompt_state:
            sc.append(pltpu.VMEM((NS, heads, dk, dv), F32))
        return sc

    kern = functools.partial(_recurrence_kernel, C=C, CS=CS, dv=dv, hgrn=hgrn,
                             q_scale=1.0 if hgrn else GLA_DK ** -0.5, n_aliased=len(aliases), HB=HB)
    return pl.pallas_call(
        kern,
        grid=(H // HB, n_chunks),
        in_specs=in_specs,
        out_specs=out_specs,
        out_shape=out_shape,
        scratch_shapes=unit_scratch(HB, C, True) + unit_scratch(1, CS, False),
        input_output_aliases=aliases,
        compiler_params=_cparams(("parallel", "arbitrary")),
        name="hgrn" if hgrn else "gla",
    )(*args)


def _outproj_kernel(og_ref, oh_ref, xn_ref, x_ref, wga_ref, wgb_ref, wo_ref, npost_ref, npre_ref, h_ref, hn_ref):
    xn = xn_ref[...]
    ga = _dot_t(xn, wga_ref[...])
    gb = _dot_t(xn, wgb_ref[...])
    merged = og_ref[...] * _sigmoid(ga) + oh_ref[...] * _sigmoid(gb)
    m = jnp.dot(merged.astype(BF16), wo_ref[...], preferred_element_type=F32)
    h = x_ref[...] + _rms(m, npost_ref[...])
    h_ref[...] = h
    hn_ref[...] = _rms(h, npre_ref[...]).astype(BF16)


def _outproj_call(og, oh, xn, x, w_c, w_o, npost, npre, *, layer, tm):
    T, D = x.shape
    row = lambda i: (i, 0)
    const = lambda i: (0, 0)
    once = pl.Buffered(1)
    n_blk = w_c.shape[1] // D
    return pl.pallas_call(
        _outproj_kernel,
        grid=(T // tm,),
        in_specs=[pl.BlockSpec((tm, D), row),
                  pl.BlockSpec((tm, D), row),
                  pl.BlockSpec((tm, D), row),
                  pl.BlockSpec((tm, D), row),
                  pl.BlockSpec((None, D, D), lambda i: (layer, n_blk - 2, 0), pipeline_mode=once),
                  pl.BlockSpec((None, D, D), lambda i: (layer, n_blk - 1, 0), pipeline_mode=once),
                  pl.BlockSpec((None, D, D), lambda i: (layer, 0, 0), pipeline_mode=once),
                  pl.BlockSpec((1, D), const),
                  pl.BlockSpec((1, D), const)],
        out_specs=[pl.BlockSpec((tm, D), row), pl.BlockSpec((tm, D), row)],
        out_shape=[jax.ShapeDtypeStruct((T, D), F32), jax.ShapeDtypeStruct((T, D), BF16)],
        compiler_params=_cparams(("parallel",)),
        name="outproj",
    )(og, oh, xn, x, w_c, w_c, w_o, npost.reshape(1, D), npre.reshape(1, D))


def _ffn_kernel(*refs, splits, emit_next):
    hn_ref, h_ref, wa_ref, wu_ref, wf_ref, nfpost_ref = refs[:6]
    if emit_next:
        nnext_ref, wlr_ref, wlr2_ref, blr_ref, y_ref, yn_ref, g_ref = refs[6:]
    else:
        y_ref, rows_scr = refs[6:]
    hn = hn_ref[...]
    f = None
    for lo, hi in splits:
        sl = slice(lo, hi)
        a = jnp.dot(hn, wa_ref[:, sl], preferred_element_type=F32)
        u = jnp.dot(hn, wu_ref[:, sl], preferred_element_type=F32)
        g = (a * _sigmoid(a) * u).astype(BF16)
        part = jnp.dot(g, wf_ref[sl, :], preferred_element_type=F32)
        f = part if f is None else f + part
    y = h_ref[...] + _rms(f, nfpost_ref[...])
    if emit_next:
        y_ref[...] = y
        yn = _rms(y, nnext_ref[...]).astype(BF16)
        yn_ref[...] = yn
        g_ref[...] = _gla_gate(yn, wlr_ref[...], wlr2_ref[...], blr_ref[...])
    else:
        n_seq, n_time, D = y_ref.shape
        for j in range(D // LANES):
            rows_scr[j] = y[:, j * LANES:(j + 1) * LANES]
        for seq in range(n_seq):
            for j in range(D // LANES):
                y_ref[seq, :, j * LANES:(j + 1) * LANES] = rows_scr[j, _tile_rows(seq, n_time), :]


def _ffn_call(hn, h, w_ffn, w_f, nfpost, next_layer, *, layer, tm, final=None):
    T, D = h.shape
    F = w_f.shape[1]
    rb0, n_tiles = 0, T // tm
    if final is not None:
        row0, B, L, n_seq, n_time = final
        assert n_seq * n_time == tm and row0 % tm == 0 and (B == n_seq or L == n_time)
        rb0, n_tiles, nt = row0 // tm, B * L // tm, L // n_time
    row = lambda i: (rb0 + i, 0)
    const = lambda i: (0, 0)
    once = pl.Buffered(1)
    half = (F // MXU_COLS + 1) // 2 * MXU_COLS
    assert F % MXU_COLS == 0 and F % LANES == 0
    in_specs = [pl.BlockSpec((tm, D), row),
                pl.BlockSpec((tm, D), row),
                pl.BlockSpec((None, D, F), lambda i: (layer, 0, 0), pipeline_mode=once),
                pl.BlockSpec((None, D, F), lambda i: (layer, 0, 1), pipeline_mode=once),
                pl.BlockSpec((None, F, D), lambda i: (layer, 0, 0), pipeline_mode=once),
                pl.BlockSpec((1, D), const)]
    args = [hn, h, w_ffn, w_ffn, w_f, nfpost.reshape(1, D)]
    scratch = []
    if next_layer is None:
        out_specs = [pl.BlockSpec((n_seq, n_time, D), lambda i: (i // nt, i % nt, 0))]
        out_shape = [jax.ShapeDtypeStruct((B, L, D), F32)]
        scratch = [pltpu.VMEM((D // LANES, tm, LANES), F32)]
    else:
        out_specs = [pl.BlockSpec((tm, D), row)]
        out_shape = [jax.ShapeDtypeStruct((T, D), F32)]
        nnext, w_lr, w_lr2, b_lr = next_layer
        R, N = w_lr2.shape[1:]
        lnext = lambda i: (layer + 1, 0, 0)
        in_specs += [pl.BlockSpec((1, D), const),
                     pl.BlockSpec((None, R, D), lnext, pipeline_mode=once),
                     pl.BlockSpec((None, R, N), lnext, pipeline_mode=once),
                     pl.BlockSpec((1, N), const)]
        args += [nnext.reshape(1, D), w_lr, w_lr2, b_lr.reshape(1, N)]
        out_specs += [pl.BlockSpec((tm, D), row), pl.BlockSpec((tm, N), row)]
        out_shape += [jax.ShapeDtypeStruct((T, D), BF16), jax.ShapeDtypeStruct((T, N), F32)]
    return pl.pallas_call(
        functools.partial(_ffn_kernel, splits=((0, half), (half, F)), emit_next=next_layer is not None),
        grid=(n_tiles,),
        in_specs=in_specs,
        out_specs=out_specs,
        out_shape=out_shape,
        scratch_shapes=scratch,
        compiler_params=_cparams(("parallel",)),
        name="ffn",
    )(*args)


def kernel(x_prompt, x_sample, state_gla, state_hgrn, norm_mix_pre, norm_mix_post, norm_ffn_pre, norm_ffn_post, w_in, gla_w_lr2, gla_b_lr, gla_onorm, hgrn_lb, hgrn_onorm, w_out, w_ffn_in, w_ffn_out):
    B, L, D = x_prompt.shape
    BS, LS, _ = x_sample.shape
    NS = SEQ_PER_ROWGROUP
    assert B == NS and BS % NS == 0 and L % PROMPT_CHUNK == 0
    TP = B * L
    TS = BS * LS
    T = TP + TS
    assert TP % POST_ROWS == 0 and TS % POST_ROWS == 0

    sm = jax.nn.softmax(hgrn_lb.astype(F32), axis=0)
    lb = jnp.clip(jnp.cumsum(sm, axis=0) - sm[0:1], 0.0, 1.0 - 1e-6)
    lbf = jnp.maximum(lb, LB_FLOOR)
    oml = 1.0 - lb

    c_lr0 = GLA_HEADS * (2 * GLA_DK + 2 * GLA_DV)
    c_lr1 = c_lr0 + GLA_GATE_RANK
    w_t = jnp.swapaxes(w_in, 1, 2).astype(BF16)
    w_g = w_t
    w_h = w_t[:, c_lr1:]
    assert w_h.shape[1] == HGRN_HEADS * (2 * HGRN_EXPAND + 2 * HGRN_DV) + 2 * D
    w_lr = jnp.pad(w_t[:, c_lr0:c_lr1], ((0, 0), (0, LANES - GLA_GATE_RANK), (0, 0)))
    w_lr2 = jnp.pad(gla_w_lr2, ((0, 0), (0, LANES - GLA_GATE_RANK), (0, 0))).astype(BF16)
    w_o = w_out.astype(BF16)
    w_ffn = w_ffn_in.astype(BF16)
    w_f = w_ffn_out.astype(BF16)

    P_TIME = POST_ROWS // NS
    S_SEQ = POST_ROWS // LS
    pre = _prenorm_call(x_prompt, norm_mix_pre[0], w_lr, w_lr2, gla_b_lr[0],
                        n_seq=NS, n_time=P_TIME, row0=0, total_rows=T)
    x, xn, g_gla = _prenorm_call(x_sample, norm_mix_pre[0], w_lr, w_lr2, gla_b_lr[0],
                                 n_seq=S_SEQ, n_time=LS, row0=TP, total_rows=T, prev=pre)

    n_sample_groups = BS // NS
    gla_p = gla_s = hg_p = hg_s = None
    for l in range(DEPTH):
        og, gla_p, og_s, gla_s = _recurrence_call(
            xn, w_g, 0, g_gla, state_gla, gla_onorm[l], None, None, layer=l, hgrn=False,
            C=PROMPT_CHUNK, CS=LS, n_chunks=L // PROMPT_CHUNK, sample_row0=TP, n_groups=n_sample_groups,
            heads_per_step=HEADS_PER_STEP, prev_states=gla_p, prev_sample_states=gla_s)
        oh, hg_p, oh_s, hg_s = _recurrence_call(
            xn, w_h, 0, None, state_hgrn, hgrn_onorm[l], lbf[l], oml[l], layer=l, hgrn=True,
            C=PROMPT_CHUNK, CS=LS, n_chunks=L // PROMPT_CHUNK, sample_row0=TP, n_groups=n_sample_groups,
            heads_per_step=HEADS_PER_STEP, prev_states=hg_p, prev_sample_states=hg_s)
        og = lax.dynamic_update_slice(og, og_s, (TP, 0))
        oh = lax.dynamic_update_slice(oh, oh_s, (TP, 0))

        h, hn = _outproj_call(og, oh, xn, x, w_h, w_o, norm_mix_post[l], norm_ffn_pre[l], layer=l, tm=POST_ROWS)
        if l + 1 < DEPTH:
            nxt = (norm_mix_pre[l + 1], w_lr, w_lr2, gla_b_lr[l + 1])
            x, xn, g_gla = _ffn_call(hn, h, w_ffn, w_f, norm_ffn_post[l], nxt, layer=l, tm=POST_ROWS)
        else:
            (y_p,) = _ffn_call(hn, h, w_ffn, w_f, norm_ffn_post[l], None, layer=l, tm=POST_ROWS,
                               final=(0, B, L, NS, P_TIME))
            (y_s,) = _ffn_call(hn, h, w_ffn, w_f, norm_ffn_post[l], None, layer=l, tm=POST_ROWS,
                               final=(TP, BS, LS, S_SEQ, LS))

    return (y_p, y_s, gla_p, gla_s, hg_p, hg_s)
```
